```python
import jax, jax.numpy as jnp
from jax import lax
import numpy as np

D_MODEL = 2048
BATCH = 2
SEQ = 4096
DEPTH = 1
DEC_BATCH = 32
DEC_SEQ = 8
PAST_LEN = 16384
PAGE_SIZE = 128

D_CONV = D_MODEL // 2
CONV_GROUPS = 8
CONV_W = 31
HEAD_DIM = 128
N_HEADS = (D_MODEL // 2) // HEAD_DIM
N_KV_HEADS = 2
D_ATTN = N_HEADS * HEAD_DIM
D_MIX = D_CONV + D_ATTN
KV_W = N_KV_HEADS * HEAD_DIM
N_IDX_HEADS = 16
IDX_DIM = 64
TOPK_MAX = 256
Q_BLOCK = 128
ROPE_THETA = 10000.0
D_FF = 5632
FFN_CONV_W = 3
EPS = 1e-6
IN_SIZES = [D_CONV, D_CONV, D_ATTN, KV_W, KV_W, N_IDX_HEADS * IDX_DIM, IDX_DIM, N_IDX_HEADS]
IN_SPLITS = [sum(IN_SIZES[:i + 1]) for i in range(len(IN_SIZES) - 1)]
N_IN = sum(IN_SIZES)

kernel_name = 'hymba_conformer_dsa_convffn_step'


def _rmsnorm(x, g):
    xf = x.astype(jnp.float32)
    y = xf * lax.rsqrt(jnp.mean(xf * xf, axis=-1, keepdims=True) + EPS)
    return (y * g.astype(jnp.float32)).astype(x.dtype)


def _group_layernorm(x, g, b):
    shp = x.shape
    xf = x.astype(jnp.float32).reshape(shp[:-1] + (CONV_GROUPS, shp[-1] // CONV_GROUPS))
    mu = jnp.mean(xf, axis=-1, keepdims=True)
    var = jnp.mean(jnp.square(xf - mu), axis=-1, keepdims=True)
    y = ((xf - mu) * lax.rsqrt(var + EPS)).reshape(shp)
    return (y * g.astype(jnp.float32) + b.astype(jnp.float32)).astype(x.dtype)


def _rope(x, pos):
    d = x.shape[-1]
    half = d // 2
    inv = jnp.power(ROPE_THETA, -jnp.arange(0, d, 2, dtype=jnp.float32) / d)
    ang = pos.astype(jnp.float32)[:, None] * inv[None, :]
    cos = jnp.cos(ang)[None, :, None, :]
    sin = jnp.sin(ang)[None, :, None, :]
    xf = x.astype(jnp.float32)
    x1, x2 = xf[..., :half], xf[..., half:]
    return jnp.concatenate([x1 * cos - x2 * sin, x2 * cos + x1 * sin], axis=-1).astype(x.dtype)


def _dwconv_valid(x, w, b):
    c = x.shape[-1]
    y = lax.conv_general_dilated(x, w[:, None, :].astype(x.dtype), window_strides=(1,),
                                 padding='VALID', dimension_numbers=('NWC', 'WIO', 'NWC'),
                                 feature_group_count=c)
    return y + b.astype(x.dtype)


def _ada(c, w_ada, b_ada):
    m = jnp.dot(jax.nn.silu(c), w_ada) + b_ada
    return jnp.split(m[:, None, :], 6, axis=-1)


def _project(h, pos, lp):
    B, T, _ = h.shape
    z = jnp.dot(h, lp['w_in'])
    za, zg, zq, zk, zv, zqi, zki, zw = jnp.split(z, IN_SPLITS, axis=-1)
    a = za * jax.nn.sigmoid(zg)
    q = _rope(_rmsnorm(zq.reshape(B, T, N_HEADS, HEAD_DIM), lp['g_q']), pos)
    k = _rope(_rmsnorm(zk.reshape(B, T, N_KV_HEADS, HEAD_DIM), lp['g_k']), pos)
    v = zv.reshape(B, T, N_KV_HEADS, HEAD_DIM)
    qi = _rope(zqi.reshape(B, T, N_IDX_HEADS, IDX_DIM), pos)
    ki = _rope(zki[:, :, None, :], pos)[:, :, 0, :]
    return a, q, k, v, qi, ki, zw


def _index_scores(qi, wi, ki):
    s = jnp.einsum('bthd,bsd->bths', qi, ki, preferred_element_type=jnp.float32) * (IDX_DIM ** -0.5)
    return jnp.einsum('bths,bth->bts', jax.nn.relu(s), wi.astype(jnp.float32)) * (N_IDX_HEADS ** -0.5)


def _sparse_attend(q, k_sel, v_sel, valid):
    B, T = q.shape[:2]
    qg = q.reshape(B, T, N_KV_HEADS, N_HEADS // N_KV_HEADS, HEAD_DIM)
    logits = jnp.einsum('btkgd,btjkd->btkgj', qg, k_sel,
                        preferred_element_type=jnp.float32) * (HEAD_DIM ** -0.5)
    logits = jnp.where(valid[:, :, None, None, :], logits, -jnp.inf)
    p = jax.nn.softmax(logits, axis=-1)
    o = jnp.einsum('btkgj,btjkd->btkgd', p.astype(v_sel.dtype), v_sel)
    return o.reshape(B, T, N_HEADS, HEAD_DIM)


def _conv_branch(a_hist, lp):
    y = _dwconv_valid(a_hist, lp['w_dw_a'], lp['b_dw_a'])
    return jax.nn.silu(_group_layernorm(y, lp['gn_g'], lp['gn_b']))


def _merge(conv_o, attn_o, lp):
    B, T = conv_o.shape[:2]
    o = _rmsnorm(attn_o, lp['g_o']).reshape(B, T, D_ATTN)
    return jnp.dot(jnp.concatenate([conv_o, o], axis=-1), lp['w_out'])


def _ffn(u_hist, lp):
    uc = _dwconv_valid(u_hist, lp['w_dw_f'], lp['b_dw_f'])
    gate, val = jnp.split(uc, 2, axis=-1)
    return jnp.dot(jax.nn.silu(gate) * val, lp['w_down'])


def _prompt_layer(x, c, lp):
    B, S, _ = x.shape
    sh1, sc1, gt1, sh2, sc2, gt2 = _ada(c, lp['w_ada'], lp['b_ada'])
    h = _rmsnorm(x, lp['norm1']) * (1 + sc1) + sh1
    pos = jnp.arange(S, dtype=jnp.int32)
    a, q, k, v, qi, ki, wi = _project(h, pos, lp)
    a_hist = jnp.pad(a, ((0, 0), (CONV_W - 1, 0), (0, 0)))
    conv_o = _conv_branch(a_hist, lp)
    n_sel = min(TOPK_MAX, S // 4)
    qb = min(Q_BLOCK, S)
    key_pos = jnp.arange(S, dtype=jnp.int32)
    bidx = jnp.arange(B)[:, None, None]

    def block(i):
        start = i * qb
        qpos = start + jnp.arange(qb, dtype=jnp.int32)
        sl = lambda t: lax.dynamic_slice_in_dim(t, start, qb, axis=1)
        sc = _index_scores(sl(qi), sl(wi), ki)
        causal = key_pos[None, None, :] <= qpos[None, :, None]
        _, idx = lax.top_k(jnp.where(causal, sc, -jnp.inf), n_sel)
        valid = idx <= qpos[None, :, None]
        return _sparse_attend(sl(q), k[bidx, idx], v[bidx, idx], valid)

    o = lax.map(block, jnp.arange(S // qb))
    o = jnp.moveaxis(o, 0, 1).reshape(B, S, N_HEADS, HEAD_DIM)
    x1 = x + gt1 * _merge(conv_o, o, lp)
    h2 = _rmsnorm(x1, lp['norm2']) * (1 + sc2) + sh2
    u = jnp.dot(h2, lp['w_up'])
    u_hist = jnp.pad(u, ((0, 0), (FFN_CONV_W - 1, 0), (0, 0)))
    y = x1 + gt2 * _ffn(u_hist, lp)
    return y, (k, v, ki, a_hist[:, -(CONV_W - 1):], u_hist[:, -(FFN_CONV_W - 1):])


def _sample_layer(x, c, ck, cv, cki, s_conv, s_ffn, page_table, lp):
    B, T, _ = x.shape
    n_pages = page_table.shape[1]
    page = ck.shape[1]
    past = n_pages * page
    sh1, sc1, gt1, sh2, sc2, gt2 = _ada(c, lp['w_ada'], lp['b_ada'])
    h = _rmsnorm(x, lp['norm1']) * (1 + sc1) + sh1
    pos = past + jnp.arange(T, dtype=jnp.int32)
    a, q, k, v, qi, ki, wi = _project(h, pos, lp)
    a_hist = jnp.concatenate([s_conv.astype(a.dtype), a], axis=1)
    conv_o = _conv_branch(a_hist, lp)
    L = past + T
    ki_all = jnp.concatenate([cki[page_table].reshape(B, past, IDX_DIM).astype(ki.dtype), ki], axis=1)
    sc = _index_scores(qi, wi, ki_all)
    causal = jnp.arange(L, dtype=jnp.int32)[None, None, :] <= pos[None, :, None]
    n_sel = min(TOPK_MAX, L // 4)
    _, idx = lax.top_k(jnp.where(causal, sc, -jnp.inf), n_sel)
    valid = idx <= pos[None, :, None]
    bidx = jnp.arange(B)[:, None, None]
    pidx = jnp.minimum(idx, past - 1)
    phys = page_table[bidx, pidx // page]
    off = pidx % page
    nidx = jnp.clip(idx - past, 0, T - 1)
    in_past = (idx < past)[..., None, None]
    k_sel = jnp.where(in_past, ck[phys, off].astype(k.dtype), k[bidx, nidx])
    v_sel = jnp.where(in_past, cv[phys, off].astype(v.dtype), v[bidx, nidx])
    o = _sparse_attend(q, k_sel, v_sel, valid)
    x1 = x + gt1 * _merge(conv_o, o, lp)
    h2 = _rmsnorm(x1, lp['norm2']) * (1 + sc2) + sh2
    u = jnp.dot(h2, lp['w_up'])
    u_hist = jnp.concatenate([s_ffn.astype(u.dtype), u], axis=1)
    y = x1 + gt2 * _ffn(u_hist, lp)
    return y, (k, v, ki, a_hist[:, -(CONV_W - 1):], u_hist[:, -(FFN_CONV_W - 1):])


def setup_inputs(seed: int = 0) -> dict:
    key = jax.random.key(seed)
    ks = jax.random.split(key, 32)
    f32 = jnp.float32
    nrm = lambda k, shp, s: jax.random.normal(k, shp, f32) * s
    n_pages = PAST_LEN // PAGE_SIZE
    n_used = DEC_BATCH * n_pages
    n_pool = n_used + max(1, n_used // 4)
    page_table = jax.random.permutation(ks[0], n_pool)[:n_used].reshape(DEC_BATCH, n_pages).astype(jnp.int32)
    D = D_MODEL
    return {
        'x_prompt': nrm(ks[1], (BATCH, SEQ, D), 1.0),
        'x_sample': nrm(ks[2], (DEC_BATCH, DEC_SEQ, D), 1.0),
        'cache_k': nrm(ks[3], (DEPTH, n_pool, PAGE_SIZE, N_KV_HEADS, HEAD_DIM), 1.0),
        'cache_v': nrm(ks[4], (DEPTH, n_pool, PAGE_SIZE, N_KV_HEADS, HEAD_DIM), 1.0),
        'cache_kidx': nrm(ks[5], (DEPTH, n_pool, PAGE_SIZE, IDX_DIM), 1.0),
        'state_conv': nrm(ks[6], (DEPTH, DEC_BATCH, CONV_W - 1, D_CONV), 0.5),
        'state_ffn': nrm(ks[7], (DEPTH, DEC_BATCH, FFN_CONV_W - 1, 2 * D_FF), 1.0),
        'page_table': page_table,
        'c_prompt': nrm(ks[8], (BATCH, D), 1.0),
        'c_sample': nrm(ks[9], (DEC_BATCH, D), 1.0),
        'norm1': 1.0 + nrm(ks[10], (DEPTH, D), 0.02),
        'w_ada': nrm(ks[11], (DEPTH, D, 6 * D), 0.5 * D ** -0.5),
        'b_ada': nrm(ks[12], (DEPTH, 6 * D), 0.02),
        'w_in': nrm(ks[13], (DEPTH, D, N_IN), D ** -0.5),
        'g_q': 1.0 + nrm(ks[14], (DEPTH, HEAD_DIM), 0.02),
        'g_k': 1.0 + nrm(ks[15], (DEPTH, HEAD_DIM), 0.02),
        'w_dw_a': nrm(ks[16], (DEPTH, CONV_W, D_CONV), CONV_W ** -0.5),
        'b_dw_a': nrm(ks[17], (DEPTH, D_CONV), 0.02),
        'gn_g': 1.0 + nrm(ks[18], (DEPTH, D_CONV), 0.02),
        'gn_b': nrm(ks[19], (DEPTH, D_CONV), 0.02),
        'g_o': 1.0 + nrm(ks[20], (DEPTH, HEAD_DIM), 0.02),
        'w_out': nrm(ks[21], (DEPTH, D_MIX, D), D_MIX ** -0.5),
        'norm2': 1.0 + nrm(ks[22], (DEPTH, D), 0.02),
        'w_up': nrm(ks[23], (DEPTH, D, 2 * D_FF), D ** -0.5),
        'w_dw_f': nrm(ks[24], (DEPTH, FFN_CONV_W, 2 * D_FF), FFN_CONV_W ** -0.5),
        'b_dw_f': nrm(ks[25], (DEPTH, 2 * D_FF), 0.02),
        'w_down': nrm(ks[26], (DEPTH, D_FF, D), D_FF ** -0.5),
    }


def reference(x_prompt, x_sample, cache_k, cache_v, cache_kidx, state_conv, state_ffn, page_table,
              c_prompt, c_sample, norm1, w_ada, b_ada, w_in, g_q, g_k, w_dw_a, b_dw_a, gn_g, gn_b,
              g_o, w_out, norm2, w_up, w_dw_f, b_dw_f, w_down):
    hp, hs = x_prompt, x_sample
    sp = ([], [], [], [], [])
    ss = ([], [], [], [], [])
    for l in range(DEPTH):
        lp = {'norm1': norm1[l], 'w_ada': w_ada[l], 'b_ada': b_ada[l], 'w_in': w_in[l],
              'g_q': g_q[l], 'g_k': g_k[l], 'w_dw_a': w_dw_a[l], 'b_dw_a': b_dw_a[l],
              'gn_g': gn_g[l], 'gn_b': gn_b[l], 'g_o': g_o[l], 'w_out': w_out[l],
              'norm2': norm2[l], 'w_up': w_up[l], 'w_dw_f': w_dw_f[l], 'b_dw_f': b_dw_f[l],
              'w_down': w_down[l]}
        hp, st_p = _prompt_layer(hp, c_prompt, lp)
        hs, st_s = _sample_layer(hs, c_sample, cache_k[l], cache_v[l], cache_kidx[l],
                                 state_conv[l], state_ffn[l], page_table, lp)
        for lst, t in zip(sp, st_p):
            lst.append(t)
        for lst, t in zip(ss, st_s):
            lst.append(t)
    new_k_p = jnp.stack(sp[0])
    new_v_p = jnp.stack(sp[1])
    new_kidx_p = jnp.stack(sp[2])
    new_conv_p = jnp.stack(sp[3])
    new_ffn_p = jnp.stack(sp[4])
    new_k_s = jnp.stack(ss[0])
    new_v_s = jnp.stack(ss[1])
    new_kidx_s = jnp.stack(ss[2])
    new_conv_s = jnp.stack(ss[3])
    new_ffn_s = jnp.stack(ss[4])
    return (hp, hs, new_k_p, new_v_p, new_kidx_p, new_conv_p, new_ffn_p,
            new_k_s, new_v_s, new_kidx_s, new_conv_s, new_ffn_s)
```

```python
import functools

import jax
import jax.numpy as jnp
from jax import lax
from jax.experimental import pallas as pl
from jax.experimental.pallas import tpu as pltpu

F32 = jnp.float32
BF16 = jnp.bfloat16
I32 = jnp.int32

CONV_GROUPS = 8
N_KV_HEADS = 2
N_IDX_HEADS = 16
IDX_DIM = 64
TOPK_MAX = 256
ROPE_THETA = 10000.0
EPS = 1e-6
IDX_SCALE = (IDX_DIM ** -0.5) * (N_IDX_HEADS ** -0.5)

LANES = 128
SUBLANES = 8
QB = 128
VMEM_LIMIT_BYTES = 56 * 1024 * 1024
NEG_BIG = -1e30
INT_MIN = -2 ** 31
INT_MAX = 2 ** 31 - 1
KEY_NEG_INF = -2139095041


def _cparams(*sem):
    return pltpu.CompilerParams(dimension_semantics=sem, vmem_limit_bytes=VMEM_LIMIT_BYTES)


def _silu(x):
    return x * jax.nn.sigmoid(x)


def _order_key(x):
    bits = pltpu.bitcast(x, I32)
    return bits ^ ((bits >> 31) & INT_MAX)


def _ada_kernel(c_ref, w_ref, b_ref, o_ref):
    s = _silu(c_ref[...]).astype(BF16)
    o_ref[...] = jnp.dot(s, w_ref[...].astype(BF16), preferred_element_type=F32) + b_ref[...]


def _ada(c_all, w_ada, b_ada, tn=1024):
    r, d = c_all.shape
    n = w_ada.shape[1]
    return pl.pallas_call(
        _ada_kernel,
        grid=(n // tn,),
        in_specs=[pl.BlockSpec((r, d), lambda j: (0, 0)),
                  pl.BlockSpec((d, tn), lambda j: (0, j)),
                  pl.BlockSpec((1, tn), lambda j: (0, j))],
        out_specs=pl.BlockSpec((r, tn), lambda j: (0, j)),
        out_shape=jax.ShapeDtypeStruct((r, n), F32),
        compiler_params=_cparams("arbitrary"),
        name="ada",
    )(c_all, w_ada, b_ada.reshape(1, n))


def _normmod_matmul_kernel(x_ref, g_ref, sc_ref, sh_ref, w_ref, o_ref, h_ref):
    @pl.when(pl.program_id(1) == 0)
    def _():
        x = x_ref[...]
        y = x * lax.rsqrt(jnp.mean(x * x, axis=-1, keepdims=True) + EPS) * g_ref[...]
        h_ref[...] = (y * (1.0 + sc_ref[0]) + sh_ref[0]).astype(BF16)

    o_ref[...] = jnp.dot(h_ref[...], w_ref[...].astype(BF16), preferred_element_type=F32)


def _normmod_matmul(x, gain, sc, sh, w, ncols, tm, tn, name):
    m, d = x.shape
    g, r, _ = sc.shape
    tiles_per_group = (m // tm) // g
    mod_spec = pl.BlockSpec((1, r, d), lambda i, j: (i // tiles_per_group, 0, 0))
    return pl.pallas_call(
        _normmod_matmul_kernel,
        grid=(m // tm, ncols // tn),
        in_specs=[pl.BlockSpec((tm, d), lambda i, j: (i, 0)),
                  pl.BlockSpec((1, d), lambda i, j: (0, 0)),
                  mod_spec, mod_spec,
                  pl.BlockSpec((d, tn), lambda i, j: (0, j))],
        out_specs=pl.BlockSpec((tm, tn), lambda i, j: (i, j)),
        out_shape=jax.ShapeDtypeStruct((m, ncols), F32),
        scratch_shapes=[pltpu.VMEM((tm, d), BF16)],
        compiler_params=_cparams("arbitrary", "arbitrary"),
        name=name,
    )(x, gain.reshape(1, d), sc, sh, w)


def _conv_gn_silu(hist_ref, off, rows, wdw_ref, bdw_ref, gng_ref, gnb_ref, o_ref, row_chunk):
    conv_w = wdw_ref.shape[0]
    for r0 in range(0, rows, row_chunk):
        for c in range(CONV_GROUPS):
            cs = slice(c * LANES, (c + 1) * LANES)
            acc = jnp.zeros((row_chunk, LANES), F32)
            for w in range(conv_w):
                acc = acc + hist_ref[off + r0 + w:off + r0 + w + row_chunk, cs] * wdw_ref[w:w + 1, cs]
            y = acc + bdw_ref[:, cs]
            mu = jnp.mean(y, axis=-1, keepdims=True)
            dlt = y - mu
            var = jnp.mean(dlt * dlt, axis=-1, keepdims=True)
            yn = dlt * lax.rsqrt(var + EPS) * gng_ref[:, cs] + gnb_ref[:, cs]
            o_ref[r0:r0 + row_chunk, cs] = _silu(yn).astype(o_ref.dtype)


HALO = 32


def _conv_prompt_kernel(za_ref, zg_ref, ha_ref, hg_ref, wdw_ref, bdw_ref, gng_ref, gnb_ref,
                        o_ref, newc_ref, hist_ref, *, tm, tiles_per_seq):
    i = pl.program_id(0)
    first = (i % tiles_per_seq) == 0
    a_halo = ha_ref[...] * jax.nn.sigmoid(hg_ref[...])
    hist_ref[0:HALO, :] = jnp.where(first, 0.0, a_halo)
    hist_ref[HALO:HALO + tm, :] = za_ref[...] * jax.nn.sigmoid(zg_ref[...])
    conv_w = wdw_ref.shape[0]
    _conv_gn_silu(hist_ref, HALO - (conv_w - 1), tm, wdw_ref, bdw_ref, gng_ref, gnb_ref, o_ref, 64)

    @pl.when((i % tiles_per_seq) == tiles_per_seq - 1)
    def _():
        newc_ref[0] = hist_ref[HALO + tm - (conv_w - 1):HALO + tm, :]


def _conv_prompt(z, n_seq, seq, d_conv, wdw, bdw, gng, gnb, tm=256):
    m = z.shape[0]
    conv_w = wdw.shape[0]
    tiles_per_seq = seq // tm
    cb = d_conv // d_conv
    halo_idx = lambda i: jnp.maximum(i * (tm // HALO) - 1, 0)
    vec = pl.BlockSpec((1, d_conv), lambda i: (0, 0))
    return pl.pallas_call(
        functools.partial(_conv_prompt_kernel, tm=tm, tiles_per_seq=tiles_per_seq),
        grid=(m // tm,),
        in_specs=[pl.BlockSpec((tm, d_conv), lambda i: (i, 0)),
                  pl.BlockSpec((tm, d_conv), lambda i: (i, cb)),
                  pl.BlockSpec((HALO, d_conv), lambda i: (halo_idx(i), 0)),
                  pl.BlockSpec((HALO, d_conv), lambda i: (halo_idx(i), cb)),
                  pl.BlockSpec((conv_w, d_conv), lambda i: (0, 0)),
                  vec, vec, vec],
        out_specs=[pl.BlockSpec((tm, d_conv), lambda i: (i, 0)),
                   pl.BlockSpec((1, conv_w - 1, d_conv), lambda i: (i // tiles_per_seq, 0, 0))],
        out_shape=[jax.ShapeDtypeStruct((m, d_conv), BF16),
                   jax.ShapeDtypeStruct((n_seq, conv_w - 1, d_conv), F32)],
        scratch_shapes=[pltpu.VMEM((HALO + tm, d_conv), F32)],
        compiler_params=_cparams("arbitrary"),
        name="conv_prompt",
    )(z, z, z, z, wdw, bdw.reshape(1, -1), gng.reshape(1, -1), gnb.reshape(1, -1))


def _conv_sample_kernel(za_ref, zg_ref, st_ref, wdw_ref, bdw_ref, gng_ref, gnb_ref,
                        o_ref, newc_ref, hist_ref, *, t_new):
    conv_w = wdw_ref.shape[0]
    hist_ref[0:conv_w - 1, :] = st_ref[0]
    hist_ref[conv_w - 1:conv_w - 1 + t_new, :] = za_ref[...] * jax.nn.sigmoid(zg_ref[...])
    _conv_gn_silu(hist_ref, 0, t_new, wdw_ref, bdw_ref, gng_ref, gnb_ref, o_ref, t_new)
    newc_ref[0] = hist_ref[t_new:t_new + conv_w - 1, :]


def _conv_sample(z, state, d_conv, wdw, bdw, gng, gnb):
    n_seq, hist_rows, _ = state.shape
    conv_w = wdw.shape[0]
    m = z.shape[0]
    t_new = m // n_seq
    vec = pl.BlockSpec((1, d_conv), lambda b: (0, 0))
    return pl.pallas_call(
        functools.partial(_conv_sample_kernel, t_new=t_new),
        grid=(n_seq,),
        in_specs=[pl.BlockSpec((t_new, d_conv), lambda b: (b, 0)),
                  pl.BlockSpec((t_new, d_conv), lambda b: (b, 1)),
                  pl.BlockSpec((1, hist_rows, d_conv), lambda b: (b, 0, 0)),
                  pl.BlockSpec((conv_w, d_conv), lambda b: (0, 0)),
                  vec, vec, vec],
        out_specs=[pl.BlockSpec((t_new, d_conv), lambda b: (b, 0)),
                   pl.BlockSpec((1, hist_rows, d_conv), lambda b: (b, 0, 0))],
        out_shape=[jax.ShapeDtypeStruct((m, d_conv), F32),
                   jax.ShapeDtypeStruct((n_seq, hist_rows, d_conv), F32)],
        scratch_shapes=[pltpu.VMEM((hist_rows + t_new + SUBLANES, d_conv), F32)],
        compiler_params=_cparams("arbitrary"),
        name="conv_sample",
    )(z, z, state, wdw, bdw.reshape(1, -1), gng.reshape(1, -1), gnb.reshape(1, -1))


def _qk_kernel(zq_ref, zkv_ref, zqi0_ref, zqi1_ref, zt_ref, cos_ref, sin_ref, cosi_ref, sini_ref,
               gq_ref, gk_ref, qhm_ref, qihm_ref, k_ref, kbf_ref, vt_ref, ki_ref, kibf_ref, *, n_heads):
    cos, sin = cos_ref[...], sin_ref[...]
    cosi, sini = cosi_ref[...], sini_ref[...]
    tm = cos.shape[0]
    lane = lax.broadcasted_iota(I32, (tm, LANES), 1)
    low_half = (lane % IDX_DIM) < (IDX_DIM // 2)

    def norm_rope(x, g):
        y = x * lax.rsqrt(jnp.mean(x * x, axis=-1, keepdims=True) + EPS) * g
        return y * cos + pltpu.roll(y, LANES // 2, 1) * sin

    def rope_idx(x):
        r = jnp.where(low_half, pltpu.roll(x, LANES - IDX_DIM // 2, 1), pltpu.roll(x, IDX_DIM // 2, 1))
        return x * cosi + r * sini

    gq, gk = gq_ref[...], gk_ref[...]
    for h in range(n_heads):
        qhm_ref[0, h] = norm_rope(zq_ref[:, h * LANES:(h + 1) * LANES], gq).astype(BF16)
    kv_w = N_KV_HEADS * LANES
    for g in range(N_KV_HEADS):
        kg = norm_rope(zkv_ref[:, g * LANES:(g + 1) * LANES], gk)
        k_ref[:, g * LANES:(g + 1) * LANES] = kg
        kbf_ref[:, g * LANES:(g + 1) * LANES] = kg.astype(BF16)
    vt_ref[0] = zkv_ref[:, kv_w:2 * kv_w].T.astype(BF16)
    half = (N_IDX_HEADS * IDX_DIM) // 2
    for j in range(N_IDX_HEADS // 2):
        src = zqi0_ref if j * LANES < half else zqi1_ref
        c0 = (j * LANES) % half
        y = rope_idx(src[:, c0:c0 + LANES])
        qihm_ref[0, 2 * j] = y[:, :IDX_DIM].astype(BF16)
        qihm_ref[0, 2 * j + 1] = y[:, IDX_DIM:].astype(BF16)
    yk = rope_idx(zt_ref[...])[:, :IDX_DIM]
    ki_ref[...] = yk
    kibf_ref[...] = yk.astype(BF16)


def _qk_epilogue(z, zt, tabs, gq, gk, n_seq, seq, n_heads):
    m = z.shape[0]
    tm = QB
    head_dim = LANES
    cos, sin, cosi, sini = tabs
    tab_tiles = cos.shape[0] // tm
    kv_w = N_KV_HEADS * head_dim
    d_attn = n_heads * head_dim
    d_conv = d_attn
    q_cb = (2 * d_conv) // d_attn
    kv_cb = (2 * d_conv + d_attn) // (2 * kv_w)
    qi_w = (N_IDX_HEADS * IDX_DIM) // 2
    qi_cb = (2 * d_conv + d_attn + 2 * kv_w) // qi_w
    tiles_per_seq = seq // tm
    tab = pl.BlockSpec((tm, LANES), lambda i: (i % tab_tiles, 0))
    vec = pl.BlockSpec((1, LANES), lambda i: (0, 0))
    return pl.pallas_call(
        functools.partial(_qk_kernel, n_heads=n_heads),
        grid=(m // tm,),
        in_specs=[pl.BlockSpec((tm, d_attn), lambda i: (i, q_cb)),
                  pl.BlockSpec((tm, 2 * kv_w), lambda i: (i, kv_cb)),
                  pl.BlockSpec((tm, qi_w), lambda i: (i, qi_cb)),
                  pl.BlockSpec((tm, qi_w), lambda i: (i, qi_cb + 1)),
                  pl.BlockSpec((tm, LANES), lambda i: (i, 0)),
                  tab, tab, tab, tab, vec, vec],
        out_specs=[pl.BlockSpec((1, n_heads, tm, head_dim), lambda i: (i, 0, 0, 0)),
                   pl.BlockSpec((1, N_IDX_HEADS, tm, IDX_DIM), lambda i: (i, 0, 0, 0)),
                   pl.BlockSpec((tm, kv_w), lambda i: (i, 0)),
                   pl.BlockSpec((tm, kv_w), lambda i: (i, 0)),
                   pl.BlockSpec((1, kv_w, tm), lambda i: (i // tiles_per_seq, 0, i % tiles_per_seq)),
                   pl.BlockSpec((tm, IDX_DIM), lambda i: (i, 0)),
                   pl.BlockSpec((tm, IDX_DIM), lambda i: (i, 0))],
        out_shape=[jax.ShapeDtypeStruct((m // tm, n_heads, tm, head_dim), BF16),
                   jax.ShapeDtypeStruct((m // tm, N_IDX_HEADS, tm, IDX_DIM), BF16),
                   jax.ShapeDtypeStruct((m, kv_w), F32),
                   jax.ShapeDtypeStruct((m, kv_w), BF16),
                   jax.ShapeDtypeStruct((n_seq, kv_w, seq), BF16),
                   jax.ShapeDtypeStruct((m, IDX_DIM), F32),
                   jax.ShapeDtypeStruct((m, IDX_DIM), BF16)],
        compiler_params=_cparams("arbitrary"),
        name="qk_epilogue",
    )(z, z, z, z, zt, cos, sin, cosi, sini, gq.reshape(1, -1), gk.reshape(1, -1))


def _attn_prompt_kernel(qi_ref, w_ref, ki_ref, q_ref, k_ref, vt_ref, o_ref,
                        key_ref, m_ref, l_ref, acc_ref, cut_ref, *, n_sel, scale, n_heads, idx_bits):
    i = pl.program_id(1)
    n_kt = i + 1
    row_iota = lax.broadcasted_iota(I32, (QB, QB), 0)
    t_pos = i * QB + lax.broadcasted_iota(I32, (QB, QB), 1)
    w_all = w_ref[0] * IDX_SCALE
    heads_per_kv = n_heads // N_KV_HEADS
    hc = 4

    def score_tile(kt, carry):
        ks = pl.multiple_of(kt * QB, QB)
        ki_t = ki_ref[0, pl.ds(ks, QB), :]
        acc = jnp.zeros((QB, QB), F32)
        for h0 in range(0, N_IDX_HEADS, hc):
            s = lax.dot_general(ki_t, qi_ref[0, h0:h0 + hc].reshape(hc * QB, IDX_DIM),
                                (((1,), (1,)), ((), ())), preferred_element_type=F32)
            for h in range(hc):
                acc = acc + jnp.maximum(s[:, h * QB:(h + 1) * QB], 0.0) * w_all[h0 + h:h0 + h + 1, :]
        acc = jnp.where(ks + row_iota <= t_pos, acc, -jnp.inf)
        key_ref[pl.ds(ks, QB), :] = _order_key(acc)
        return carry

    lax.fori_loop(0, n_kt, score_tile, 0)

    def count(pred):
        def body(kt, cnt):
            ks = pl.multiple_of(kt * QB, QB)
            hit = jnp.where(pred(key_ref[pl.ds(ks, QB), :], ks + row_iota), 1, 0)
            return cnt + jnp.sum(hit.reshape(QB // SUBLANES, SUBLANES, QB), axis=0)
        cnt = lax.fori_loop(0, n_kt, body, jnp.zeros((SUBLANES, QB), I32))
        return jnp.sum(cnt, axis=0, keepdims=True)

    def search(it, thr):
        cand = thr + lax.shift_left(jnp.int32(1), 31 - it)
        c = count(lambda kk, pos: kk >= cand)
        return jnp.where(c >= n_sel, cand, thr)

    thr = lax.fori_loop(0, 32, search, jnp.full((1, QB), INT_MIN, I32))
    cnt_ge = count(lambda kk, pos: kk >= thr)
    need = n_sel - count(lambda kk, pos: kk > thr)
    cut_ref[...] = jnp.full((1, QB), INT_MAX, I32)

    @pl.when(jnp.max(cnt_ge) > n_sel)
    def _():
        def search_pos(it, p):
            cand = p + lax.shift_left(jnp.int32(1), idx_bits - 1 - it)
            c = count(lambda kk, pos: jnp.where(kk == thr, pos, INT_MAX) < cand)
            return jnp.where(c < need, cand, p)
        cut_ref[...] = lax.fori_loop(0, idx_bits, search_pos, jnp.zeros((1, QB), I32))

    few = thr <= KEY_NEG_INF
    thr_eff = jnp.where(few, KEY_NEG_INF + 1, thr)
    cut = jnp.where(few, INT_MAX, cut_ref[...])

    m_ref[...] = jnp.full(m_ref.shape, NEG_BIG, F32)
    l_ref[...] = jnp.zeros(l_ref.shape, F32)
    acc_ref[...] = jnp.zeros(acc_ref.shape, F32)

    def attn_tile(kt, carry):
        ks = pl.multiple_of(kt * QB, QB)
        kk = key_ref[pl.ds(ks, QB), :]
        sel = (kk - jnp.where(ks + row_iota > cut, 1, 0)) >= thr_eff
        for g in range(N_KV_HEADS):
            k_t = k_ref[0, pl.ds(ks, QB), g * LANES:(g + 1) * LANES]
            v_t = vt_ref[0, g * LANES:(g + 1) * LANES, pl.ds(ks, QB)]
            q_g = q_ref[0, g * heads_per_kv:(g + 1) * heads_per_kv].reshape(heads_per_kv * QB, LANES)
            lg = lax.dot_general(k_t, q_g, (((1,), (1,)), ((), ())),
                                 preferred_element_type=F32) * scale
            for hh in range(heads_per_kv):
                cs = slice((g * heads_per_kv + hh) * QB, (g * heads_per_kv + hh + 1) * QB)
                lgh = jnp.where(sel, lg[:, hh * QB:(hh + 1) * QB], NEG_BIG)
                m_old = m_ref[:, cs]
                m_new = jnp.maximum(m_old, jnp.max(lgh, axis=0, keepdims=True))
                p = jnp.where(sel, jnp.exp(lgh - m_new), 0.0)
                alpha = jnp.exp(m_old - m_new)
                l_ref[:, cs] = alpha * l_ref[:, cs] + jnp.sum(p, axis=0, keepdims=True)
                m_ref[:, cs] = m_new
                acc_ref[:, cs] = alpha * acc_ref[:, cs] + jnp.dot(v_t, p.astype(BF16),
                                                                   preferred_element_type=F32)
        return carry

    lax.fori_loop(0, n_kt, attn_tile, 0)
    for h in range(n_heads):
        cs = slice(h * QB, (h + 1) * QB)
        o_ref[:, cs] = (acc_ref[:, cs] / l_ref[:, cs]).T


def _attn_prompt(qi_hm, w_hm, ki_bf, q_hm, k_bf, vt_bf, n_seq, seq, n_sel):
    n_heads = q_hm.shape[1]
    head_dim = q_hm.shape[3]
    nblk = seq // QB
    kv_w = k_bf.shape[-1]
    blk = lambda b, i: (b * nblk + i, 0, 0, 0)
    return pl.pallas_call(
        functools.partial(_attn_prompt_kernel, n_sel=n_sel, scale=head_dim ** -0.5, n_heads=n_heads,
                          idx_bits=int(seq).bit_length()),
        grid=(n_seq, nblk),
        in_specs=[pl.BlockSpec((1, N_IDX_HEADS, QB, IDX_DIM), blk),
                  pl.BlockSpec((1, N_IDX_HEADS, QB), lambda b, i: (b * nblk + i, 0, 0)),
                  pl.BlockSpec((1, seq, IDX_DIM), lambda b, i: (b, 0, 0)),
                  pl.BlockSpec((1, n_heads, QB, head_dim), blk),
                  pl.BlockSpec((1, seq, kv_w), lambda b, i: (b, 0, 0)),
                  pl.BlockSpec((1, kv_w, seq), lambda b, i: (b, 0, 0))],
        out_specs=pl.BlockSpec((QB, n_heads * head_dim), lambda b, i: (b * nblk + i, 0)),
        out_shape=jax.ShapeDtypeStruct((n_seq * seq, n_heads * head_dim), F32),
        scratch_shapes=[pltpu.VMEM((seq, QB), I32),
                        pltpu.VMEM((1, n_heads * QB), F32),
                        pltpu.VMEM((1, n_heads * QB), F32),
                        pltpu.VMEM((head_dim, n_heads * QB), F32),
                        pltpu.VMEM((1, QB), I32)],
        compiler_params=_cparams("arbitrary", "arbitrary"),
        name="attn_prompt",
    )(qi_hm, w_hm, ki_bf.reshape(n_seq, seq, IDX_DIM), q_hm, k_bf.reshape(n_seq, seq, kv_w), vt_bf)


SCORE_PAGES = 16
ATTN_PAGES = 8


def _sample_score_kernel(pt_ref, qi_ref, w_ref, kin_ref, *refs, t_new):
    pages, (sp_ref, sn_ref) = refs[:SCORE_PAGES], refs[SCORE_PAGES:]
    page = pages[0].shape[1]
    qi = qi_ref[0]
    rows = qi.shape[0]
    wb = jnp.broadcast_to(w_ref[0] * IDX_SCALE, (rows, page))

    def score(keys_bf):
        s = lax.dot_general(qi, keys_bf, (((1,), (1,)), ((), ())), preferred_element_type=F32)
        r = jnp.maximum(s, 0.0) * wb
        return jnp.sum(r.reshape(N_IDX_HEADS, t_new, page), axis=0)

    for j in range(SCORE_PAGES):
        sp_ref[0, :, j * page:(j + 1) * page] = score(pages[j][0].astype(BF16))

    @pl.when(pl.program_id(1) == 0)
    def _():
        sn = score(kin_ref[0])
        s_idx = lax.broadcasted_iota(I32, (t_new, page), 1)
        t_idx = lax.broadcasted_iota(I32, (t_new, page), 0)
        sn_ref[0] = jnp.where(s_idx <= t_idx, sn, -jnp.inf)


def _sample_scores(page_table, qi_s, w_s, ki_new, cache_kidx, t_new):
    n_seq, n_pages = page_table.shape
    page = cache_kidx.shape[1]
    rows = qi_s.shape[1]
    page_specs = [pl.BlockSpec((1, page, IDX_DIM),
                               functools.partial(lambda b, c, pt, j: (pt[b, c * SCORE_PAGES + j], 0, 0), j=j))
                  for j in range(SCORE_PAGES)]
    grid_spec = pltpu.PrefetchScalarGridSpec(
        num_scalar_prefetch=1,
        grid=(n_seq, n_pages // SCORE_PAGES),
        in_specs=[pl.BlockSpec((1, rows, IDX_DIM), lambda b, c, pt: (b, 0, 0)),
                  pl.BlockSpec((1, rows, 1), lambda b, c, pt: (b, 0, 0)),
                  pl.BlockSpec((1, page, IDX_DIM), lambda b, c, pt: (b, 0, 0))] + page_specs,
        out_specs=[pl.BlockSpec((1, t_new, SCORE_PAGES * page), lambda b, c, pt: (b, 0, c)),
                   pl.BlockSpec((1, t_new, page), lambda b, c, pt: (b, 0, 0))],
    )
    return pl.pallas_call(
        functools.partial(_sample_score_kernel, t_new=t_new),
        grid_spec=grid_spec,
        out_shape=[jax.ShapeDtypeStruct((n_seq, t_new, n_pages * page), F32),
                   jax.ShapeDtypeStruct((n_seq, t_new, page), F32)],
        compiler_params=_cparams("arbitrary", "arbitrary"),
        name="sample_scores",
    )(page_table, qi_s, w_s, ki_new, *([cache_kidx] * SCORE_PAGES))


def _sample_thresh_kernel(sp_ref, sn_ref, thr_ref, cut_ref, key_ref, *, n_sel, n_past, idx_bits):
    rows = sp_ref.shape[0]
    n_tiles = n_past // LANES + 1
    key_ref[:, 0:n_past] = _order_key(sp_ref[...])
    key_ref[:, n_past:n_past + LANES] = _order_key(sn_ref[...])
    lane = lax.broadcasted_iota(I32, (rows, LANES), 1)

    def count(pred):
        def body(j, cnt):
            c0 = pl.multiple_of(j * LANES, LANES)
            return cnt + jnp.where(pred(key_ref[:, pl.ds(c0, LANES)], c0 + lane), 1, 0)
        cnt = lax.fori_loop(0, n_tiles, body, jnp.zeros((rows, LANES), I32))
        return jnp.broadcast_to(jnp.sum(cnt, axis=1, keepdims=True), (rows, LANES))

    def search(it, thr):
        cand = thr + lax.shift_left(jnp.int32(1), 31 - it)
        c = count(lambda kk, pos: kk >= cand)
        return jnp.where(c >= n_sel, cand, thr)

    thr = lax.fori_loop(0, 32, search, jnp.full((rows, LANES), INT_MIN, I32))
    cnt_ge = count(lambda kk, pos: kk >= thr)
    need = n_sel - count(lambda kk, pos: kk > thr)
    cut_ref[...] = jnp.full((rows, LANES), INT_MAX, I32)

    @pl.when(jnp.max(cnt_ge) > n_sel)
    def _():
        def search_pos(it, p):
            cand = p + lax.shift_left(jnp.int32(1), idx_bits - 1 - it)
            c = count(lambda kk, pos: jnp.where(kk == thr, pos, INT_MAX) < cand)
            return jnp.where(c < need, cand, p)
        cut_ref[...] = lax.fori_loop(0, idx_bits, search_pos, jnp.zeros((rows, LANES), I32))

    few = thr <= KEY_NEG_INF
    thr_ref[...] = jnp.where(few, KEY_NEG_INF + 1, thr)
    cut_ref[...] = jnp.where(few, INT_MAX, cut_ref[...])


def _sample_thresh(sp, sn, n_sel, rows_per_step=64):
    m, n_past = sp.shape
    rows_per_step = min(rows_per_step, m)
    spec = pl.BlockSpec((rows_per_step, LANES), lambda r: (r, 0))
    return pl.pallas_call(
        functools.partial(_sample_thresh_kernel, n_sel=n_sel, n_past=n_past,
                          idx_bits=int(n_past + LANES).bit_length()),
        grid=(m // rows_per_step,),
        in_specs=[pl.BlockSpec((rows_per_step, n_past), lambda r: (r, 0)), spec],
        out_specs=[spec, spec],
        out_shape=[jax.ShapeDtypeStruct((m, LANES), I32), jax.ShapeDtypeStruct((m, LANES), I32)],
        scratch_shapes=[pltpu.VMEM((rows_per_step, n_past + LANES), I32)],
        compiler_params=_cparams("arbitrary"),
        name="sample_thresh",
    )(sp, sn)


def _sample_attn_kernel(pt_ref, q_ref, sp_ref, sn_ref, thr_ref, cut_ref, kn_ref, vn_ref, *refs,
                        scale, n_past, t_new, reps):
    k_pages = refs[:ATTN_PAGES]
    v_pages = refs[ATTN_PAGES:2 * ATTN_PAGES]
    o_ref, m_ref, l_ref, acc_ref = refs[2 * ATTN_PAGES:]
    c = pl.program_id(1)
    page = k_pages[0].shape[1]
    rows = q_ref.shape[2]
    thr = thr_ref[0]
    cut = cut_ref[0]
    lane = lax.broadcasted_iota(I32, (t_new, page), 1)

    @pl.when(c == 0)
    def _():
        m_ref[...] = jnp.full(m_ref.shape, NEG_BIG, F32)
        l_ref[...] = jnp.zeros(l_ref.shape, F32)
        acc_ref[...] = jnp.zeros(acc_ref.shape, F32)

    def select(score, pos0):
        sel = jnp.where((_order_key(score) - jnp.where(pos0 + lane > cut, 1, 0)) >= thr, 1.0, 0.0)
        return jnp.concatenate([sel] * reps, axis=0) > 0.5

    def attend(tiles):
        for g in range(N_KV_HEADS):
            gs = slice(g * LANES, (g + 1) * LANES)
            rs = slice(g * rows, (g + 1) * rows)
            q_g = q_ref[0, g]
            lgs = [jnp.where(sel, lax.dot_general(q_g, k[:, gs], (((1,), (1,)), ((), ())),
                                                  preferred_element_type=F32) * scale, NEG_BIG)
                   for sel, k, _ in tiles]
            tile_max = functools.reduce(jnp.maximum, lgs)
            m_old = m_ref[rs, :]
            m_new = jnp.maximum(m_old, jnp.max(tile_max, axis=1, keepdims=True))
            alpha = jnp.exp(m_old - m_new)
            ps = [jnp.where(sel, jnp.exp(lg - m_new), 0.0) for lg, (sel, _, _) in zip(lgs, tiles)]
            l_ref[rs, :] = alpha * l_ref[rs, :] + jnp.sum(functools.reduce(jnp.add, ps), axis=1, keepdims=True)
            m_ref[rs, :] = m_new
            pv = functools.reduce(jnp.add, [jnp.dot(p.astype(BF16), v[:, gs], preferred_element_type=F32)
                                            for p, (_, _, v) in zip(ps, tiles)])
            acc_ref[rs, :] = alpha * acc_ref[rs, :] + pv

    tiles = []
    for j in range(ATTN_PAGES):
        pos0 = (c * ATTN_PAGES + j) * page
        sel = select(sp_ref[0, :, j * page:(j + 1) * page], pos0)
        tiles.append((sel, k_pages[j][0].astype(BF16), v_pages[j][0].astype(BF16)))
    attend(tiles)

    @pl.when(c == pl.num_programs(1) - 1)
    def _():
        attend([(select(sn_ref[0], n_past), kn_ref[0], vn_ref[0])])
        o_ref[0] = acc_ref[...] / l_ref[...]


def _sample_attn(page_table, q_s, sp, sn, thr, cut, k_new, v_new, cache_k, cache_v, t_new):
    n_seq, n_pages = page_table.shape
    page, kv_w = cache_k.shape[1], cache_k.shape[2]
    rows = q_s.shape[2]
    head_dim = q_s.shape[3]
    n_past = n_pages * page
    reps = rows // t_new
    page_specs = [pl.BlockSpec((1, page, kv_w),
                               functools.partial(lambda b, c, pt, j: (pt[b, c * ATTN_PAGES + j], 0, 0), j=j))
                  for j in range(ATTN_PAGES)]
    per_seq = lambda shape: pl.BlockSpec((1,) + shape, lambda b, c, pt: (b, 0, 0))
    grid_spec = pltpu.PrefetchScalarGridSpec(
        num_scalar_prefetch=1,
        grid=(n_seq, n_pages // ATTN_PAGES),
        in_specs=[pl.BlockSpec((1, N_KV_HEADS, rows, head_dim), lambda b, c, pt: (b, 0, 0, 0)),
                  pl.BlockSpec((1, t_new, ATTN_PAGES * page), lambda b, c, pt: (b, 0, c)),
                  per_seq((t_new, page)), per_seq((t_new, LANES)), per_seq((t_new, LANES)),
                  per_seq((page, kv_w)), per_seq((page, kv_w))] + page_specs + page_specs,
        out_specs=per_seq((N_KV_HEADS * rows, head_dim)),
        scratch_shapes=[pltpu.VMEM((N_KV_HEADS * rows, 1), F32),
                        pltpu.VMEM((N_KV_HEADS * rows, 1), F32),
                        pltpu.VMEM((N_KV_HEADS * rows, head_dim), F32)],
    )
    return pl.pallas_call(
        functools.partial(_sample_attn_kernel, scale=head_dim ** -0.5, n_past=n_past, t_new=t_new, reps=reps),
        grid_spec=grid_spec,
        out_shape=jax.ShapeDtypeStruct((n_seq, N_KV_HEADS * rows, head_dim), F32),
        compiler_params=_cparams("arbitrary", "arbitrary"),
        name="sample_attn",
    )(page_table, q_s, sp, sn, thr, cut, k_new, v_new,
      *([cache_k] * ATTN_PAGES), *([cache_v] * ATTN_PAGES))


def _merge_kernel(x_ref, gt_ref, co_ref, ao_ref, go_ref, wc_ref, wa_ref, o_ref, on_ref, *, n_heads):
    @pl.when(pl.program_id(1) == 0)
    def _():
        g = go_ref[...]
        for h in range(n_heads):
            cs = slice(h * LANES, (h + 1) * LANES)
            o = ao_ref[:, cs]
            on_ref[:, cs] = (o * lax.rsqrt(jnp.mean(o * o, axis=-1, keepdims=True) + EPS) * g).astype(BF16)

    y = jnp.dot(co_ref[...].astype(BF16), wc_ref[...].astype(BF16), preferred_element_type=F32)
    y = y + jnp.dot(on_ref[...], wa_ref[...].astype(BF16), preferred_element_type=F32)
    o_ref[...] = x_ref[...] + gt_ref[0] * y


def _merge(x, gt, conv_o, attn_o, g_o, w_out, tm, tn=512):
    m, d = x.shape
    g, r, _ = gt.shape
    d_conv = conv_o.shape[1]
    d_attn = attn_o.shape[1]
    tiles_per_group = (m // tm) // g
    rb = d_conv // d_attn
    return pl.pallas_call(
        functools.partial(_merge_kernel, n_heads=d_attn // LANES),
        grid=(m // tm, d // tn),
        in_specs=[pl.BlockSpec((tm, tn), lambda i, j: (i, j)),
                  pl.BlockSpec((1, r, tn), lambda i, j: (i // tiles_per_group, 0, j)),
                  pl.BlockSpec((tm, d_conv), lambda i, j: (i, 0)),
                  pl.BlockSpec((tm, d_attn), lambda i, j: (i, 0)),
                  pl.BlockSpec((1, LANES), lambda i, j: (0, 0)),
                  pl.BlockSpec((d_conv, tn), lambda i, j: (0, j)),
                  pl.BlockSpec((d_attn, tn), lambda i, j: (rb, j))],
        out_specs=pl.BlockSpec((tm, tn), lambda i, j: (i, j)),
        out_shape=jax.ShapeDtypeStruct((m, d), F32),
        scratch_shapes=[pltpu.VMEM((tm, d_attn), BF16)],
        compiler_params=_cparams("arbitrary", "arbitrary"),
        name="merge",
    )(x, gt, conv_o, attn_o, g_o.reshape(1, -1), w_out, w_out)


def _ffn_act(cur_g, p1_g, p2_g, cur_v, p1_v, p2_v, wg_ref, wv_ref, bg_ref, bv_ref):
    gate = p2_g * wg_ref[0:1, :] + p1_g * wg_ref[1:2, :] + cur_g * wg_ref[2:3, :] + bg_ref[...]
    val = p2_v * wv_ref[0:1, :] + p1_v * wv_ref[1:2, :] + cur_v * wv_ref[2:3, :] + bv_ref[...]
    return (_silu(gate) * val).astype(BF16)


def _ffn_finish(f, act, wd_ref, x_ref, gt_ref, o_ref, acc_ref):
    @pl.when(f == 0)
    def _():
        acc_ref[...] = jnp.zeros(acc_ref.shape, F32)

    acc_ref[...] += jnp.dot(act, wd_ref[...].astype(BF16), preferred_element_type=F32)

    @pl.when(f == pl.num_programs(1) - 1)
    def _():
        o_ref[...] = x_ref[...] + gt_ref[0] * acc_ref[...]


def _ffn_prompt_kernel(ug_ref, uv_ref, hg_ref, hv_ref, wg_ref, wv_ref, bg_ref, bv_ref, wd_ref, x_ref, gt_ref,
                       o_ref, acc_ref, hist_g, hist_v, *, tm, tiles_per_seq):
    i, f = pl.program_id(0), pl.program_id(1)
    first = (i % tiles_per_seq) == 0
    hist_g[0:SUBLANES, :] = jnp.where(first, 0.0, hg_ref[...])
    hist_v[0:SUBLANES, :] = jnp.where(first, 0.0, hv_ref[...])
    hist_g[SUBLANES:SUBLANES + tm, :] = ug_ref[...]
    hist_v[SUBLANES:SUBLANES + tm, :] = uv_ref[...]
    act = _ffn_act(ug_ref[...], hist_g[SUBLANES - 1:SUBLANES - 1 + tm, :], hist_g[SUBLANES - 2:SUBLANES - 2 + tm, :],
                   uv_ref[...], hist_v[SUBLANES - 1:SUBLANES - 1 + tm, :], hist_v[SUBLANES - 2:SUBLANES - 2 + tm, :],
                   wg_ref, wv_ref, bg_ref, bv_ref)
    _ffn_finish(f, act, wd_ref, x_ref, gt_ref, o_ref, acc_ref)


def _ffn_prompt(u, x1, gt, w_dw, b_dw, w_down, seq, tm=512, tf=512):
    m, d = x1.shape
    d_ff = w_down.shape[0]
    nf = d_ff // tf
    tiles_per_seq = seq // tm
    fw = w_dw.shape[0]
    halo_idx = lambda i: jnp.maximum(i * (tm // SUBLANES) - 1, 0)
    return pl.pallas_call(
        functools.partial(_ffn_prompt_kernel, tm=tm, tiles_per_seq=tiles_per_seq),
        grid=(m // tm, nf),
        in_specs=[pl.BlockSpec((tm, tf), lambda i, f: (i, f)),
                  pl.BlockSpec((tm, tf), lambda i, f: (i, f + nf)),
                  pl.BlockSpec((SUBLANES, tf), lambda i, f: (halo_idx(i), f)),
                  pl.BlockSpec((SUBLANES, tf), lambda i, f: (halo_idx(i), f + nf)),
                  pl.BlockSpec((fw, tf), lambda i, f: (0, f)),
                  pl.BlockSpec((fw, tf), lambda i, f: (0, f + nf)),
                  pl.BlockSpec((1, tf), lambda i, f: (0, f)),
                  pl.BlockSpec((1, tf), lambda i, f: (0, f + nf)),
                  pl.BlockSpec((tf, d), lambda i, f: (f, 0)),
                  pl.BlockSpec((tm, d), lambda i, f: (i, 0)),
                  pl.BlockSpec((1, 1, d), lambda i, f: (i // tiles_per_seq, 0, 0))],
        out_specs=pl.BlockSpec((tm, d), lambda i, f: (i, 0)),
        out_shape=jax.ShapeDtypeStruct((m, d), F32),
        scratch_shapes=[pltpu.VMEM((tm, d), F32),
                        pltpu.VMEM((SUBLANES + tm, tf), F32),
                        pltpu.VMEM((SUBLANES + tm, tf), F32)],
        compiler_params=_cparams("arbitrary", "arbitrary"),
        name="ffn_prompt",
    )(u, u, u, u, w_dw, w_dw, b_dw.reshape(1, -1), b_dw.reshape(1, -1), w_down, x1, gt)


def _ffn_sample_kernel(cg_ref, cv_ref, p1g_ref, p1v_ref, p2g_ref, p2v_ref, wg_ref, wv_ref, bg_ref, bv_ref,
                       wd_ref, x_ref, gt_ref, o_ref, acc_ref):
    act = _ffn_act(cg_ref[...], p1g_ref[...], p2g_ref[...], cv_ref[...], p1v_ref[...], p2v_ref[...],
                   wg_ref, wv_ref, bg_ref, bv_ref)
    _ffn_finish(pl.program_id(1), act, wd_ref, x_ref, gt_ref, o_ref, acc_ref)


def _ffn_sample(cur, prev1, prev2, x1, gt, w_dw, b_dw, w_down, tf=512):
    m, d = x1.shape
    d_ff = w_down.shape[0]
    nf = d_ff // tf
    fw = w_dw.shape[0]
    lo = pl.BlockSpec((m, tf), lambda i, f: (0, f))
    hi = pl.BlockSpec((m, tf), lambda i, f: (0, f + nf))
    return pl.pallas_call(
        _ffn_sample_kernel,
        grid=(1, nf),
        in_specs=[lo, hi, lo, hi, lo, hi,
                  pl.BlockSpec((fw, tf), lambda i, f: (0, f)),
                  pl.BlockSpec((fw, tf), lambda i, f: (0, f + nf)),
                  pl.BlockSpec((1, tf), lambda i, f: (0, f)),
                  pl.BlockSpec((1, tf), lambda i, f: (0, f + nf)),
                  pl.BlockSpec((tf, d), lambda i, f: (f, 0)),
                  pl.BlockSpec((m, d), lambda i, f: (0, 0)),
                  pl.BlockSpec((1, m, d), lambda i, f: (0, 0, 0))],
        out_specs=pl.BlockSpec((m, d), lambda i, f: (0, 0)),
        out_shape=jax.ShapeDtypeStruct((m, d), F32),
        scratch_shapes=[pltpu.VMEM((m, d), F32)],
        compiler_params=_cparams("arbitrary", "arbitrary"),
        name="ffn_sample",
    )(cur, cur, prev1, prev1, prev2, prev2, w_dw, w_dw, b_dw.reshape(1, -1), b_dw.reshape(1, -1),
      w_down, x1, gt)


def _rope_tables(pos):
    def tab(dim):
        inv = jnp.power(ROPE_THETA, -jnp.arange(0, dim, 2, dtype=F32) / dim)
        ang = pos.astype(F32)[:, None] * inv[None, :]
        cos, sin = jnp.cos(ang), jnp.sin(ang)
        reps = LANES // dim
        return (jnp.tile(jnp.concatenate([cos, cos], axis=-1), (1, reps)),
                jnp.tile(jnp.concatenate([-sin, sin], axis=-1), (1, reps)))
    return tab(LANES) + tab(IDX_DIM)


def _in_proj(x2d, norm1, sc, sh, w_in, n_main, tm):
    z = _normmod_matmul(x2d, norm1, sc, sh, w_in, n_main, tm, 512, "in_proj")
    w_tail = jnp.pad(w_in[:, n_main:], ((0, 0), (0, LANES - (w_in.shape[1] - n_main))))
    zt = _normmod_matmul(x2d, norm1, sc, sh, w_tail, LANES, tm, LANES, "in_proj_tail")
    return z, zt


def kernel(x_prompt, x_sample, cache_k, cache_v, cache_kidx, state_conv, state_ffn, page_table, c_prompt, c_sample, norm1, w_ada, b_ada, w_in, g_q, g_k, w_dw_a, b_dw_a, gn_g, gn_b, g_o, w_out, norm2, w_up, w_dw_f, b_dw_f, w_down):
    n_b, seq, d = x_prompt.shape
    n_s, t_new, _ = x_sample.shape
    depth = norm1.shape[0]
    assert depth == 1
    head_dim = g_q.shape[-1]
    assert head_dim == LANES
    d_conv = w_dw_a.shape[-1]
    d_attn = w_out.shape[1] - d_conv
    n_heads = d_attn // head_dim
    kv_w = N_KV_HEADS * head_dim
    n_main = 2 * d_conv + d_attn + 2 * kv_w + N_IDX_HEADS * IDX_DIM
    n_pool, page = cache_k.shape[1], cache_k.shape[2]
    n_pages = page_table.shape[1]
    n_past = n_pages * page
    mp, ms = n_b * seq, n_s * t_new
    heads_per_kv = n_heads // N_KV_HEADS

    n_c = n_b + n_s
    pad_c = (-n_c) % SUBLANES
    c_all = jnp.concatenate([c_prompt, c_sample, jnp.zeros((pad_c, d), F32)], axis=0)
    mods = _ada(c_all, w_ada[0], b_ada[0])
    mp6 = mods[:n_b].reshape(n_b, 6, 1, d)
    sh1p, sc1p, gt1p, sh2p, sc2p, gt2p = [mp6[:, k] for k in range(6)]
    ms6 = jnp.repeat(mods[n_b:n_c].reshape(n_s, 6, d), t_new, axis=0)
    sh1s, sc1s, gt1s, sh2s, sc2s, gt2s = [ms6[:, k][None] for k in range(6)]

    xp = x_prompt.reshape(mp, d)
    tm_p = 1024 if seq % 1024 == 0 else QB
    z, zt = _in_proj(xp, norm1[0], sc1p, sh1p, w_in[0], n_main, tm_p)
    conv_o, new_conv_p = _conv_prompt(z, n_b, seq, d_conv, w_dw_a[0], b_dw_a[0], gn_g[0], gn_b[0],
                                      tm=256 if seq % 256 == 0 else QB)
    tabs_p = _rope_tables(jnp.arange(seq, dtype=I32))
    q_hm, qi_hm, k_p, k_bf, vt_bf, ki_p, ki_bf = _qk_epilogue(z, zt, tabs_p, g_q[0], g_k[0], n_b, seq, n_heads)
    w_hm = zt[:, IDX_DIM:IDX_DIM + N_IDX_HEADS].reshape(mp // QB, QB, N_IDX_HEADS).transpose(0, 2, 1)
    n_sel_p = min(TOPK_MAX, seq // 4)
    attn_p = _attn_prompt(qi_hm, w_hm, ki_bf, q_hm, k_bf, vt_bf, n_b, seq, n_sel_p)
    x1p = _merge(xp, gt1p, conv_o, attn_p, g_o[0], w_out[0], tm_p)
    u_p = _normmod_matmul(x1p, norm2[0], sc2p, sh2p, w_up[0], w_up.shape[-1], tm_p, 512, "up_proj")
    y_p = _ffn_prompt(u_p, x1p, gt2p, w_dw_f[0], b_dw_f[0], w_down[0], seq,
                      tm=512 if seq % 512 == 0 else QB)
    v_p = z[:, 2 * d_conv + d_attn + kv_w:2 * d_conv + d_attn + 2 * kv_w]
    fw = w_dw_f.shape[1]
    new_ffn_p = u_p.reshape(n_b, seq, -1)[:, seq - (fw - 1):]

    xs = x_sample.reshape(ms, d)
    zs, zts = _in_proj(xs, norm1[0], sc1s, sh1s, w_in[0], n_main, ms)
    conv_os, new_conv_s = _conv_sample(zs, state_conv[0], d_conv, w_dw_a[0], b_dw_a[0], gn_g[0], gn_b[0])
    pos_s = jnp.tile(n_past + jnp.arange(t_new, dtype=I32), ms // t_new)
    tabs_s = _rope_tables(pos_s)
    q_hm_s, qi_hm_s, k_s, k_bf_s, _, ki_s, ki_bf_s = _qk_epilogue(zs, zts, tabs_s, g_q[0], g_k[0], 1, ms, n_heads)
    v_s = zs[:, 2 * d_conv + d_attn + kv_w:2 * d_conv + d_attn + 2 * kv_w]
    seq_rows = lambda a, nh: (a.reshape(ms // QB, nh, QB // t_new, t_new, a.shape[-1])
                              .transpose(0, 2, 1, 3, 4).reshape(n_s, nh * t_new, a.shape[-1]))
    qi_s = seq_rows(qi_hm_s, N_IDX_HEADS)
    q_s = seq_rows(q_hm_s, n_heads).reshape(n_s, N_KV_HEADS, heads_per_kv * t_new, head_dim)
    w_s = (zts[:, IDX_DIM:IDX_DIM + N_IDX_HEADS].reshape(n_s, t_new, N_IDX_HEADS)
           .transpose(0, 2, 1).reshape(n_s, N_IDX_HEADS * t_new, 1))
    pad_new = lambda a: jnp.pad(a.reshape(n_s, t_new, a.shape[-1]), ((0, 0), (0, page - t_new), (0, 0)))
    sp, sn = _sample_scores(page_table, qi_s, w_s, pad_new(ki_bf_s), cache_kidx[0], t_new)
    n_sel_s = min(TOPK_MAX, (n_past + t_new) // 4)
    thr, cut = _sample_thresh(sp.reshape(ms, n_past), sn.reshape(ms, page), n_sel_s)
    o_s = _sample_attn(page_table, q_s, sp, sn, thr.reshape(n_s, t_new, LANES), cut.reshape(n_s, t_new, LANES),
                       pad_new(k_bf_s), pad_new(v_s.astype(BF16)),
                       cache_k[0].reshape(n_pool, page, kv_w), cache_v[0].reshape(n_pool, page, kv_w), t_new)
    attn_s = o_s.reshape(n_s, n_heads, t_new, head_dim).transpose(0, 2, 1, 3).reshape(ms, d_attn)
    x1s = _merge(xs, gt1s, conv_os, attn_s, g_o[0], w_out[0], ms)
    u_s = _normmod_matmul(x1s, norm2[0], sc2s, sh2s, w_up[0], w_up.shape[-1], ms, 512, "up_proj")
    u_hist = jnp.concatenate([state_ffn[0], u_s.reshape(n_s, t_new, -1)], axis=1)
    shifted = lambda k: u_hist[:, k:k + t_new].reshape(ms, -1)
    y_s = _ffn_sample(shifted(2), shifted(1), shifted(0), x1s, gt2s, w_dw_f[0], b_dw_f[0], w_down[0])
    new_ffn_s = u_hist[:, t_new:]

    return (y_p.reshape(n_b, seq, d), y_s.reshape(n_s, t_new, d),
            k_p.reshape(1, n_b, seq, N_KV_HEADS, head_dim), v_p.reshape(1, n_b, seq, N_KV_HEADS, head_dim),
            ki_p.reshape(1, n_b, seq, IDX_DIM), new_conv_p[None], new_ffn_p[None],
            k_s.reshape(1, n_s, t_new, N_KV_HEADS, head_dim), v_s.reshape(1, n_s, t_new, N_KV_HEADS, head_dim),
            ki_s.reshape(1, n_s, t_new, IDX_DIM), new_conv_s[None], new_ffn_s[None])
```

```python
import functools

import jax
import jax.numpy as jnp
from jax import lax
from jax.experimental import pallas as pl
from jax.experimental.pallas import tpu as pltpu

F32 = jnp.float32
BF16 = jnp.bfloat16
I32 = jnp.int32

CONV_GROUPS = 8
N_KV_HEADS = 2
N_IDX_HEADS = 16
IDX_DIM = 64
TOPK_MAX = 256
ROPE_THETA = 10000.0
EPS = 1e-6
IDX_SCALE = (IDX_DIM ** -0.5) * (N_IDX_HEADS ** -0.5)

LANES = 128
SUBLANES = 8
QB = 128
VMEM_LIMIT_BYTES = 56 * 1024 * 1024
NEG_BIG = -1e30
M_INIT = -1e29
LOG2_E = 1.4426950408889634
INT_MIN = -2 ** 31
INT_MAX = 2 ** 31 - 1
KEY_NEG_INF = -2139095041


def _cparams(*sem):
    return pltpu.CompilerParams(dimension_semantics=sem, vmem_limit_bytes=VMEM_LIMIT_BYTES)


def _silu(x):
    return x * jax.nn.sigmoid(x)


def _order_key(x):
    bits = pltpu.bitcast(x, I32)
    return bits ^ ((bits >> 31) & INT_MAX)


def _ada_kernel(c_ref, w_ref, b_ref, o_ref):
    s = _silu(c_ref[...]).astype(BF16)
    o_ref[...] = jnp.dot(s, w_ref[...].astype(BF16), preferred_element_type=F32) + b_ref[...]


def _ada(c_all, w_ada, b_ada, tn=1024):
    r, d = c_all.shape
    n = w_ada.shape[1]
    return pl.pallas_call(
        _ada_kernel,
        grid=(n // tn,),
        in_specs=[pl.BlockSpec((r, d), lambda j: (0, 0)),
                  pl.BlockSpec((d, tn), lambda j: (0, j)),
                  pl.BlockSpec((1, tn), lambda j: (0, j))],
        out_specs=pl.BlockSpec((r, tn), lambda j: (0, j)),
        out_shape=jax.ShapeDtypeStruct((r, n), F32),
        compiler_params=_cparams("arbitrary"),
        name="ada",
    )(c_all, w_ada, b_ada.reshape(1, n))


_CONTRACT_LAST = (((1,), (1,)), ((), ()))


def _normmod_matmul_kernel(x_ref, g_ref, sc_ref, sh_ref, w_ref, o_ref, h_ref, *, w_transposed):
    @pl.when(pl.program_id(1) == 0)
    def _():
        x = x_ref[...]
        y = x * lax.rsqrt(jnp.mean(x * x, axis=-1, keepdims=True) + EPS) * g_ref[...]
        h_ref[...] = (y * (1.0 + sc_ref[0]) + sh_ref[0]).astype(BF16)

    w = w_ref[...].astype(BF16)
    if w_transposed:
        o_ref[...] = lax.dot_general(h_ref[...], w, _CONTRACT_LAST, preferred_element_type=F32)
    else:
        o_ref[...] = jnp.dot(h_ref[...], w, preferred_element_type=F32)


def _normmod_matmul(x, gain, sc, sh, w, ncols, tm, tn, name, w_transposed=False):
    m, d = x.shape
    g, r, _ = sc.shape
    tiles_per_group = (m // tm) // g
    mod_spec = pl.BlockSpec((1, r, d), lambda i, j: (i // tiles_per_group, 0, 0))
    w_spec = pl.BlockSpec((tn, d), lambda i, j: (j, 0)) if w_transposed else pl.BlockSpec((d, tn), lambda i, j: (0, j))
    return pl.pallas_call(
        functools.partial(_normmod_matmul_kernel, w_transposed=w_transposed),
        grid=(m // tm, ncols // tn),
        in_specs=[pl.BlockSpec((tm, d), lambda i, j: (i, 0)),
                  pl.BlockSpec((1, d), lambda i, j: (0, 0)),
                  mod_spec, mod_spec,
                  w_spec],
        out_specs=pl.BlockSpec((tm, tn), lambda i, j: (i, j)),
        out_shape=jax.ShapeDtypeStruct((m, ncols), F32),
        scratch_shapes=[pltpu.VMEM((tm, d), BF16)],
        compiler_params=_cparams("arbitrary", "arbitrary"),
        name=name,
    )(x, gain.reshape(1, d), sc, sh, w)


def _conv_gn_silu(hist_ref, off, rows, wdw_ref, bdw_ref, gng_ref, gnb_ref, o_ref, row_chunk):
    conv_w = wdw_ref.shape[0]
    for r0 in range(0, rows, row_chunk):
        for c in range(CONV_GROUPS):
            cs = slice(c * LANES, (c + 1) * LANES)
            acc = jnp.zeros((row_chunk, LANES), F32)
            for w in range(conv_w):
                acc = acc + hist_ref[off + r0 + w:off + r0 + w + row_chunk, cs] * wdw_ref[w:w + 1, cs]
            y = acc + bdw_ref[:, cs]
            mu = jnp.mean(y, axis=-1, keepdims=True)
            dlt = y - mu
            var = jnp.mean(dlt * dlt, axis=-1, keepdims=True)
            yn = dlt * lax.rsqrt(var + EPS) * gng_ref[:, cs] + gnb_ref[:, cs]
            o_ref[r0:r0 + row_chunk, cs] = _silu(yn).astype(o_ref.dtype)


HALO = 32


def _conv_prompt_kernel(za_ref, zg_ref, ha_ref, hg_ref, wdw_ref, bdw_ref, gng_ref, gnb_ref,
                        o_ref, newc_ref, hist_ref, *, tm, tiles_per_seq):
    i = pl.program_id(0)
    first = (i % tiles_per_seq) == 0
    a_halo = ha_ref[...] * jax.nn.sigmoid(hg_ref[...])
    hist_ref[0:HALO, :] = jnp.where(first, 0.0, a_halo)
    hist_ref[HALO:HALO + tm, :] = za_ref[...] * jax.nn.sigmoid(zg_ref[...])
    conv_w = wdw_ref.shape[0]
    _conv_gn_silu(hist_ref, HALO - (conv_w - 1), tm, wdw_ref, bdw_ref, gng_ref, gnb_ref, o_ref, 64)

    @pl.when((i % tiles_per_seq) == tiles_per_seq - 1)
    def _():
        newc_ref[0] = hist_ref[HALO + tm - (conv_w - 1):HALO + tm, :]


def _conv_prompt(z, n_seq, seq, d_conv, wdw, bdw, gng, gnb, tm=256):
    m = z.shape[0]
    conv_w = wdw.shape[0]
    tiles_per_seq = seq // tm
    cb = 1
    halo_idx = lambda i: jnp.maximum(i * (tm // HALO) - 1, 0)
    vec = pl.BlockSpec((1, d_conv), lambda i: (0, 0))
    return pl.pallas_call(
        functools.partial(_conv_prompt_kernel, tm=tm, tiles_per_seq=tiles_per_seq),
        grid=(m // tm,),
        in_specs=[pl.BlockSpec((tm, d_conv), lambda i: (i, 0)),
                  pl.BlockSpec((tm, d_conv), lambda i: (i, cb)),
                  pl.BlockSpec((HALO, d_conv), lambda i: (halo_idx(i), 0)),
                  pl.BlockSpec((HALO, d_conv), lambda i: (halo_idx(i), cb)),
                  pl.BlockSpec((conv_w, d_conv), lambda i: (0, 0)),
                  vec, vec, vec],
        out_specs=[pl.BlockSpec((tm, d_conv), lambda i: (i, 0)),
                   pl.BlockSpec((1, conv_w - 1, d_conv), lambda i: (i // tiles_per_seq, 0, 0))],
        out_shape=[jax.ShapeDtypeStruct((m, d_conv), BF16),
                   jax.ShapeDtypeStruct((n_seq, conv_w - 1, d_conv), F32)],
        scratch_shapes=[pltpu.VMEM((HALO + tm, d_conv), F32)],
        compiler_params=_cparams("arbitrary"),
        name="conv_prompt",
    )(z, z, z, z, wdw, bdw.reshape(1, -1), gng.reshape(1, -1), gnb.reshape(1, -1))


def _conv_sample_kernel(za_ref, zg_ref, st_ref, wdw_ref, bdw_ref, gng_ref, gnb_ref,
                        o_ref, newc_ref, hist_ref, *, t_new):
    conv_w = wdw_ref.shape[0]
    hist_ref[0:conv_w - 1, :] = st_ref[0]
    hist_ref[conv_w - 1:conv_w - 1 + t_new, :] = za_ref[...] * jax.nn.sigmoid(zg_ref[...])
    _conv_gn_silu(hist_ref, 0, t_new, wdw_ref, bdw_ref, gng_ref, gnb_ref, o_ref, t_new)
    newc_ref[0] = hist_ref[t_new:t_new + conv_w - 1, :]


def _conv_sample(z, state, d_conv, wdw, bdw, gng, gnb):
    n_seq, hist_rows, _ = state.shape
    conv_w = wdw.shape[0]
    m = z.shape[0]
    t_new = m // n_seq
    vec = pl.BlockSpec((1, d_conv), lambda b: (0, 0))
    return pl.pallas_call(
        functools.partial(_conv_sample_kernel, t_new=t_new),
        grid=(n_seq,),
        in_specs=[pl.BlockSpec((t_new, d_conv), lambda b: (b, 0)),
                  pl.BlockSpec((t_new, d_conv), lambda b: (b, 1)),
                  pl.BlockSpec((1, hist_rows, d_conv), lambda b: (b, 0, 0)),
                  pl.BlockSpec((conv_w, d_conv), lambda b: (0, 0)),
                  vec, vec, vec],
        out_specs=[pl.BlockSpec((t_new, d_conv), lambda b: (b, 0)),
                   pl.BlockSpec((1, hist_rows, d_conv), lambda b: (b, 0, 0))],
        out_shape=[jax.ShapeDtypeStruct((m, d_conv), F32),
                   jax.ShapeDtypeStruct((n_seq, hist_rows, d_conv), F32)],
        scratch_shapes=[pltpu.VMEM((hist_rows + t_new + SUBLANES, d_conv), F32)],
        compiler_params=_cparams("arbitrary"),
        name="conv_sample",
    )(z, z, state, wdw, bdw.reshape(1, -1), gng.reshape(1, -1), gnb.reshape(1, -1))


def _qk_kernel(zq_ref, zkv_ref, zqi0_ref, zqi1_ref, zt_ref, cos_ref, sin_ref, cosi_ref, sini_ref,
               gq_ref, gk_ref, qhm_ref, qihm_ref, k_ref, kbf_ref, vt_ref, ki_ref, kibf_ref, *, n_heads):
    cos, sin = cos_ref[...], sin_ref[...]
    cosi, sini = cosi_ref[...], sini_ref[...]
    tm = cos.shape[0]
    lane = lax.broadcasted_iota(I32, (tm, LANES), 1)
    low_half = (lane % IDX_DIM) < (IDX_DIM // 2)

    def norm_rope(x, g):
        y = x * lax.rsqrt(jnp.mean(x * x, axis=-1, keepdims=True) + EPS) * g
        return y * cos + pltpu.roll(y, LANES // 2, 1) * sin

    def rope_idx(x):
        r = jnp.where(low_half, pltpu.roll(x, LANES - IDX_DIM // 2, 1), pltpu.roll(x, IDX_DIM // 2, 1))
        return x * cosi + r * sini

    gq, gk = gq_ref[...], gk_ref[...]
    for h in range(n_heads):
        qhm_ref[0, h] = norm_rope(zq_ref[:, h * LANES:(h + 1) * LANES], gq).astype(BF16)
    kv_w = N_KV_HEADS * LANES
    for g in range(N_KV_HEADS):
        kg = norm_rope(zkv_ref[:, g * LANES:(g + 1) * LANES], gk)
        k_ref[:, g * LANES:(g + 1) * LANES] = kg
        kbf_ref[:, g * LANES:(g + 1) * LANES] = kg.astype(BF16)
    vt_ref[0] = zkv_ref[:, kv_w:2 * kv_w].T.astype(BF16)
    half = (N_IDX_HEADS * IDX_DIM) // 2
    for j in range(N_IDX_HEADS // 2):
        src = zqi0_ref if j * LANES < half else zqi1_ref
        c0 = (j * LANES) % half
        y = rope_idx(src[:, c0:c0 + LANES])
        qihm_ref[0, 2 * j] = y[:, :IDX_DIM].astype(BF16)
        qihm_ref[0, 2 * j + 1] = y[:, IDX_DIM:].astype(BF16)
    yk = rope_idx(zt_ref[...])[:, :IDX_DIM]
    ki_ref[...] = yk
    kibf_ref[...] = yk.astype(BF16)


def _qk_epilogue(z, zt, tabs, gq, gk, n_seq, seq, n_heads):
    m = z.shape[0]
    tm = QB
    head_dim = LANES
    cos, sin, cosi, sini = tabs
    tab_tiles = cos.shape[0] // tm
    kv_w = N_KV_HEADS * head_dim
    d_attn = n_heads * head_dim
    d_conv = d_attn
    q_cb = (2 * d_conv) // d_attn
    kv_cb = (2 * d_conv + d_attn) // (2 * kv_w)
    qi_w = (N_IDX_HEADS * IDX_DIM) // 2
    qi_cb = (2 * d_conv + d_attn + 2 * kv_w) // qi_w
    tiles_per_seq = seq // tm
    tab = pl.BlockSpec((tm, LANES), lambda i: (i % tab_tiles, 0))
    vec = pl.BlockSpec((1, LANES), lambda i: (0, 0))
    return pl.pallas_call(
        functools.partial(_qk_kernel, n_heads=n_heads),
        grid=(m // tm,),
        in_specs=[pl.BlockSpec((tm, d_attn), lambda i: (i, q_cb)),
                  pl.BlockSpec((tm, 2 * kv_w), lambda i: (i, kv_cb)),
                  pl.BlockSpec((tm, qi_w), lambda i: (i, qi_cb)),
                  pl.BlockSpec((tm, qi_w), lambda i: (i, qi_cb + 1)),
                  pl.BlockSpec((tm, LANES), lambda i: (i, 0)),
                  tab, tab, tab, tab, vec, vec],
        out_specs=[pl.BlockSpec((1, n_heads, tm, head_dim), lambda i: (i, 0, 0, 0)),
                   pl.BlockSpec((1, N_IDX_HEADS, tm, IDX_DIM), lambda i: (i, 0, 0, 0)),
                   pl.BlockSpec((tm, kv_w), lambda i: (i, 0)),
                   pl.BlockSpec((tm, kv_w), lambda i: (i, 0)),
                   pl.BlockSpec((1, kv_w, tm), lambda i: (i // tiles_per_seq, 0, i % tiles_per_seq)),
                   pl.BlockSpec((tm, IDX_DIM), lambda i: (i, 0)),
                   pl.BlockSpec((tm, IDX_DIM), lambda i: (i, 0))],
        out_shape=[jax.ShapeDtypeStruct((m // tm, n_heads, tm, head_dim), BF16),
                   jax.ShapeDtypeStruct((m // tm, N_IDX_HEADS, tm, IDX_DIM), BF16),
                   jax.ShapeDtypeStruct((m, kv_w), F32),
                   jax.ShapeDtypeStruct((m, kv_w), BF16),
                   jax.ShapeDtypeStruct((n_seq, kv_w, seq), BF16),
                   jax.ShapeDtypeStruct((m, IDX_DIM), F32),
                   jax.ShapeDtypeStruct((m, IDX_DIM), BF16)],
        compiler_params=_cparams("arbitrary"),
        name="qk_epilogue",
    )(z, z, z, z, zt, cos, sin, cosi, sini, gq.reshape(1, -1), gk.reshape(1, -1))


KT = 2 * QB


def _attn_prompt_kernel(qi_ref, w_ref, ki_ref, q_ref, k_ref, vt_ref, o_ref,
                        key_ref, m_ref, l_ref, acc_ref, cut_ref, *, n_sel, c_exp, n_heads, idx_bits):
    i = pl.program_id(1)
    n_kt = lax.div(i * QB + QB + KT - 1, KT)
    row_iota = lax.broadcasted_iota(I32, (KT, QB), 0)
    w_all = w_ref[0] * IDX_SCALE
    heads_per_kv = n_heads // N_KV_HEADS
    hc = 4

    def score_tile(kt, carry):
        for half in range(KT // QB):
            ks = pl.multiple_of(kt * KT + half * QB, QB)
            ki_t = ki_ref[0, pl.ds(ks, QB), :]
            acc = jnp.zeros((QB, QB), F32)
            for h0 in range(0, N_IDX_HEADS, hc):
                s = lax.dot_general(ki_t, qi_ref[0, h0:h0 + hc].reshape(hc * QB, IDX_DIM),
                                    _CONTRACT_LAST, preferred_element_type=F32)
                for h in range(hc):
                    acc = acc + jnp.maximum(s[:, h * QB:(h + 1) * QB], 0.0) * w_all[h0 + h:h0 + h + 1, :]
            acc = jnp.where(ks + lax.broadcasted_iota(I32, (QB, QB), 0) <=
                            i * QB + lax.broadcasted_iota(I32, (QB, QB), 1), acc, -jnp.inf)
            key_ref[pl.ds(ks, QB), :] = _order_key(acc)
        return carry

    lax.fori_loop(0, n_kt, score_tile, 0)

    def count(pred):
        def body(kt, cnt):
            ks = pl.multiple_of(kt * KT, KT)
            hit = jnp.where(pred(key_ref[pl.ds(ks, KT), :], ks + row_iota), 1, 0)
            return cnt + jnp.sum(hit.reshape(KT // SUBLANES, SUBLANES, QB), axis=0)
        cnt = lax.fori_loop(0, n_kt, body, jnp.zeros((SUBLANES, QB), I32))
        return jnp.sum(cnt, axis=0, keepdims=True)

    def search(it, thr):
        cand = thr + lax.shift_left(jnp.int32(1), 31 - it)
        c = count(lambda kk, pos: kk >= cand)
        return jnp.where(c >= n_sel, cand, thr)

    thr = lax.fori_loop(0, 32, search, jnp.full((1, QB), INT_MIN, I32))
    cnt_ge = count(lambda kk, pos: kk >= thr)
    need = n_sel - count(lambda kk, pos: kk > thr)
    cut_ref[...] = jnp.full((1, QB), INT_MAX, I32)

    @pl.when(jnp.max(cnt_ge) > n_sel)
    def _():
        def search_pos(it, p):
            cand = p + lax.shift_left(jnp.int32(1), idx_bits - 1 - it)
            c = count(lambda kk, pos: jnp.where(kk == thr, pos, INT_MAX) < cand)
            return jnp.where(c < need, cand, p)
        cut_ref[...] = lax.fori_loop(0, idx_bits, search_pos, jnp.zeros((1, QB), I32))

    few = thr <= KEY_NEG_INF
    thr_eff = jnp.where(few, KEY_NEG_INF + 1, thr)
    cut = jnp.where(few, INT_MAX, cut_ref[...])

    m_ref[...] = jnp.full(m_ref.shape, M_INIT, F32)
    l_ref[...] = jnp.zeros(l_ref.shape, F32)
    acc_ref[...] = jnp.zeros(acc_ref.shape, F32)

    def attn_tile(kt, carry):
        ks = pl.multiple_of(kt * KT, KT)
        kk = key_ref[pl.ds(ks, KT), :]
        sel = (kk - jnp.where(ks + row_iota > cut, 1, 0)) >= thr_eff
        m_old, l_old = m_ref[...], l_ref[...]
        m_parts, l_parts = [], []
        for g in range(N_KV_HEADS):
            k_t = k_ref[0, pl.ds(ks, KT), g * LANES:(g + 1) * LANES]
            v_t = vt_ref[0, g * LANES:(g + 1) * LANES, pl.ds(ks, KT)]
            q_g = q_ref[0, g * heads_per_kv:(g + 1) * heads_per_kv].reshape(heads_per_kv * QB, LANES)
            lg = lax.dot_general(k_t, q_g, _CONTRACT_LAST, preferred_element_type=F32)
            for hh in range(heads_per_kv):
                cs = slice((g * heads_per_kv + hh) * QB, (g * heads_per_kv + hh + 1) * QB)
                raw = jnp.where(sel, lg[:, hh * QB:(hh + 1) * QB], NEG_BIG)
                m_new = jnp.maximum(m_old[:, cs], jnp.max(raw, axis=0, keepdims=True))
                alpha = jnp.exp2((m_old[:, cs] - m_new) * c_exp)
                p = jnp.exp2((raw - m_new) * c_exp)
                m_parts.append(m_new)
                l_parts.append(alpha * l_old[:, cs] + jnp.sum(p, axis=0, keepdims=True))
                acc_ref[:, cs] = alpha * acc_ref[:, cs] + jnp.dot(v_t, p.astype(BF16),
                                                                   preferred_element_type=F32)
        m_ref[...] = jnp.concatenate(m_parts, axis=1)
        l_ref[...] = jnp.concatenate(l_parts, axis=1)
        return carry

    lax.fori_loop(0, n_kt, attn_tile, 0)
    for h in range(n_heads):
        cs = slice(h * QB, (h + 1) * QB)
        o_ref[:, cs] = (acc_ref[:, cs] / l_ref[:, cs]).T


def _attn_prompt(qi_hm, w_hm, ki_bf, q_hm, k_bf, vt_bf, n_seq, seq, n_sel):
    n_heads = q_hm.shape[1]
    head_dim = q_hm.shape[3]
    nblk = seq // QB
    kv_w = k_bf.shape[-1]
    blk = lambda b, i: (b * nblk + i, 0, 0, 0)
    return pl.pallas_call(
        functools.partial(_attn_prompt_kernel, n_sel=n_sel, c_exp=head_dim ** -0.5 * LOG2_E, n_heads=n_heads,
                          idx_bits=int(seq).bit_length()),
        grid=(n_seq, nblk),
        in_specs=[pl.BlockSpec((1, N_IDX_HEADS, QB, IDX_DIM), blk),
                  pl.BlockSpec((1, N_IDX_HEADS, QB), lambda b, i: (b * nblk + i, 0, 0)),
                  pl.BlockSpec((1, seq, IDX_DIM), lambda b, i: (b, 0, 0)),
                  pl.BlockSpec((1, n_heads, QB, head_dim), blk),
                  pl.BlockSpec((1, seq, kv_w), lambda b, i: (b, 0, 0)),
                  pl.BlockSpec((1, kv_w, seq), lambda b, i: (b, 0, 0))],
        out_specs=pl.BlockSpec((QB, n_heads * head_dim), lambda b, i: (b * nblk + i, 0)),
        out_shape=jax.ShapeDtypeStruct((n_seq * seq, n_heads * head_dim), F32),
        scratch_shapes=[pltpu.VMEM((seq, QB), I32),
                        pltpu.VMEM((1, n_heads * QB), F32),
                        pltpu.VMEM((1, n_heads * QB), F32),
                        pltpu.VMEM((head_dim, n_heads * QB), F32),
                        pltpu.VMEM((1, QB), I32)],
        compiler_params=_cparams("arbitrary", "arbitrary"),
        name="attn_prompt",
    )(qi_hm, w_hm, ki_bf.reshape(n_seq, seq, IDX_DIM), q_hm, k_bf.reshape(n_seq, seq, kv_w), vt_bf)


SCORE_PAGES = 32
ATTN_PAGES = 16


def _sample_score_kernel(pt_ref, qi_ref, w_ref, kin_ref, *refs, t_new):
    pages, (sp_ref, sn_ref) = refs[:SCORE_PAGES], refs[SCORE_PAGES:]
    page = pages[0].shape[2]
    qi = qi_ref[0]
    rows = qi.shape[0]
    wb = jnp.broadcast_to(w_ref[0] * IDX_SCALE, (rows, page))

    def score(keys_t_bf):
        s = jnp.dot(qi, keys_t_bf, preferred_element_type=F32)
        r = jnp.maximum(s, 0.0) * wb
        return jnp.sum(r.reshape(N_IDX_HEADS, t_new, page), axis=0)

    for j in range(SCORE_PAGES):
        sp_ref[0, :, j * page:(j + 1) * page] = score(pages[j][0].astype(BF16))

    @pl.when(pl.program_id(1) == 0)
    def _():
        sn = score(kin_ref[0])
        s_idx = lax.broadcasted_iota(I32, (t_new, page), 1)
        t_idx = lax.broadcasted_iota(I32, (t_new, page), 0)
        sn_ref[0] = jnp.where(s_idx <= t_idx, sn, -jnp.inf)


def _sample_scores(page_table, qi_s, w_s, ki_new_t, cache_kidx_t, t_new):
    n_seq, n_pages = page_table.shape
    page = cache_kidx_t.shape[2]
    rows = qi_s.shape[1]
    page_specs = [pl.BlockSpec((1, IDX_DIM, page),
                               functools.partial(lambda b, c, pt, j: (pt[b, c * SCORE_PAGES + j], 0, 0), j=j))
                  for j in range(SCORE_PAGES)]
    grid_spec = pltpu.PrefetchScalarGridSpec(
        num_scalar_prefetch=1,
        grid=(n_seq, n_pages // SCORE_PAGES),
        in_specs=[pl.BlockSpec((1, rows, IDX_DIM), lambda b, c, pt: (b, 0, 0)),
                  pl.BlockSpec((1, rows, 1), lambda b, c, pt: (b, 0, 0)),
                  pl.BlockSpec((1, IDX_DIM, page), lambda b, c, pt: (b, 0, 0))] + page_specs,
        out_specs=[pl.BlockSpec((1, t_new, SCORE_PAGES * page), lambda b, c, pt: (b, 0, c)),
                   pl.BlockSpec((1, t_new, page), lambda b, c, pt: (b, 0, 0))],
    )
    return pl.pallas_call(
        functools.partial(_sample_score_kernel, t_new=t_new),
        grid_spec=grid_spec,
        out_shape=[jax.ShapeDtypeStruct((n_seq, t_new, n_pages * page), F32),
                   jax.ShapeDtypeStruct((n_seq, t_new, page), F32)],
        compiler_params=_cparams("arbitrary", "arbitrary"),
        name="sample_scores",
    )(page_table, qi_s, w_s, ki_new_t, *([cache_kidx_t] * SCORE_PAGES))


def _sample_thresh_kernel(sp_ref, sn_ref, ex_ref, mask_ref, key_ref, cut_ref, *, n_sel, n_past, idx_bits):
    rows = sp_ref.shape[0]
    n_tiles = n_past // LANES + 1
    key_ref[:, 0:n_past] = _order_key(sp_ref[...])
    key_ref[:, n_past:n_past + LANES] = _order_key(sn_ref[...])
    lane = lax.broadcasted_iota(I32, (rows, LANES), 1)

    def count(pred):
        def body(j, cnt):
            c0 = pl.multiple_of(j * LANES, LANES)
            return cnt + jnp.where(pred(key_ref[:, pl.ds(c0, LANES)], c0 + lane), 1, 0)
        cnt = lax.fori_loop(0, n_tiles, body, jnp.zeros((rows, LANES), I32), unroll=8)
        return jnp.broadcast_to(jnp.sum(cnt, axis=1, keepdims=True), (rows, LANES))

    def search(it, thr):
        cand = thr + lax.shift_left(jnp.int32(1), 31 - it)
        c = count(lambda kk, pos: kk >= cand)
        return jnp.where(c >= n_sel, cand, thr)

    thr = lax.fori_loop(0, 32, search, jnp.full((rows, LANES), INT_MIN, I32))
    cnt_ge = count(lambda kk, pos: kk >= thr)
    need = n_sel - count(lambda kk, pos: kk > thr)
    cut_ref[...] = jnp.full((rows, LANES), INT_MAX, I32)

    @pl.when(jnp.max(cnt_ge) > n_sel)
    def _():
        def search_pos(it, p):
            cand = p + lax.shift_left(jnp.int32(1), idx_bits - 1 - it)
            c = count(lambda kk, pos: jnp.where(kk == thr, pos, INT_MAX) < cand)
            return jnp.where(c < need, cand, p)
        cut_ref[...] = lax.fori_loop(0, idx_bits, search_pos, jnp.zeros((rows, LANES), I32))

    few = thr <= KEY_NEG_INF
    thr_eff = jnp.where(few, KEY_NEG_INF + 1, thr)
    cut = jnp.where(few, INT_MAX, cut_ref[...])

    ex = ex_ref[...]
    width = ex.shape[1]

    def emit(j, carry):
        c0 = pl.multiple_of(j * LANES, LANES)
        hit = jnp.where((key_ref[:, pl.ds(c0, LANES)] - jnp.where(c0 + lane > cut, 1, 0)) >= thr_eff, 1.0, 0.0)
        mask_ref[:, pl.ds(pl.multiple_of(j * width, width), width)] = jnp.dot(
            hit.astype(BF16), ex, preferred_element_type=F32)
        return carry

    lax.fori_loop(0, n_tiles, emit, 0, unroll=4)


def _sample_thresh(sp, sn, n_sel, rows_per_step=64):
    m, n_past = sp.shape
    rows_per_step = min(rows_per_step, m)
    tok = jnp.arange(LANES, dtype=I32)[:, None]
    col = jnp.arange(LANES * N_KV_HEADS, dtype=I32)[None, :]
    expand = (col // N_KV_HEADS == tok).astype(BF16)
    width = (n_past + LANES) * N_KV_HEADS
    return pl.pallas_call(
        functools.partial(_sample_thresh_kernel, n_sel=n_sel, n_past=n_past,
                          idx_bits=int(n_past + LANES).bit_length()),
        grid=(m // rows_per_step,),
        in_specs=[pl.BlockSpec((rows_per_step, n_past), lambda r: (r, 0)),
                  pl.BlockSpec((rows_per_step, LANES), lambda r: (r, 0)),
                  pl.BlockSpec(expand.shape, lambda r: (0, 0))],
        out_specs=pl.BlockSpec((rows_per_step, width), lambda r: (r, 0)),
        out_shape=jax.ShapeDtypeStruct((m, width), F32),
        scratch_shapes=[pltpu.VMEM((rows_per_step, n_past + LANES), I32),
                        pltpu.VMEM((rows_per_step, LANES), I32)],
        compiler_params=_cparams("arbitrary"),
        name="sample_thresh",
    )(sp, sn, expand)


def _sample_attn_kernel(pt_ref, q_ref, mask_ref, maskn_ref, kn_ref, vn_ref, *refs, c_exp, group_rows):
    k_pages = refs[:ATTN_PAGES]
    v_pages = refs[ATTN_PAGES:2 * ATTN_PAGES]
    o_ref, m_ref, l_ref, acc_ref = refs[2 * ATTN_PAGES:]
    c = pl.program_id(1)
    q = q_ref[0]
    rows = q.shape[0]
    width = maskn_ref.shape[2]
    reps = rows // mask_ref.shape[1]
    own_head = jnp.where(lax.broadcasted_iota(I32, (rows, width), 1) % N_KV_HEADS ==
                         lax.broadcasted_iota(I32, (rows, width), 0) // group_rows, 1.0, 0.0)

    @pl.when(c == 0)
    def _():
        m_ref[...] = jnp.full(m_ref.shape, M_INIT, F32)
        l_ref[...] = jnp.zeros(l_ref.shape, F32)
        acc_ref[...] = jnp.zeros(acc_ref.shape, F32)

    def select(flags):
        return jnp.concatenate([flags] * reps, axis=0) * own_head > 0.5

    def attend(tiles):
        raws = [jnp.where(sel, lax.dot_general(q, k, _CONTRACT_LAST, preferred_element_type=F32), NEG_BIG)
                for sel, k, _ in tiles]
        m_old = m_ref[...]
        m_new = jnp.maximum(m_old, jnp.max(functools.reduce(jnp.maximum, raws), axis=1, keepdims=True))
        alpha = jnp.exp2((m_old - m_new) * c_exp)
        ps = [jnp.exp2((raw - m_new) * c_exp) for raw in raws]
        l_ref[...] = alpha * l_ref[...] + jnp.sum(functools.reduce(jnp.add, ps), axis=1, keepdims=True)
        m_ref[...] = m_new
        pv = functools.reduce(jnp.add, [jnp.dot(p.astype(BF16), v, preferred_element_type=F32)
                                        for p, (_, _, v) in zip(ps, tiles)])
        acc_ref[...] = alpha * acc_ref[...] + pv

    attend([(select(mask_ref[0, :, j * width:(j + 1) * width]),
             k_pages[j][0].astype(BF16), v_pages[j][0].astype(BF16)) for j in range(ATTN_PAGES)])

    @pl.when(c == pl.num_programs(1) - 1)
    def _():
        attend([(select(maskn_ref[0]), kn_ref[0], vn_ref[0])])
        o_ref[0] = acc_ref[...] / l_ref[...]


def _sample_attn(page_table, q_s, mask, k_new, v_new, cache_k, cache_v, t_new):
    n_seq, n_pages = page_table.shape
    page_rows, head_dim = cache_k.shape[1], cache_k.shape[2]
    rows = q_s.shape[1]
    page_specs = [pl.BlockSpec((1, page_rows, head_dim),
                               functools.partial(lambda b, c, pt, j: (pt[b, c * ATTN_PAGES + j], 0, 0), j=j))
                  for j in range(ATTN_PAGES)]
    per_seq = lambda shape: pl.BlockSpec((1,) + shape, lambda b, c, pt: (b, 0, 0))
    grid_spec = pltpu.PrefetchScalarGridSpec(
        num_scalar_prefetch=1,
        grid=(n_seq, n_pages // ATTN_PAGES),
        in_specs=[per_seq((rows, head_dim)),
                  pl.BlockSpec((1, t_new, ATTN_PAGES * page_rows), lambda b, c, pt: (b, 0, c)),
                  pl.BlockSpec((1, t_new, page_rows), lambda b, c, pt: (b, 0, n_pages)),
                  per_seq((page_rows, head_dim)), per_seq((page_rows, head_dim))] + page_specs + page_specs,
        out_specs=per_seq((rows, head_dim)),
        scratch_shapes=[pltpu.VMEM((rows, 1), F32),
                        pltpu.VMEM((rows, 1), F32),
                        pltpu.VMEM((rows, head_dim), F32)],
    )
    return pl.pallas_call(
        functools.partial(_sample_attn_kernel, c_exp=head_dim ** -0.5 * LOG2_E, group_rows=rows // N_KV_HEADS),
        grid_spec=grid_spec,
        out_shape=jax.ShapeDtypeStruct((n_seq, rows, head_dim), F32),
        compiler_params=_cparams("arbitrary", "arbitrary"),
        name="sample_attn",
    )(page_table, q_s, mask, mask, k_new, v_new, *([cache_k] * ATTN_PAGES), *([cache_v] * ATTN_PAGES))


def _merge_kernel(x_ref, gt_ref, co_ref, ao_ref, go_ref, wc_ref, wa_ref, o_ref, on_ref, *, n_heads):
    @pl.when(pl.program_id(1) == 0)
    def _():
        g = go_ref[...]
        for h in range(n_heads):
            cs = slice(h * LANES, (h + 1) * LANES)
            o = ao_ref[:, cs]
            on_ref[:, cs] = (o * lax.rsqrt(jnp.mean(o * o, axis=-1, keepdims=True) + EPS) * g).astype(BF16)

    y = jnp.dot(co_ref[...].astype(BF16), wc_ref[...].astype(BF16), preferred_element_type=F32)
    y = y + jnp.dot(on_ref[...], wa_ref[...].astype(BF16), preferred_element_type=F32)
    o_ref[...] = x_ref[...] + gt_ref[0] * y


def _merge(x, gt, conv_o, attn_o, g_o, w_out, tm, tn=512):
    m, d = x.shape
    g, r, _ = gt.shape
    d_conv = conv_o.shape[1]
    d_attn = attn_o.shape[1]
    tiles_per_group = (m // tm) // g
    rb = d_conv // d_attn
    return pl.pallas_call(
        functools.partial(_merge_kernel, n_heads=d_attn // LANES),
        grid=(m // tm, d // tn),
        in_specs=[pl.BlockSpec((tm, tn), lambda i, j: (i, j)),
                  pl.BlockSpec((1, r, tn), lambda i, j: (i // tiles_per_group, 0, j)),
                  pl.BlockSpec((tm, d_conv), lambda i, j: (i, 0)),
                  pl.BlockSpec((tm, d_attn), lambda i, j: (i, 0)),
                  pl.BlockSpec((1, LANES), lambda i, j: (0, 0)),
                  pl.BlockSpec((d_conv, tn), lambda i, j: (0, j)),
                  pl.BlockSpec((d_attn, tn), lambda i, j: (rb, j))],
        out_specs=pl.BlockSpec((tm, tn), lambda i, j: (i, j)),
        out_shape=jax.ShapeDtypeStruct((m, d), F32),
        scratch_shapes=[pltpu.VMEM((tm, d_attn), BF16)],
        compiler_params=_cparams("arbitrary", "arbitrary"),
        name="merge",
    )(x, gt, conv_o, attn_o, g_o.reshape(1, -1), w_out, w_out)


def _ffn_act(cur_g, p1_g, p2_g, cur_v, p1_v, p2_v, wg_ref, wv_ref, bg_ref, bv_ref):
    gate = p2_g * wg_ref[0:1, :] + p1_g * wg_ref[1:2, :] + cur_g * wg_ref[2:3, :] + bg_ref[...]
    val = p2_v * wv_ref[0:1, :] + p1_v * wv_ref[1:2, :] + cur_v * wv_ref[2:3, :] + bv_ref[...]
    return (_silu(gate) * val).astype(BF16)


def _ffn_finish(f, act, wd_ref, x_ref, gt_ref, o_ref, acc_ref):
    @pl.when(f == 0)
    def _():
        acc_ref[...] = jnp.zeros(acc_ref.shape, F32)

    acc_ref[...] += jnp.dot(act, wd_ref[...].astype(BF16), preferred_element_type=F32)

    @pl.when(f == pl.num_programs(1) - 1)
    def _():
        o_ref[...] = x_ref[...] + gt_ref[0] * acc_ref[...]


def _ffn_prompt_kernel(ug_ref, uv_ref, hg_ref, hv_ref, wg_ref, wv_ref, bg_ref, bv_ref, wd_ref, x_ref, gt_ref,
                       o_ref, acc_ref, hist_g, hist_v, *, tm, tiles_per_seq):
    i, f = pl.program_id(0), pl.program_id(1)
    first = (i % tiles_per_seq) == 0
    hist_g[0:SUBLANES, :] = jnp.where(first, 0.0, hg_ref[...])
    hist_v[0:SUBLANES, :] = jnp.where(first, 0.0, hv_ref[...])
    hist_g[SUBLANES:SUBLANES + tm, :] = ug_ref[...]
    hist_v[SUBLANES:SUBLANES + tm, :] = uv_ref[...]
    act = _ffn_act(ug_ref[...], hist_g[SUBLANES - 1:SUBLANES - 1 + tm, :], hist_g[SUBLANES - 2:SUBLANES - 2 + tm, :],
                   uv_ref[...], hist_v[SUBLANES - 1:SUBLANES - 1 + tm, :], hist_v[SUBLANES - 2:SUBLANES - 2 + tm, :],
                   wg_ref, wv_ref, bg_ref, bv_ref)
    _ffn_finish(f, act, wd_ref, x_ref, gt_ref, o_ref, acc_ref)


def _ffn_prompt(u, x1, gt, w_dw, b_dw, w_down, seq, tm=512, tf=512):
    m, d = x1.shape
    d_ff = w_down.shape[0]
    nf = d_ff // tf
    tiles_per_seq = seq // tm
    fw = w_dw.shape[0]
    halo_idx = lambda i: jnp.maximum(i * (tm // SUBLANES) - 1, 0)
    return pl.pallas_call(
        functools.partial(_ffn_prompt_kernel, tm=tm, tiles_per_seq=tiles_per_seq),
        grid=(m // tm, nf),
        in_specs=[pl.BlockSpec((tm, tf), lambda i, f: (i, f)),
                  pl.BlockSpec((tm, tf), lambda i, f: (i, f + nf)),
                  pl.BlockSpec((SUBLANES, tf), lambda i, f: (halo_idx(i), f)),
                  pl.BlockSpec((SUBLANES, tf), lambda i, f: (halo_idx(i), f + nf)),
                  pl.BlockSpec((fw, tf), lambda i, f: (0, f)),
                  pl.BlockSpec((fw, tf), lambda i, f: (0, f + nf)),
                  pl.BlockSpec((1, tf), lambda i, f: (0, f)),
                  pl.BlockSpec((1, tf), lambda i, f: (0, f + nf)),
                  pl.BlockSpec((tf, d), lambda i, f: (f, 0)),
                  pl.BlockSpec((tm, d), lambda i, f: (i, 0)),
                  pl.BlockSpec((1, 1, d), lambda i, f: (i // tiles_per_seq, 0, 0))],
        out_specs=pl.BlockSpec((tm, d), lambda i, f: (i, 0)),
        out_shape=jax.ShapeDtypeStruct((m, d), F32),
        scratch_shapes=[pltpu.VMEM((tm, d), F32),
                        pltpu.VMEM((SUBLANES + tm, tf), F32),
                        pltpu.VMEM((SUBLANES + tm, tf), F32)],
        compiler_params=_cparams("arbitrary", "arbitrary"),
        name="ffn_prompt",
    )(u, u, u, u, w_dw, w_dw, b_dw.reshape(1, -1), b_dw.reshape(1, -1), w_down, x1, gt)


def _ffn_sample_kernel(cg_ref, cv_ref, p1g_ref, p1v_ref, p2g_ref, p2v_ref, wg_ref, wv_ref, bg_ref, bv_ref,
                       wd_ref, x_ref, gt_ref, o_ref, acc_ref):
    act = _ffn_act(cg_ref[...], p1g_ref[...], p2g_ref[...], cv_ref[...], p1v_ref[...], p2v_ref[...],
                   wg_ref, wv_ref, bg_ref, bv_ref)
    _ffn_finish(pl.program_id(1), act, wd_ref, x_ref, gt_ref, o_ref, acc_ref)


def _ffn_sample(cur, prev1, prev2, x1, gt, w_dw, b_dw, w_down, tf=512):
    m, d = x1.shape
    d_ff = w_down.shape[0]
    nf = d_ff // tf
    fw = w_dw.shape[0]
    lo = pl.BlockSpec((m, tf), lambda i, f: (0, f))
    hi = pl.BlockSpec((m, tf), lambda i, f: (0, f + nf))
    return pl.pallas_call(
        _ffn_sample_kernel,
        grid=(1, nf),
        in_specs=[lo, hi, lo, hi, lo, hi,
                  pl.BlockSpec((fw, tf), lambda i, f: (0, f)),
                  pl.BlockSpec((fw, tf), lambda i, f: (0, f + nf)),
                  pl.BlockSpec((1, tf), lambda i, f: (0, f)),
                  pl.BlockSpec((1, tf), lambda i, f: (0, f + nf)),
                  pl.BlockSpec((tf, d), lambda i, f: (f, 0)),
                  pl.BlockSpec((m, d), lambda i, f: (0, 0)),
                  pl.BlockSpec((1, m, d), lambda i, f: (0, 0, 0))],
        out_specs=pl.BlockSpec((m, d), lambda i, f: (0, 0)),
        out_shape=jax.ShapeDtypeStruct((m, d), F32),
        scratch_shapes=[pltpu.VMEM((m, d), F32)],
        compiler_params=_cparams("arbitrary", "arbitrary"),
        name="ffn_sample",
    )(cur, cur, prev1, prev1, prev2, prev2, w_dw, w_dw, b_dw.reshape(1, -1), b_dw.reshape(1, -1),
      w_down, x1, gt)


def _rope_tables(pos):
    def tab(dim):
        inv = jnp.power(ROPE_THETA, -jnp.arange(0, dim, 2, dtype=F32) / dim)
        ang = pos.astype(F32)[:, None] * inv[None, :]
        cos, sin = jnp.cos(ang), jnp.sin(ang)
        reps = LANES // dim
        return (jnp.tile(jnp.concatenate([cos, cos], axis=-1), (1, reps)),
                jnp.tile(jnp.concatenate([-sin, sin], axis=-1), (1, reps)))
    return tab(LANES) + tab(IDX_DIM)


def _in_proj(x2d, norm1, sc, sh, w_in_t, n_main, tm):
    z = _normmod_matmul(x2d, norm1, sc, sh, w_in_t, n_main, tm, 512, "in_proj", w_transposed=True)
    w_tail = jnp.pad(w_in_t[n_main:], ((0, LANES - (w_in_t.shape[0] - n_main)), (0, 0)))
    zt = _normmod_matmul(x2d, norm1, sc, sh, w_tail, LANES, tm, LANES, "in_proj_tail", w_transposed=True)
    return z, zt


def kernel(x_prompt, x_sample, cache_k, cache_v, cache_kidx, state_conv, state_ffn, page_table, c_prompt, c_sample, norm1, w_ada, b_ada, w_in, g_q, g_k, w_dw_a, b_dw_a, gn_g, gn_b, g_o, w_out, norm2, w_up, w_dw_f, b_dw_f, w_down):
    n_b, seq, d = x_prompt.shape
    n_s, t_new, _ = x_sample.shape
    depth = norm1.shape[0]
    assert depth == 1
    head_dim = g_q.shape[-1]
    assert head_dim == LANES
    d_conv = w_dw_a.shape[-1]
    d_attn = w_out.shape[1] - d_conv
    n_heads = d_attn // head_dim
    kv_w = N_KV_HEADS * head_dim
    n_main = 2 * d_conv + d_attn + 2 * kv_w + N_IDX_HEADS * IDX_DIM
    n_pool, page = cache_k.shape[1], cache_k.shape[2]
    n_pages = page_table.shape[1]
    n_past = n_pages * page
    mp, ms = n_b * seq, n_s * t_new
    heads_per_kv = n_heads // N_KV_HEADS

    n_c = n_b + n_s
    pad_c = (-n_c) % SUBLANES
    c_all = jnp.concatenate([c_prompt, c_sample, jnp.zeros((pad_c, d), F32)], axis=0)
    mods = _ada(c_all, w_ada[0], b_ada[0])
    mp6 = mods[:n_b].reshape(n_b, 6, 1, d)
    sh1p, sc1p, gt1p, sh2p, sc2p, gt2p = [mp6[:, k] for k in range(6)]
    ms6 = jnp.repeat(mods[n_b:n_c].reshape(n_s, 6, d), t_new, axis=0)
    sh1s, sc1s, gt1s, sh2s, sc2s, gt2s = [ms6[:, k][None] for k in range(6)]

    w_in_t = jnp.swapaxes(w_in[0], 0, 1)

    xp = x_prompt.reshape(mp, d)
    tm_p = 1024 if seq % 1024 == 0 else QB
    z, zt = _in_proj(xp, norm1[0], sc1p, sh1p, w_in_t, n_main, tm_p)
    conv_o, new_conv_p = _conv_prompt(z, n_b, seq, d_conv, w_dw_a[0], b_dw_a[0], gn_g[0], gn_b[0],
                                      tm=256 if seq % 256 == 0 else QB)
    tabs_p = _rope_tables(jnp.arange(seq, dtype=I32))
    q_hm, qi_hm, k_p, k_bf, vt_bf, ki_p, ki_bf = _qk_epilogue(z, zt, tabs_p, g_q[0], g_k[0], n_b, seq, n_heads)
    w_hm = zt[:, IDX_DIM:IDX_DIM + N_IDX_HEADS].reshape(mp // QB, QB, N_IDX_HEADS).transpose(0, 2, 1)
    n_sel_p = min(TOPK_MAX, seq // 4)
    attn_p = _attn_prompt(qi_hm, w_hm, ki_bf, q_hm, k_bf, vt_bf, n_b, seq, n_sel_p)
    x1p = _merge(xp, gt1p, conv_o, attn_p, g_o[0], w_out[0], tm_p)
    u_p = _normmod_matmul(x1p, norm2[0], sc2p, sh2p, w_up[0], w_up.shape[-1], tm_p, 512, "up_proj")
    y_p = _ffn_prompt(u_p, x1p, gt2p, w_dw_f[0], b_dw_f[0], w_down[0], seq,
                      tm=512 if seq % 512 == 0 else QB)
    v_p = z[:, 2 * d_conv + d_attn + kv_w:2 * d_conv + d_attn + 2 * kv_w]
    fw = w_dw_f.shape[1]
    new_ffn_p = u_p.reshape(n_b, seq, -1)[:, seq - (fw - 1):]

    xs = x_sample.reshape(ms, d)
    zs, zts = _in_proj(xs, norm1[0], sc1s, sh1s, w_in_t, n_main, ms)
    conv_os, new_conv_s = _conv_sample(zs, state_conv[0], d_conv, w_dw_a[0], b_dw_a[0], gn_g[0], gn_b[0])
    pos_s = jnp.tile(n_past + jnp.arange(t_new, dtype=I32), ms // t_new)
    tabs_s = _rope_tables(pos_s)
    q_hm_s, qi_hm_s, k_s, k_bf_s, _, ki_s, ki_bf_s = _qk_epilogue(zs, zts, tabs_s, g_q[0], g_k[0], 1, ms, n_heads)
    v_s = zs[:, 2 * d_conv + d_attn + kv_w:2 * d_conv + d_attn + 2 * kv_w]
    seq_rows = lambda a, nh: (a.reshape(ms // QB, nh, QB // t_new, t_new, a.shape[-1])
                              .transpose(0, 2, 1, 3, 4).reshape(n_s, nh * t_new, a.shape[-1]))
    qi_s = seq_rows(qi_hm_s, N_IDX_HEADS)
    q_s = seq_rows(q_hm_s, n_heads)
    w_s = (zts[:, IDX_DIM:IDX_DIM + N_IDX_HEADS].reshape(n_s, t_new, N_IDX_HEADS)
           .transpose(0, 2, 1).reshape(n_s, N_IDX_HEADS * t_new, 1))
    ki_new_t = jnp.pad(ki_bf_s.reshape(n_s, t_new, IDX_DIM).transpose(0, 2, 1), ((0, 0), (0, 0), (0, page - t_new)))
    kv_new = lambda a: jnp.pad(a.reshape(n_s, t_new * N_KV_HEADS, head_dim),
                               ((0, 0), (0, (page - t_new) * N_KV_HEADS), (0, 0)))
    kidx_t = jnp.swapaxes(cache_kidx[0], 1, 2)
    sp, sn = _sample_scores(page_table, qi_s, w_s, ki_new_t, kidx_t, t_new)
    n_sel_s = min(TOPK_MAX, (n_past + t_new) // 4)
    mask = _sample_thresh(sp.reshape(ms, n_past), sn.reshape(ms, page), n_sel_s)
    o_s = _sample_attn(page_table, q_s, mask.reshape(n_s, t_new, -1), kv_new(k_bf_s), kv_new(v_s.astype(BF16)),
                       cache_k[0].reshape(n_pool, page * N_KV_HEADS, head_dim),
                       cache_v[0].reshape(n_pool, page * N_KV_HEADS, head_dim), t_new)
    attn_s = o_s.reshape(n_s, n_heads, t_new, head_dim).transpose(0, 2, 1, 3).reshape(ms, d_attn)
    x1s = _merge(xs, gt1s, conv_os, attn_s, g_o[0], w_out[0], ms)
    u_s = _normmod_matmul(x1s, norm2[0], sc2s, sh2s, w_up[0], w_up.shape[-1], ms, 512, "up_proj")
    u_hist = jnp.concatenate([state_ffn[0], u_s.reshape(n_s, t_new, -1)], axis=1)
    shifted = lambda k: u_hist[:, k:k + t_new].reshape(ms, -1)
    y_s = _ffn_sample(shifted(2), shifted(1), shifted(0), x1s, gt2s, w_dw_f[0], b_dw_f[0], w_down[0])
    new_ffn_s = u_hist[:, t_new:]

    return (y_p.reshape(n_b, seq, d), y_s.reshape(n_s, t_new, d),
            k_p.reshape(1, n_b, seq, N_KV_HEADS, head_dim), v_p.reshape(1, n_b, seq, N_KV_HEADS, head_dim),
            ki_p.reshape(1, n_b, seq, IDX_DIM), new_conv_p[None], new_ffn_p[None],
            k_s.reshape(1, n_s, t_new, N_KV_HEADS, head_dim), v_s.reshape(1, n_s, t_new, N_KV_HEADS, head_dim),
            ki_s.reshape(1, n_s, t_new, IDX_DIM), new_conv_s[None], new_ffn_s[None])
```

```python
import functools

import jax
import jax.numpy as jnp
from jax import lax
from jax.experimental import pallas as pl
from jax.experimental.pallas import tpu as pltpu

F32 = jnp.float32
BF16 = jnp.bfloat16
I32 = jnp.int32

CONV_GROUPS = 8
N_KV_HEADS = 2
N_IDX_HEADS = 16
IDX_DIM = 64
TOPK_MAX = 256
ROPE_THETA = 10000.0
EPS = 1e-6
IDX_SCALE = (IDX_DIM ** -0.5) * (N_IDX_HEADS ** -0.5)

LANES = 128
SUBLANES = 8
QB = 128
VMEM_LIMIT_BYTES = 56 * 1024 * 1024
NEG_BIG = -1e30
M_INIT = -1e29
LOG2_E = 1.4426950408889634
INT_MIN = -2 ** 31
INT_MAX = 2 ** 31 - 1
KEY_NEG_INF = -2139095041


def _cparams(*sem):
    return pltpu.CompilerParams(dimension_semantics=sem, vmem_limit_bytes=VMEM_LIMIT_BYTES)


def _silu(x):
    return x * jax.nn.sigmoid(x)


def _order_key(x):
    bits = pltpu.bitcast(x, I32)
    return bits ^ ((bits >> 31) & INT_MAX)


def _ada_kernel(c_ref, w_ref, b_ref, o_ref):
    s = _silu(c_ref[...]).astype(BF16)
    o_ref[...] = jnp.dot(s, w_ref[...].astype(BF16), preferred_element_type=F32) + b_ref[...]


def _ada(c_all, w_ada, b_ada, tn=1024):
    r, d = c_all.shape
    n = w_ada.shape[1]
    return pl.pallas_call(
        _ada_kernel,
        grid=(n // tn,),
        in_specs=[pl.BlockSpec((r, d), lambda j: (0, 0)),
                  pl.BlockSpec((d, tn), lambda j: (0, j)),
                  pl.BlockSpec((1, tn), lambda j: (0, j))],
        out_specs=pl.BlockSpec((r, tn), lambda j: (0, j)),
        out_shape=jax.ShapeDtypeStruct((r, n), F32),
        compiler_params=_cparams("arbitrary"),
        name="ada",
    )(c_all, w_ada, b_ada.reshape(1, n))


_CONTRACT_LAST = (((1,), (1,)), ((), ()))


def _normmod_matmul_kernel(x_ref, g_ref, sc_ref, sh_ref, w_ref, o_ref, h_ref, *, w_transposed):
    @pl.when(pl.program_id(1) == 0)
    def _():
        x = x_ref[...]
        y = x * lax.rsqrt(jnp.mean(x * x, axis=-1, keepdims=True) + EPS) * g_ref[...]
        h_ref[...] = (y * (1.0 + sc_ref[0]) + sh_ref[0]).astype(BF16)

    w = w_ref[...].astype(BF16)
    if w_transposed:
        o_ref[...] = lax.dot_general(h_ref[...], w, _CONTRACT_LAST, preferred_element_type=F32)
    else:
        o_ref[...] = jnp.dot(h_ref[...], w, preferred_element_type=F32)


def _normmod_matmul(x, gain, sc, sh, w, ncols, tm, tn, name, w_transposed=False):
    m, d = x.shape
    g, r, _ = sc.shape
    tiles_per_group = (m // tm) // g
    mod_spec = pl.BlockSpec((1, r, d), lambda i, j: (i // tiles_per_group, 0, 0))
    w_spec = pl.BlockSpec((tn, d), lambda i, j: (j, 0)) if w_transposed else pl.BlockSpec((d, tn), lambda i, j: (0, j))
    return pl.pallas_call(
        functools.partial(_normmod_matmul_kernel, w_transposed=w_transposed),
        grid=(m // tm, ncols // tn),
        in_specs=[pl.BlockSpec((tm, d), lambda i, j: (i, 0)),
                  pl.BlockSpec((1, d), lambda i, j: (0, 0)),
                  mod_spec, mod_spec,
                  w_spec],
        out_specs=pl.BlockSpec((tm, tn), lambda i, j: (i, j)),
        out_shape=jax.ShapeDtypeStruct((m, ncols), F32),
        scratch_shapes=[pltpu.VMEM((tm, d), BF16)],
        compiler_params=_cparams("arbitrary", "arbitrary"),
        name=name,
    )(x, gain.reshape(1, d), sc, sh, w)


def _conv_gn_silu(hist_ref, off, rows, wdw_ref, bdw_ref, gng_ref, gnb_ref, o_ref, row_chunk):
    conv_w = wdw_ref.shape[0]
    for r0 in range(0, rows, row_chunk):
        for c in range(CONV_GROUPS):
            cs = slice(c * LANES, (c + 1) * LANES)
            acc = jnp.zeros((row_chunk, LANES), F32)
            for w in range(conv_w):
                acc = acc + hist_ref[off + r0 + w:off + r0 + w + row_chunk, cs] * wdw_ref[w:w + 1, cs]
            y = acc + bdw_ref[:, cs]
            mu = jnp.mean(y, axis=-1, keepdims=True)
            dlt = y - mu
            var = jnp.mean(dlt * dlt, axis=-1, keepdims=True)
            yn = dlt * lax.rsqrt(var + EPS) * gng_ref[:, cs] + gnb_ref[:, cs]
            o_ref[r0:r0 + row_chunk, cs] = _silu(yn).astype(o_ref.dtype)


HALO = 32


def _conv_prompt_kernel(za_ref, zg_ref, ha_ref, hg_ref, wdw_ref, bdw_ref, gng_ref, gnb_ref,
                        o_ref, newc_ref, hist_ref, *, tm, tiles_per_seq):
    i = pl.program_id(0)
    first = (i % tiles_per_seq) == 0
    a_halo = ha_ref[...] * jax.nn.sigmoid(hg_ref[...])
    hist_ref[0:HALO, :] = jnp.where(first, 0.0, a_halo)
    hist_ref[HALO:HALO + tm, :] = za_ref[...] * jax.nn.sigmoid(zg_ref[...])
    conv_w = wdw_ref.shape[0]
    _conv_gn_silu(hist_ref, HALO - (conv_w - 1), tm, wdw_ref, bdw_ref, gng_ref, gnb_ref, o_ref, 64)

    @pl.when((i % tiles_per_seq) == tiles_per_seq - 1)
    def _():
        newc_ref[0] = hist_ref[HALO + tm - (conv_w - 1):HALO + tm, :]


def _conv_prompt(z, n_seq, seq, d_conv, wdw, bdw, gng, gnb, tm=256):
    m = z.shape[0]
    conv_w = wdw.shape[0]
    tiles_per_seq = seq // tm
    cb = 1
    halo_idx = lambda i: jnp.maximum(i * (tm // HALO) - 1, 0)
    vec = pl.BlockSpec((1, d_conv), lambda i: (0, 0))
    return pl.pallas_call(
        functools.partial(_conv_prompt_kernel, tm=tm, tiles_per_seq=tiles_per_seq),
        grid=(m // tm,),
        in_specs=[pl.BlockSpec((tm, d_conv), lambda i: (i, 0)),
                  pl.BlockSpec((tm, d_conv), lambda i: (i, cb)),
                  pl.BlockSpec((HALO, d_conv), lambda i: (halo_idx(i), 0)),
                  pl.BlockSpec((HALO, d_conv), lambda i: (halo_idx(i), cb)),
                  pl.BlockSpec((conv_w, d_conv), lambda i: (0, 0)),
                  vec, vec, vec],
        out_specs=[pl.BlockSpec((tm, d_conv), lambda i: (i, 0)),
                   pl.BlockSpec((1, conv_w - 1, d_conv), lambda i: (i // tiles_per_seq, 0, 0))],
        out_shape=[jax.ShapeDtypeStruct((m, d_conv), BF16),
                   jax.ShapeDtypeStruct((n_seq, conv_w - 1, d_conv), F32)],
        scratch_shapes=[pltpu.VMEM((HALO + tm, d_conv), F32)],
        compiler_params=_cparams("arbitrary"),
        name="conv_prompt",
    )(z, z, z, z, wdw, bdw.reshape(1, -1), gng.reshape(1, -1), gnb.reshape(1, -1))


def _conv_sample_kernel(za_ref, zg_ref, st_ref, wdw_ref, bdw_ref, gng_ref, gnb_ref,
                        o_ref, newc_ref, hist_ref, *, t_new):
    conv_w = wdw_ref.shape[0]
    hist_ref[0:conv_w - 1, :] = st_ref[0]
    hist_ref[conv_w - 1:conv_w - 1 + t_new, :] = za_ref[...] * jax.nn.sigmoid(zg_ref[...])
    _conv_gn_silu(hist_ref, 0, t_new, wdw_ref, bdw_ref, gng_ref, gnb_ref, o_ref, t_new)
    newc_ref[0] = hist_ref[t_new:t_new + conv_w - 1, :]


def _conv_sample(z, state, d_conv, wdw, bdw, gng, gnb):
    n_seq, hist_rows, _ = state.shape
    conv_w = wdw.shape[0]
    m = z.shape[0]
    t_new = m // n_seq
    vec = pl.BlockSpec((1, d_conv), lambda b: (0, 0))
    return pl.pallas_call(
        functools.partial(_conv_sample_kernel, t_new=t_new),
        grid=(n_seq,),
        in_specs=[pl.BlockSpec((t_new, d_conv), lambda b: (b, 0)),
                  pl.BlockSpec((t_new, d_conv), lambda b: (b, 1)),
                  pl.BlockSpec((1, hist_rows, d_conv), lambda b: (b, 0, 0)),
                  pl.BlockSpec((conv_w, d_conv), lambda b: (0, 0)),
                  vec, vec, vec],
        out_specs=[pl.BlockSpec((t_new, d_conv), lambda b: (b, 0)),
                   pl.BlockSpec((1, hist_rows, d_conv), lambda b: (b, 0, 0))],
        out_shape=[jax.ShapeDtypeStruct((m, d_conv), F32),
                   jax.ShapeDtypeStruct((n_seq, hist_rows, d_conv), F32)],
        scratch_shapes=[pltpu.VMEM((hist_rows + t_new + SUBLANES, d_conv), F32)],
        compiler_params=_cparams("arbitrary"),
        name="conv_sample",
    )(z, z, state, wdw, bdw.reshape(1, -1), gng.reshape(1, -1), gnb.reshape(1, -1))


def _qk_kernel(zq_ref, zkv_ref, zqi0_ref, zqi1_ref, zt_ref, cos_ref, sin_ref, cosi_ref, sini_ref,
               gq_ref, gk_ref, qhm_ref, qihm_ref, k_ref, kbf_ref, vt_ref, ki_ref, kibf_ref, *, n_heads):
    cos, sin = cos_ref[...], sin_ref[...]
    cosi, sini = cosi_ref[...], sini_ref[...]
    tm = cos.shape[0]
    lane = lax.broadcasted_iota(I32, (tm, LANES), 1)
    low_half = (lane % IDX_DIM) < (IDX_DIM // 2)

    def norm_rope(x, g):
        y = x * lax.rsqrt(jnp.mean(x * x, axis=-1, keepdims=True) + EPS) * g
        return y * cos + pltpu.roll(y, LANES // 2, 1) * sin

    def rope_idx(x):
        r = jnp.where(low_half, pltpu.roll(x, LANES - IDX_DIM // 2, 1), pltpu.roll(x, IDX_DIM // 2, 1))
        return x * cosi + r * sini

    gq, gk = gq_ref[...], gk_ref[...]
    for h in range(n_heads):
        qhm_ref[0, h] = norm_rope(zq_ref[:, h * LANES:(h + 1) * LANES], gq).astype(BF16)
    kv_w = N_KV_HEADS * LANES
    for g in range(N_KV_HEADS):
        kg = norm_rope(zkv_ref[:, g * LANES:(g + 1) * LANES], gk)
        k_ref[:, g * LANES:(g + 1) * LANES] = kg
        kbf_ref[:, g * LANES:(g + 1) * LANES] = kg.astype(BF16)
    vt_ref[0] = zkv_ref[:, kv_w:2 * kv_w].T.astype(BF16)
    half = (N_IDX_HEADS * IDX_DIM) // 2
    for j in range(N_IDX_HEADS // 2):
        src = zqi0_ref if j * LANES < half else zqi1_ref
        c0 = (j * LANES) % half
        y = rope_idx(src[:, c0:c0 + LANES])
        qihm_ref[0, 2 * j] = y[:, :IDX_DIM].astype(BF16)
        qihm_ref[0, 2 * j + 1] = y[:, IDX_DIM:].astype(BF16)
    yk = rope_idx(zt_ref[...])[:, :IDX_DIM]
    ki_ref[...] = yk
    kibf_ref[...] = yk.astype(BF16)


def _qk_epilogue(z, zt, tabs, gq, gk, n_seq, seq, n_heads):
    m = z.shape[0]
    tm = QB
    head_dim = LANES
    cos, sin, cosi, sini = tabs
    tab_tiles = cos.shape[0] // tm
    kv_w = N_KV_HEADS * head_dim
    d_attn = n_heads * head_dim
    d_conv = d_attn
    q_cb = (2 * d_conv) // d_attn
    kv_cb = (2 * d_conv + d_attn) // (2 * kv_w)
    qi_w = (N_IDX_HEADS * IDX_DIM) // 2
    qi_cb = (2 * d_conv + d_attn + 2 * kv_w) // qi_w
    tiles_per_seq = seq // tm
    tab = pl.BlockSpec((tm, LANES), lambda i: (i % tab_tiles, 0))
    vec = pl.BlockSpec((1, LANES), lambda i: (0, 0))
    return pl.pallas_call(
        functools.partial(_qk_kernel, n_heads=n_heads),
        grid=(m // tm,),
        in_specs=[pl.BlockSpec((tm, d_attn), lambda i: (i, q_cb)),
                  pl.BlockSpec((tm, 2 * kv_w), lambda i: (i, kv_cb)),
                  pl.BlockSpec((tm, qi_w), lambda i: (i, qi_cb)),
                  pl.BlockSpec((tm, qi_w), lambda i: (i, qi_cb + 1)),
                  pl.BlockSpec((tm, LANES), lambda i: (i, 0)),
                  tab, tab, tab, tab, vec, vec],
        out_specs=[pl.BlockSpec((1, n_heads, tm, head_dim), lambda i: (i, 0, 0, 0)),
                   pl.BlockSpec((1, N_IDX_HEADS, tm, IDX_DIM), lambda i: (i, 0, 0, 0)),
                   pl.BlockSpec((tm, kv_w), lambda i: (i, 0)),
                   pl.BlockSpec((tm, kv_w), lambda i: (i, 0)),
                   pl.BlockSpec((1, kv_w, tm), lambda i: (i // tiles_per_seq, 0, i % tiles_per_seq)),
                   pl.BlockSpec((tm, IDX_DIM), lambda i: (i, 0)),
                   pl.BlockSpec((tm, IDX_DIM), lambda i: (i, 0))],
        out_shape=[jax.ShapeDtypeStruct((m // tm, n_heads, tm, head_dim), BF16),
                   jax.ShapeDtypeStruct((m // tm, N_IDX_HEADS, tm, IDX_DIM), BF16),
                   jax.ShapeDtypeStruct((m, kv_w), F32),
                   jax.ShapeDtypeStruct((m, kv_w), BF16),
                   jax.ShapeDtypeStruct((n_seq, kv_w, seq), BF16),
                   jax.ShapeDtypeStruct((m, IDX_DIM), F32),
                   jax.ShapeDtypeStruct((m, IDX_DIM), BF16)],
        compiler_params=_cparams("arbitrary"),
        name="qk_epilogue",
    )(z, z, z, z, zt, cos, sin, cosi, sini, gq.reshape(1, -1), gk.reshape(1, -1))


KT = 2 * QB


def _attn_prompt_kernel(qi_ref, w_ref, ki_ref, q_ref, k_ref, vt_ref, o_ref,
                        key_ref, m_ref, l_ref, acc_ref, cut_ref, lg_ref, *, n_sel, c_exp, n_heads, idx_bits):
    i = pl.program_id(1)
    n_kt = lax.div(i * QB + QB + KT - 1, KT)
    row_iota = lax.broadcasted_iota(I32, (KT, QB), 0)
    w_all = w_ref[0] * IDX_SCALE
    heads_per_kv = n_heads // N_KV_HEADS
    hc = 4

    def score_tile(kt, carry):
        for half in range(KT // QB):
            ks = pl.multiple_of(kt * KT + half * QB, QB)
            ki_t = ki_ref[0, pl.ds(ks, QB), :]
            acc = jnp.zeros((QB, QB), F32)
            for h0 in range(0, N_IDX_HEADS, hc):
                s = lax.dot_general(ki_t, qi_ref[0, h0:h0 + hc].reshape(hc * QB, IDX_DIM),
                                    _CONTRACT_LAST, preferred_element_type=F32)
                for h in range(hc):
                    acc = acc + jnp.maximum(s[:, h * QB:(h + 1) * QB], 0.0) * w_all[h0 + h:h0 + h + 1, :]
            acc = jnp.where(ks + lax.broadcasted_iota(I32, (QB, QB), 0) <=
                            i * QB + lax.broadcasted_iota(I32, (QB, QB), 1), acc, -jnp.inf)
            key_ref[pl.ds(ks, QB), :] = _order_key(acc)
        return carry

    lax.fori_loop(0, n_kt, score_tile, 0)

    def count(pred):
        def body(kt, cnt):
            ks = pl.multiple_of(kt * KT, KT)
            hit = jnp.where(pred(key_ref[pl.ds(ks, KT), :], ks + row_iota), 1, 0)
            return cnt + jnp.sum(hit.reshape(KT // SUBLANES, SUBLANES, QB), axis=0)
        cnt = lax.fori_loop(0, n_kt, body, jnp.zeros((SUBLANES, QB), I32))
        return jnp.sum(cnt, axis=0, keepdims=True)

    def search(it, thr):
        cand = thr + lax.shift_left(jnp.int32(1), 31 - it)
        c = count(lambda kk, pos: kk >= cand)
        return jnp.where(c >= n_sel, cand, thr)

    thr = lax.fori_loop(0, 32, search, jnp.full((1, QB), INT_MIN, I32))
    cnt_ge = count(lambda kk, pos: kk >= thr)
    need = n_sel - count(lambda kk, pos: kk > thr)
    cut_ref[...] = jnp.full((1, QB), INT_MAX, I32)

    @pl.when(jnp.max(cnt_ge) > n_sel)
    def _():
        def search_pos(it, p):
            cand = p + lax.shift_left(jnp.int32(1), idx_bits - 1 - it)
            c = count(lambda kk, pos: jnp.where(kk == thr, pos, INT_MAX) < cand)
            return jnp.where(c < need, cand, p)
        cut_ref[...] = lax.fori_loop(0, idx_bits, search_pos, jnp.zeros((1, QB), I32))

    few = thr <= KEY_NEG_INF
    thr_eff = jnp.where(few, KEY_NEG_INF + 1, thr)
    cut = jnp.where(few, INT_MAX, cut_ref[...])

    m_ref[...] = jnp.full(m_ref.shape, M_INIT, F32)
    l_ref[...] = jnp.zeros(l_ref.shape, F32)
    acc_ref[...] = jnp.zeros(acc_ref.shape, F32)

    def attn_tile(kt, carry):
        ks = pl.multiple_of(kt * KT, KT)
        kk = key_ref[pl.ds(ks, KT), :]
        sel = (kk - jnp.where(ks + row_iota > cut, 1, 0)) >= thr_eff
        m_old, l_old = m_ref[...], l_ref[...]
        m_parts, l_parts = [], []
        for g in range(N_KV_HEADS):
            k_t = k_ref[0, pl.ds(ks, KT), g * LANES:(g + 1) * LANES]
            q_g = q_ref[0, g * heads_per_kv:(g + 1) * heads_per_kv].reshape(heads_per_kv * QB, LANES)
            lg_ref[:, g * heads_per_kv * QB:(g + 1) * heads_per_kv * QB] = lax.dot_general(
                k_t, q_g, _CONTRACT_LAST, preferred_element_type=F32)
        for g in range(N_KV_HEADS):
            v_t = vt_ref[0, g * LANES:(g + 1) * LANES, pl.ds(ks, KT)]
            for hh in range(heads_per_kv):
                cs = slice((g * heads_per_kv + hh) * QB, (g * heads_per_kv + hh + 1) * QB)
                raw = jnp.where(sel, lg_ref[:, cs], NEG_BIG)
                m_new = jnp.maximum(m_old[:, cs], jnp.max(raw, axis=0, keepdims=True))
                alpha = jnp.exp2((m_old[:, cs] - m_new) * c_exp)
                p = jnp.exp2((raw - m_new) * c_exp)
                m_parts.append(m_new)
                l_parts.append(alpha * l_old[:, cs] + jnp.sum(p, axis=0, keepdims=True))
                acc_ref[:, cs] = alpha * acc_ref[:, cs] + jnp.dot(v_t, p.astype(BF16),
                                                                   preferred_element_type=F32)
        m_ref[...] = jnp.concatenate(m_parts, axis=1)
        l_ref[...] = jnp.concatenate(l_parts, axis=1)
        return carry

    lax.fori_loop(0, n_kt, attn_tile, 0)
    for h in range(n_heads):
        cs = slice(h * QB, (h + 1) * QB)
        o_ref[:, cs] = (acc_ref[:, cs] / l_ref[:, cs]).T


def _attn_prompt(qi_hm, w_hm, ki_bf, q_hm, k_bf, vt_bf, n_seq, seq, n_sel):
    n_heads = q_hm.shape[1]
    head_dim = q_hm.shape[3]
    nblk = seq // QB
    kv_w = k_bf.shape[-1]
    blk = lambda b, i: (b * nblk + i, 0, 0, 0)
    return pl.pallas_call(
        functools.partial(_attn_prompt_kernel, n_sel=n_sel, c_exp=head_dim ** -0.5 * LOG2_E, n_heads=n_heads,
                          idx_bits=int(seq).bit_length()),
        grid=(n_seq, nblk),
        in_specs=[pl.BlockSpec((1, N_IDX_HEADS, QB, IDX_DIM), blk),
                  pl.BlockSpec((1, N_IDX_HEADS, QB), lambda b, i: (b * nblk + i, 0, 0)),
                  pl.BlockSpec((1, seq, IDX_DIM), lambda b, i: (b, 0, 0)),
                  pl.BlockSpec((1, n_heads, QB, head_dim), blk),
                  pl.BlockSpec((1, seq, kv_w), lambda b, i: (b, 0, 0)),
                  pl.BlockSpec((1, kv_w, seq), lambda b, i: (b, 0, 0))],
        out_specs=pl.BlockSpec((QB, n_heads * head_dim), lambda b, i: (b * nblk + i, 0)),
        out_shape=jax.ShapeDtypeStruct((n_seq * seq, n_heads * head_dim), F32),
        scratch_shapes=[pltpu.VMEM((seq, QB), I32),
                        pltpu.VMEM((1, n_heads * QB), F32),
                        pltpu.VMEM((1, n_heads * QB), F32),
                        pltpu.VMEM((head_dim, n_heads * QB), F32),
                        pltpu.VMEM((1, QB), I32),
                        pltpu.VMEM((KT, n_heads * QB), F32)],
        compiler_params=_cparams("arbitrary", "arbitrary"),
        name="attn_prompt",
    )(qi_hm, w_hm, ki_bf.reshape(n_seq, seq, IDX_DIM), q_hm, k_bf.reshape(n_seq, seq, kv_w), vt_bf)


SCORE_PAGES = 32
ATTN_PAGES = 16


def _sample_score_kernel(pt_ref, qi_ref, w_ref, kin_ref, *refs, t_new):
    pages, (sp_ref, sn_ref) = refs[:SCORE_PAGES], refs[SCORE_PAGES:]
    page = pages[0].shape[2]
    qi = qi_ref[0]
    rows = qi.shape[0]
    wb = jnp.broadcast_to(w_ref[0] * IDX_SCALE, (rows, page))

    def score(keys_t_bf):
        s = jnp.dot(qi, keys_t_bf, preferred_element_type=F32)
        r = jnp.maximum(s, 0.0) * wb
        return jnp.sum(r.reshape(N_IDX_HEADS, t_new, page), axis=0)

    for j in range(SCORE_PAGES):
        sp_ref[0, :, j * page:(j + 1) * page] = score(pages[j][0].astype(BF16))

    @pl.when(pl.program_id(1) == 0)
    def _():
        sn = score(kin_ref[0])
        s_idx = lax.broadcasted_iota(I32, (t_new, page), 1)
        t_idx = lax.broadcasted_iota(I32, (t_new, page), 0)
        sn_ref[0] = jnp.where(s_idx <= t_idx, sn, -jnp.inf)


def _sample_scores(page_table, qi_s, w_s, ki_new_t, cache_kidx_t, t_new):
    n_seq, n_pages = page_table.shape
    page = cache_kidx_t.shape[2]
    rows = qi_s.shape[1]
    page_specs = [pl.BlockSpec((1, IDX_DIM, page),
                               functools.partial(lambda b, c, pt, j: (pt[b, c * SCORE_PAGES + j], 0, 0), j=j))
                  for j in range(SCORE_PAGES)]
    grid_spec = pltpu.PrefetchScalarGridSpec(
        num_scalar_prefetch=1,
        grid=(n_seq, n_pages // SCORE_PAGES),
        in_specs=[pl.BlockSpec((1, rows, IDX_DIM), lambda b, c, pt: (b, 0, 0)),
                  pl.BlockSpec((1, rows, 1), lambda b, c, pt: (b, 0, 0)),
                  pl.BlockSpec((1, IDX_DIM, page), lambda b, c, pt: (b, 0, 0))] + page_specs,
        out_specs=[pl.BlockSpec((1, t_new, SCORE_PAGES * page), lambda b, c, pt: (b, 0, c)),
                   pl.BlockSpec((1, t_new, page), lambda b, c, pt: (b, 0, 0))],
    )
    return pl.pallas_call(
        functools.partial(_sample_score_kernel, t_new=t_new),
        grid_spec=grid_spec,
        out_shape=[jax.ShapeDtypeStruct((n_seq, t_new, n_pages * page), F32),
                   jax.ShapeDtypeStruct((n_seq, t_new, page), F32)],
        compiler_params=_cparams("arbitrary", "arbitrary"),
        name="sample_scores",
    )(page_table, qi_s, w_s, ki_new_t, *([cache_kidx_t] * SCORE_PAGES))


def _sample_thresh_kernel(sp_ref, sn_ref, ex_ref, mask_ref, key_ref, cut_ref, *, n_sel, n_past, idx_bits):
    rows = sp_ref.shape[0]
    n_tiles = n_past // LANES + 1
    key_ref[:, 0:n_past] = _order_key(sp_ref[...])
    key_ref[:, n_past:n_past + LANES] = _order_key(sn_ref[...])
    lane = lax.broadcasted_iota(I32, (rows, LANES), 1)

    def count(pred):
        def body(j, cnt):
            c0 = pl.multiple_of(j * LANES, LANES)
            return cnt + jnp.where(pred(key_ref[:, pl.ds(c0, LANES)], c0 + lane), 1, 0)
        cnt = lax.fori_loop(0, n_tiles, body, jnp.zeros((rows, LANES), I32), unroll=8)
        return jnp.broadcast_to(jnp.sum(cnt, axis=1, keepdims=True), (rows, LANES))

    def search(it, thr):
        cand = thr + lax.shift_left(jnp.int32(1), 31 - it)
        c = count(lambda kk, pos: kk >= cand)
        return jnp.where(c >= n_sel, cand, thr)

    thr = lax.fori_loop(0, 32, search, jnp.full((rows, LANES), INT_MIN, I32))
    cnt_ge = count(lambda kk, pos: kk >= thr)
    need = n_sel - count(lambda kk, pos: kk > thr)
    cut_ref[...] = jnp.full((rows, LANES), INT_MAX, I32)

    @pl.when(jnp.max(cnt_ge) > n_sel)
    def _():
        def search_pos(it, p):
            cand = p + lax.shift_left(jnp.int32(1), idx_bits - 1 - it)
            c = count(lambda kk, pos: jnp.where(kk == thr, pos, INT_MAX) < cand)
            return jnp.where(c < need, cand, p)
        cut_ref[...] = lax.fori_loop(0, idx_bits, search_pos, jnp.zeros((rows, LANES), I32))

    few = thr <= KEY_NEG_INF
    thr_eff = jnp.where(few, KEY_NEG_INF + 1, thr)
    cut = jnp.where(few, INT_MAX, cut_ref[...])

    ex = ex_ref[...]
    width = ex.shape[1]

    def emit(j, carry):
        c0 = pl.multiple_of(j * LANES, LANES)
        hit = jnp.where((key_ref[:, pl.ds(c0, LANES)] - jnp.where(c0 + lane > cut, 1, 0)) >= thr_eff, 1.0, 0.0)
        mask_ref[:, pl.ds(pl.multiple_of(j * width, width), width)] = jnp.dot(
            hit.astype(BF16), ex, preferred_element_type=F32)
        return carry

    lax.fori_loop(0, n_tiles, emit, 0, unroll=4)


def _sample_thresh(sp, sn, n_sel, rows_per_step=64):
    m, n_past = sp.shape
    rows_per_step = min(rows_per_step, m)
    tok = jnp.arange(LANES, dtype=I32)[:, None]
    col = jnp.arange(LANES * N_KV_HEADS, dtype=I32)[None, :]
    expand = (col // N_KV_HEADS == tok).astype(BF16)
    width = (n_past + LANES) * N_KV_HEADS
    return pl.pallas_call(
        functools.partial(_sample_thresh_kernel, n_sel=n_sel, n_past=n_past,
                          idx_bits=int(n_past + LANES).bit_length()),
        grid=(m // rows_per_step,),
        in_specs=[pl.BlockSpec((rows_per_step, n_past), lambda r: (r, 0)),
                  pl.BlockSpec((rows_per_step, LANES), lambda r: (r, 0)),
                  pl.BlockSpec(expand.shape, lambda r: (0, 0))],
        out_specs=pl.BlockSpec((rows_per_step, width), lambda r: (r, 0)),
        out_shape=jax.ShapeDtypeStruct((m, width), F32),
        scratch_shapes=[pltpu.VMEM((rows_per_step, n_past + LANES), I32),
                        pltpu.VMEM((rows_per_step, LANES), I32)],
        compiler_params=_cparams("arbitrary"),
        name="sample_thresh",
    )(sp, sn, expand)


def _sample_attn_kernel(pt_ref, q_ref, mask_ref, maskn_ref, kn_ref, vn_ref, *refs, c_exp, group_rows):
    k_pages = refs[:ATTN_PAGES]
    v_pages = refs[ATTN_PAGES:2 * ATTN_PAGES]
    o_ref, m_ref, l_ref, acc_ref = refs[2 * ATTN_PAGES:]
    c = pl.program_id(1)
    q = q_ref[0]
    rows = q.shape[0]
    width = maskn_ref.shape[2]
    reps = rows // mask_ref.shape[1]
    own_head = jnp.where(lax.broadcasted_iota(I32, (rows, width), 1) % N_KV_HEADS ==
                         lax.broadcasted_iota(I32, (rows, width), 0) // group_rows, 1.0, 0.0)

    @pl.when(c == 0)
    def _():
        m_ref[...] = jnp.full(m_ref.shape, M_INIT, F32)
        l_ref[...] = jnp.zeros(l_ref.shape, F32)
        acc_ref[...] = jnp.zeros(acc_ref.shape, F32)

    def select(flags):
        return jnp.concatenate([flags] * reps, axis=0) * own_head > 0.5

    def attend(tiles):
        raws = [jnp.where(sel, lax.dot_general(q, k, _CONTRACT_LAST, preferred_element_type=F32), NEG_BIG)
                for sel, k, _ in tiles]
        m_old = m_ref[...]
        m_new = jnp.maximum(m_old, jnp.max(functools.reduce(jnp.maximum, raws), axis=1, keepdims=True))
        alpha = jnp.exp2((m_old - m_new) * c_exp)
        ps = [jnp.exp2((raw - m_new) * c_exp) for raw in raws]
        l_ref[...] = alpha * l_ref[...] + jnp.sum(functools.reduce(jnp.add, ps), axis=1, keepdims=True)
        m_ref[...] = m_new
        pv = functools.reduce(jnp.add, [jnp.dot(p.astype(BF16), v, preferred_element_type=F32)
                                        for p, (_, _, v) in zip(ps, tiles)])
        acc_ref[...] = alpha * acc_ref[...] + pv

    attend([(select(mask_ref[0, :, j * width:(j + 1) * width]),
             k_pages[j][0].astype(BF16), v_pages[j][0].astype(BF16)) for j in range(ATTN_PAGES)])

    @pl.when(c == pl.num_programs(1) - 1)
    def _():
        attend([(select(maskn_ref[0]), kn_ref[0], vn_ref[0])])
        o_ref[0] = acc_ref[...] / l_ref[...]


def _sample_attn(page_table, q_s, mask, k_new, v_new, cache_k, cache_v, t_new):
    n_seq, n_pages = page_table.shape
    page_rows, head_dim = cache_k.shape[1], cache_k.shape[2]
    rows = q_s.shape[1]
    page_specs = [pl.BlockSpec((1, page_rows, head_dim),
                               functools.partial(lambda b, c, pt, j: (pt[b, c * ATTN_PAGES + j], 0, 0), j=j))
                  for j in range(ATTN_PAGES)]
    per_seq = lambda shape: pl.BlockSpec((1,) + shape, lambda b, c, pt: (b, 0, 0))
    grid_spec = pltpu.PrefetchScalarGridSpec(
        num_scalar_prefetch=1,
        grid=(n_seq, n_pages // ATTN_PAGES),
        in_specs=[per_seq((rows, head_dim)),
                  pl.BlockSpec((1, t_new, ATTN_PAGES * page_rows), lambda b, c, pt: (b, 0, c)),
                  pl.BlockSpec((1, t_new, page_rows), lambda b, c, pt: (b, 0, n_pages)),
                  per_seq((page_rows, head_dim)), per_seq((page_rows, head_dim))] + page_specs + page_specs,
        out_specs=per_seq((rows, head_dim)),
        scratch_shapes=[pltpu.VMEM((rows, 1), F32),
                        pltpu.VMEM((rows, 1), F32),
                        pltpu.VMEM((rows, head_dim), F32)],
    )
    return pl.pallas_call(
        functools.partial(_sample_attn_kernel, c_exp=head_dim ** -0.5 * LOG2_E, group_rows=rows // N_KV_HEADS),
        grid_spec=grid_spec,
        out_shape=jax.ShapeDtypeStruct((n_seq, rows, head_dim), F32),
        compiler_params=_cparams("arbitrary", "arbitrary"),
        name="sample_attn",
    )(page_table, q_s, mask, mask, k_new, v_new, *([cache_k] * ATTN_PAGES), *([cache_v] * ATTN_PAGES))


def _merge_kernel(x_ref, gt_ref, co_ref, ao_ref, go_ref, wc_ref, wa_ref, o_ref, on_ref, *, n_heads):
    @pl.when(pl.program_id(1) == 0)
    def _():
        g = go_ref[...]
        for h in range(n_heads):
            cs = slice(h * LANES, (h + 1) * LANES)
            o = ao_ref[:, cs]
            on_ref[:, cs] = (o * lax.rsqrt(jnp.mean(o * o, axis=-1, keepdims=True) + EPS) * g).astype(BF16)

    y = jnp.dot(co_ref[...].astype(BF16), wc_ref[...].astype(BF16), preferred_element_type=F32)
    y = y + jnp.dot(on_ref[...], wa_ref[...].astype(BF16), preferred_element_type=F32)
    o_ref[...] = x_ref[...] + gt_ref[0] * y


def _merge(x, gt, conv_o, attn_o, g_o, w_out, tm, tn=512):
    m, d = x.shape
    g, r, _ = gt.shape
    d_conv = conv_o.shape[1]
    d_attn = attn_o.shape[1]
    tiles_per_group = (m // tm) // g
    rb = d_conv // d_attn
    return pl.pallas_call(
        functools.partial(_merge_kernel, n_heads=d_attn // LANES),
        grid=(m // tm, d // tn),
        in_specs=[pl.BlockSpec((tm, tn), lambda i, j: (i, j)),
                  pl.BlockSpec((1, r, tn), lambda i, j: (i // tiles_per_group, 0, j)),
                  pl.BlockSpec((tm, d_conv), lambda i, j: (i, 0)),
                  pl.BlockSpec((tm, d_attn), lambda i, j: (i, 0)),
                  pl.BlockSpec((1, LANES), lambda i, j: (0, 0)),
                  pl.BlockSpec((d_conv, tn), lambda i, j: (0, j)),
                  pl.BlockSpec((d_attn, tn), lambda i, j: (rb, j))],
        out_specs=pl.BlockSpec((tm, tn), lambda i, j: (i, j)),
        out_shape=jax.ShapeDtypeStruct((m, d), F32),
        scratch_shapes=[pltpu.VMEM((tm, d_attn), BF16)],
        compiler_params=_cparams("arbitrary", "arbitrary"),
        name="merge",
    )(x, gt, conv_o, attn_o, g_o.reshape(1, -1), w_out, w_out)


def _ffn_act(cur_g, p1_g, p2_g, cur_v, p1_v, p2_v, wg_ref, wv_ref, bg_ref, bv_ref):
    gate = p2_g * wg_ref[0:1, :] + p1_g * wg_ref[1:2, :] + cur_g * wg_ref[2:3, :] + bg_ref[...]
    val = p2_v * wv_ref[0:1, :] + p1_v * wv_ref[1:2, :] + cur_v * wv_ref[2:3, :] + bv_ref[...]
    return (_silu(gate) * val).astype(BF16)


def _ffn_finish(f, act, wd_ref, x_ref, gt_ref, o_ref, acc_ref):
    @pl.when(f == 0)
    def _():
        acc_ref[...] = jnp.zeros(acc_ref.shape, F32)

    acc_ref[...] += jnp.dot(act, wd_ref[...].astype(BF16), preferred_element_type=F32)

    @pl.when(f == pl.num_programs(1) - 1)
    def _():
        o_ref[...] = x_ref[...] + gt_ref[0] * acc_ref[...]


def _to_bf16_kernel(x_ref, o_ref):
    o_ref[...] = x_ref[...].astype(BF16)


def _to_bf16(w, rows_per_step=256):
    r, c = w.shape
    return pl.pallas_call(
        _to_bf16_kernel,
        grid=(r // rows_per_step,),
        in_specs=[pl.BlockSpec((rows_per_step, c), lambda i: (i, 0))],
        out_specs=pl.BlockSpec((rows_per_step, c), lambda i: (i, 0)),
        out_shape=jax.ShapeDtypeStruct((r, c), BF16),
        compiler_params=_cparams("arbitrary"),
        name="to_bf16",
    )(w)


FFN_HALO = 16


def _ffn_prompt_kernel(x_ref, xh_ref, g_ref, sc_ref, sh_ref, wug_ref, wuv_ref, wg_ref, wv_ref, bg_ref, bv_ref,
                       wd_ref, gt_ref, o_ref, tg_ref, tv_ref, h_ref, hist_g, hist_v, act_ref, *, tm, tiles_per_seq):
    i, f = pl.program_id(0), pl.program_id(1)

    def normmod(x):
        y = x * lax.rsqrt(jnp.mean(x * x, axis=-1, keepdims=True) + EPS) * g_ref[...]
        return (y * (1.0 + sc_ref[0]) + sh_ref[0]).astype(BF16)

    @pl.when(f == 0)
    def _():
        h_ref[0:FFN_HALO, :] = normmod(xh_ref[...])
        h_ref[FFN_HALO:FFN_HALO + tm, :] = normmod(x_ref[...])
        o_ref[...] = jnp.zeros(o_ref.shape, F32)

    for w_ref, hist in ((wug_ref, hist_g), (wuv_ref, hist_v)):
        hist[...] = jnp.dot(h_ref[...], w_ref[...], preferred_element_type=F32)

        @pl.when((i % tiles_per_seq) == 0)
        def _():
            hist[0:FFN_HALO, :] = jnp.zeros((FFN_HALO, hist.shape[1]), F32)

    chunk = 64
    for r0 in range(0, tm, chunk):
        rows = lambda hist, k: hist[FFN_HALO + r0 - k:FFN_HALO + r0 - k + chunk, :]
        act_ref[r0:r0 + chunk, :] = _ffn_act(rows(hist_g, 0), rows(hist_g, 1), rows(hist_g, 2),
                                             rows(hist_v, 0), rows(hist_v, 1), rows(hist_v, 2),
                                             wg_ref, wv_ref, bg_ref, bv_ref)
    tg_ref[0] = hist_g[tm + FFN_HALO - SUBLANES:tm + FFN_HALO, :]
    tv_ref[0] = hist_v[tm + FFN_HALO - SUBLANES:tm + FFN_HALO, :]
    o_ref[...] += jnp.dot(act_ref[...], wd_ref[...], preferred_element_type=F32)

    @pl.when(f == pl.num_programs(1) - 1)
    def _():
        o_ref[...] = x_ref[...] + gt_ref[0] * o_ref[...]


def _ffn_prompt(x1, gain, sc, sh, gt, w_up_bf, w_dw, b_dw, w_down_bf, seq, tm, tf=512):
    m, d = x1.shape
    d_ff = w_down_bf.shape[0]
    nf = d_ff // tf
    tiles_per_seq = seq // tm
    fw = w_dw.shape[0]
    halo_idx = lambda i: jnp.maximum(i * (tm // FFN_HALO) - 1, 0)
    mod = pl.BlockSpec((1, 1, d), lambda i, f: (i // tiles_per_seq, 0, 0))
    tail = pl.BlockSpec((1, SUBLANES, tf), lambda i, f: (i, 0, f))
    return pl.pallas_call(
        functools.partial(_ffn_prompt_kernel, tm=tm, tiles_per_seq=tiles_per_seq),
        grid=(m // tm, nf),
        in_specs=[pl.BlockSpec((tm, d), lambda i, f: (i, 0)),
                  pl.BlockSpec((FFN_HALO, d), lambda i, f: (halo_idx(i), 0)),
                  pl.BlockSpec((1, d), lambda i, f: (0, 0)),
                  mod, mod,
                  pl.BlockSpec((d, tf), lambda i, f: (0, f)),
                  pl.BlockSpec((d, tf), lambda i, f: (0, f + nf)),
                  pl.BlockSpec((fw, tf), lambda i, f: (0, f)),
                  pl.BlockSpec((fw, tf), lambda i, f: (0, f + nf)),
                  pl.BlockSpec((1, tf), lambda i, f: (0, f)),
                  pl.BlockSpec((1, tf), lambda i, f: (0, f + nf)),
                  pl.BlockSpec((tf, d), lambda i, f: (f, 0)),
                  mod],
        out_specs=[pl.BlockSpec((tm, d), lambda i, f: (i, 0)), tail, tail],
        out_shape=[jax.ShapeDtypeStruct((m, d), F32),
                   jax.ShapeDtypeStruct((m // tm, SUBLANES, d_ff), F32),
                   jax.ShapeDtypeStruct((m // tm, SUBLANES, d_ff), F32)],
        scratch_shapes=[pltpu.VMEM((FFN_HALO + tm, d), BF16),
                        pltpu.VMEM((FFN_HALO + tm, tf), F32),
                        pltpu.VMEM((FFN_HALO + tm, tf), F32),
                        pltpu.VMEM((tm, tf), BF16)],
        compiler_params=_cparams("arbitrary", "arbitrary"),
        name="ffn_prompt",
    )(x1, x1, gain.reshape(1, d), sc, sh, w_up_bf, w_up_bf, w_dw, w_dw, b_dw.reshape(1, -1), b_dw.reshape(1, -1),
      w_down_bf, gt)


def _ffn_sample_kernel(cg_ref, cv_ref, p1g_ref, p1v_ref, p2g_ref, p2v_ref, wg_ref, wv_ref, bg_ref, bv_ref,
                       wd_ref, x_ref, gt_ref, o_ref, acc_ref):
    act = _ffn_act(cg_ref[...], p1g_ref[...], p2g_ref[...], cv_ref[...], p1v_ref[...], p2v_ref[...],
                   wg_ref, wv_ref, bg_ref, bv_ref)
    _ffn_finish(pl.program_id(1), act, wd_ref, x_ref, gt_ref, o_ref, acc_ref)


def _ffn_sample(cur, prev1, prev2, x1, gt, w_dw, b_dw, w_down, tf=512):
    m, d = x1.shape
    d_ff = w_down.shape[0]
    nf = d_ff // tf
    fw = w_dw.shape[0]
    lo = pl.BlockSpec((m, tf), lambda i, f: (0, f))
    hi = pl.BlockSpec((m, tf), lambda i, f: (0, f + nf))
    return pl.pallas_call(
        _ffn_sample_kernel,
        grid=(1, nf),
        in_specs=[lo, hi, lo, hi, lo, hi,
                  pl.BlockSpec((fw, tf), lambda i, f: (0, f)),
                  pl.BlockSpec((fw, tf), lambda i, f: (0, f + nf)),
                  pl.BlockSpec((1, tf), lambda i, f: (0, f)),
                  pl.BlockSpec((1, tf), lambda i, f: (0, f + nf)),
                  pl.BlockSpec((tf, d), lambda i, f: (f, 0)),
                  pl.BlockSpec((m, d), lambda i, f: (0, 0)),
                  pl.BlockSpec((1, m, d), lambda i, f: (0, 0, 0))],
        out_specs=pl.BlockSpec((m, d), lambda i, f: (0, 0)),
        out_shape=jax.ShapeDtypeStruct((m, d), F32),
        scratch_shapes=[pltpu.VMEM((m, d), F32)],
        compiler_params=_cparams("arbitrary", "arbitrary"),
        name="ffn_sample",
    )(cur, cur, prev1, prev1, prev2, prev2, w_dw, w_dw, b_dw.reshape(1, -1), b_dw.reshape(1, -1),
      w_down, x1, gt)


def _rope_tables(pos):
    def tab(dim):
        inv = jnp.power(ROPE_THETA, -jnp.arange(0, dim, 2, dtype=F32) / dim)
        ang = pos.astype(F32)[:, None] * inv[None, :]
        cos, sin = jnp.cos(ang), jnp.sin(ang)
        reps = LANES // dim
        return (jnp.tile(jnp.concatenate([cos, cos], axis=-1), (1, reps)),
                jnp.tile(jnp.concatenate([-sin, sin], axis=-1), (1, reps)))
    return tab(LANES) + tab(IDX_DIM)


def _in_proj(x2d, norm1, sc, sh, w_in_t, n_main, tm):
    z = _normmod_matmul(x2d, norm1, sc, sh, w_in_t, n_main, tm, 512, "in_proj", w_transposed=True)
    w_tail = jnp.pad(w_in_t[n_main:], ((0, LANES - (w_in_t.shape[0] - n_main)), (0, 0)))
    zt = _normmod_matmul(x2d, norm1, sc, sh, w_tail, LANES, tm, LANES, "in_proj_tail", w_transposed=True)
    return z, zt


def kernel(x_prompt, x_sample, cache_k, cache_v, cache_kidx, state_conv, state_ffn, page_table, c_prompt, c_sample, norm1, w_ada, b_ada, w_in, g_q, g_k, w_dw_a, b_dw_a, gn_g, gn_b, g_o, w_out, norm2, w_up, w_dw_f, b_dw_f, w_down):
    n_b, seq, d = x_prompt.shape
    n_s, t_new, _ = x_sample.shape
    depth = norm1.shape[0]
    assert depth == 1
    head_dim = g_q.shape[-1]
    assert head_dim == LANES
    d_conv = w_dw_a.shape[-1]
    d_attn = w_out.shape[1] - d_conv
    n_heads = d_attn // head_dim
    kv_w = N_KV_HEADS * head_dim
    n_main = 2 * d_conv + d_attn + 2 * kv_w + N_IDX_HEADS * IDX_DIM
    n_pool, page = cache_k.shape[1], cache_k.shape[2]
    n_pages = page_table.shape[1]
    n_past = n_pages * page
    mp, ms = n_b * seq, n_s * t_new
    heads_per_kv = n_heads // N_KV_HEADS

    n_c = n_b + n_s
    pad_c = (-n_c) % SUBLANES
    c_all = jnp.concatenate([c_prompt, c_sample, jnp.zeros((pad_c, d), F32)], axis=0)
    mods = _ada(c_all, w_ada[0], b_ada[0])
    mp6 = mods[:n_b].reshape(n_b, 6, 1, d)
    sh1p, sc1p, gt1p, sh2p, sc2p, gt2p = [mp6[:, k] for k in range(6)]
    ms6 = jnp.repeat(mods[n_b:n_c].reshape(n_s, 6, d), t_new, axis=0)
    sh1s, sc1s, gt1s, sh2s, sc2s, gt2s = [ms6[:, k][None] for k in range(6)]

    w_in_t = jnp.swapaxes(w_in[0], 0, 1)

    xp = x_prompt.reshape(mp, d)
    tm_p = 1024 if seq % 1024 == 0 else QB
    z, zt = _in_proj(xp, norm1[0], sc1p, sh1p, w_in_t, n_main, tm_p)
    conv_o, new_conv_p = _conv_prompt(z, n_b, seq, d_conv, w_dw_a[0], b_dw_a[0], gn_g[0], gn_b[0],
                                      tm=256 if seq % 256 == 0 else QB)
    tabs_p = _rope_tables(jnp.arange(seq, dtype=I32))
    q_hm, qi_hm, k_p, k_bf, vt_bf, ki_p, ki_bf = _qk_epilogue(z, zt, tabs_p, g_q[0], g_k[0], n_b, seq, n_heads)
    w_hm = zt[:, IDX_DIM:IDX_DIM + N_IDX_HEADS].reshape(mp // QB, QB, N_IDX_HEADS).transpose(0, 2, 1)
    n_sel_p = min(TOPK_MAX, seq // 4)
    attn_p = _attn_prompt(qi_hm, w_hm, ki_bf, q_hm, k_bf, vt_bf, n_b, seq, n_sel_p)
    x1p = _merge(xp, gt1p, conv_o, attn_p, g_o[0], w_out[0], tm_p)
    w_up_bf, w_down_bf = _to_bf16(w_up[0]), _to_bf16(w_down[0])
    tm_f = 512 if seq % 512 == 0 else QB
    y_p, u_tail_g, u_tail_v = _ffn_prompt(x1p, norm2[0], sc2p, sh2p, gt2p, w_up_bf, w_dw_f[0], b_dw_f[0],
                                          w_down_bf, seq, tm_f)
    v_p = z[:, 2 * d_conv + d_attn + kv_w:2 * d_conv + d_attn + 2 * kv_w]
    fw = w_dw_f.shape[1]
    last_tile = (jnp.arange(n_b) + 1) * (seq // tm_f) - 1
    new_ffn_p = jnp.concatenate([u_tail_g[last_tile], u_tail_v[last_tile]], axis=-1)[:, SUBLANES - (fw - 1):]

    xs = x_sample.reshape(ms, d)
    zs, zts = _in_proj(xs, norm1[0], sc1s, sh1s, w_in_t, n_main, ms)
    conv_os, new_conv_s = _conv_sample(zs, state_conv[0], d_conv, w_dw_a[0], b_dw_a[0], gn_g[0], gn_b[0])
    pos_s = jnp.tile(n_past + jnp.arange(t_new, dtype=I32), ms // t_new)
    tabs_s = _rope_tables(pos_s)
    q_hm_s, qi_hm_s, k_s, k_bf_s, _, ki_s, ki_bf_s = _qk_epilogue(zs, zts, tabs_s, g_q[0], g_k[0], 1, ms, n_heads)
    v_s = zs[:, 2 * d_conv + d_attn + kv_w:2 * d_conv + d_attn + 2 * kv_w]
    seq_rows = lambda a, nh: (a.reshape(ms // QB, nh, QB // t_new, t_new, a.shape[-1])
                              .transpose(0, 2, 1, 3, 4).reshape(n_s, nh * t_new, a.shape[-1]))
    qi_s = seq_rows(qi_hm_s, N_IDX_HEADS)
    q_s = seq_rows(q_hm_s, n_heads)
    w_s = (zts[:, IDX_DIM:IDX_DIM + N_IDX_HEADS].reshape(n_s, t_new, N_IDX_HEADS)
           .transpose(0, 2, 1).reshape(n_s, N_IDX_HEADS * t_new, 1))
    ki_new_t = jnp.pad(ki_bf_s.reshape(n_s, t_new, IDX_DIM).transpose(0, 2, 1), ((0, 0), (0, 0), (0, page - t_new)))
    kv_new = lambda a: jnp.pad(a.reshape(n_s, t_new * N_KV_HEADS, head_dim),
                               ((0, 0), (0, (page - t_new) * N_KV_HEADS), (0, 0)))
    kidx_t = jnp.swapaxes(cache_kidx[0], 1, 2)
    sp, sn = _sample_scores(page_table, qi_s, w_s, ki_new_t, kidx_t, t_new)
    n_sel_s = min(TOPK_MAX, (n_past + t_new) // 4)
    mask = _sample_thresh(sp.reshape(ms, n_past), sn.reshape(ms, page), n_sel_s)
    o_s = _sample_attn(page_table, q_s, mask.reshape(n_s, t_new, -1), kv_new(k_bf_s), kv_new(v_s.astype(BF16)),
                       cache_k[0].reshape(n_pool, page * N_KV_HEADS, head_dim),
                       cache_v[0].reshape(n_pool, page * N_KV_HEADS, head_dim), t_new)
    attn_s = o_s.reshape(n_s, n_heads, t_new, head_dim).transpose(0, 2, 1, 3).reshape(ms, d_attn)
    x1s = _merge(xs, gt1s, conv_os, attn_s, g_o[0], w_out[0], ms)
    u_s = _normmod_matmul(x1s, norm2[0], sc2s, sh2s, w_up_bf, w_up.shape[-1], ms, 512, "up_proj")
    u_hist = jnp.concatenate([state_ffn[0], u_s.reshape(n_s, t_new, -1)], axis=1)
    shifted = lambda k: u_hist[:, k:k + t_new].reshape(ms, -1)
    y_s = _ffn_sample(shifted(2), shifted(1), shifted(0), x1s, gt2s, w_dw_f[0], b_dw_f[0], w_down_bf)
    new_ffn_s = u_hist[:, t_new:]

    return (y_p.reshape(n_b, seq, d), y_s.reshape(n_s, t_new, d),
            k_p.reshape(1, n_b, seq, N_KV_HEADS, head_dim), v_p.reshape(1, n_b, seq, N_KV_HEADS, head_dim),
            ki_p.reshape(1, n_b, seq, IDX_DIM), new_conv_p[None], new_ffn_p[None],
            k_s.reshape(1, n_s, t_new, N_KV_HEADS, head_dim), v_s.reshape(1, n_s, t_new, N_KV_HEADS, head_dim),
            ki_s.reshape(1, n_s, t_new, IDX_DIM), new_conv_s[None], new_ffn_s[None])
```

```python
import functools

import jax
import jax.numpy as jnp
from jax import lax
from jax.experimental import pallas as pl
from jax.experimental.pallas import tpu as pltpu

F32 = jnp.float32
BF16 = jnp.bfloat16
I32 = jnp.int32

CONV_GROUPS = 8
N_KV_HEADS = 2
N_IDX_HEADS = 16
IDX_DIM = 64
TOPK_MAX = 256
ROPE_THETA = 10000.0
EPS = 1e-6
IDX_SCALE = (IDX_DIM ** -0.5) * (N_IDX_HEADS ** -0.5)

LANES = 128
SUBLANES = 8
QB = 128
VMEM_LIMIT_BYTES = 56 * 1024 * 1024
NEG_BIG = -1e30
M_INIT = -1e29
LOG2_E = 1.4426950408889634
INT_MIN = -2 ** 31
INT_MAX = 2 ** 31 - 1
KEY_NEG_INF = -2139095041


def _cparams(*sem):
    return pltpu.CompilerParams(dimension_semantics=sem, vmem_limit_bytes=VMEM_LIMIT_BYTES)


def _silu(x):
    return x * jax.nn.sigmoid(x)


def _order_key(x):
    bits = pltpu.bitcast(x, I32)
    return bits ^ ((bits >> 31) & INT_MAX)


def _ada_kernel(c_ref, w_ref, b_ref, o_ref):
    s = _silu(c_ref[...]).astype(BF16)
    o_ref[...] = jnp.dot(s, w_ref[...].astype(BF16), preferred_element_type=F32) + b_ref[...]


def _ada(c_all, w_ada, b_ada, tn=1024):
    r, d = c_all.shape
    n = w_ada.shape[1]
    return pl.pallas_call(
        _ada_kernel,
        grid=(n // tn,),
        in_specs=[pl.BlockSpec((r, d), lambda j: (0, 0)),
                  pl.BlockSpec((d, tn), lambda j: (0, j)),
                  pl.BlockSpec((1, tn), lambda j: (0, j))],
        out_specs=pl.BlockSpec((r, tn), lambda j: (0, j)),
        out_shape=jax.ShapeDtypeStruct((r, n), F32),
        compiler_params=_cparams("arbitrary"),
        name="ada",
    )(c_all, w_ada, b_ada.reshape(1, n))


_CONTRACT_LAST = (((1,), (1,)), ((), ()))


def _normmod_matmul_kernel(x_ref, g_ref, sc_ref, sh_ref, w_ref, o_ref, h_ref, *, w_transposed):
    @pl.when(pl.program_id(1) == 0)
    def _():
        x = x_ref[...]
        y = x * lax.rsqrt(jnp.mean(x * x, axis=-1, keepdims=True) + EPS) * g_ref[...]
        h_ref[...] = (y * (1.0 + sc_ref[0]) + sh_ref[0]).astype(BF16)

    w = w_ref[...].astype(BF16)
    if w_transposed:
        o_ref[...] = lax.dot_general(h_ref[...], w, _CONTRACT_LAST, preferred_element_type=F32)
    else:
        o_ref[...] = jnp.dot(h_ref[...], w, preferred_element_type=F32)


def _normmod_matmul(x, gain, sc, sh, w, ncols, tm, tn, name, w_transposed=False):
    m, d = x.shape
    g, r, _ = sc.shape
    tiles_per_group = (m // tm) // g
    mod_spec = pl.BlockSpec((1, r, d), lambda i, j: (i // tiles_per_group, 0, 0))
    w_spec = pl.BlockSpec((tn, d), lambda i, j: (j, 0)) if w_transposed else pl.BlockSpec((d, tn), lambda i, j: (0, j))
    return pl.pallas_call(
        functools.partial(_normmod_matmul_kernel, w_transposed=w_transposed),
        grid=(m // tm, ncols // tn),
        in_specs=[pl.BlockSpec((tm, d), lambda i, j: (i, 0)),
                  pl.BlockSpec((1, d), lambda i, j: (0, 0)),
                  mod_spec, mod_spec,
                  w_spec],
        out_specs=pl.BlockSpec((tm, tn), lambda i, j: (i, j)),
        out_shape=jax.ShapeDtypeStruct((m, ncols), F32),
        scratch_shapes=[pltpu.VMEM((tm, d), BF16)],
        compiler_params=_cparams("arbitrary", "arbitrary"),
        name=name,
    )(x, gain.reshape(1, d), sc, sh, w)


def _conv_gn_silu(hist_ref, off, rows, wdw_ref, bdw_ref, gng_ref, gnb_ref, o_ref, row_chunk):
    conv_w = wdw_ref.shape[0]
    for r0 in range(0, rows, row_chunk):
        for c in range(CONV_GROUPS):
            cs = slice(c * LANES, (c + 1) * LANES)
            acc = jnp.zeros((row_chunk, LANES), F32)
            for r in range(SUBLANES):
                n_rows = row_chunk if r == 0 else row_chunk + SUBLANES
                part = jnp.zeros((n_rows, LANES), F32)
                for w in range((r - off) % SUBLANES, conv_w, SUBLANES):
                    a0 = off + r0 + w - r
                    part = part + hist_ref[a0:a0 + n_rows, cs] * wdw_ref[w:w + 1, cs]
                acc = acc + part[r:r + row_chunk]
            y = acc + bdw_ref[:, cs]
            mu = jnp.mean(y, axis=-1, keepdims=True)
            dlt = y - mu
            var = jnp.mean(dlt * dlt, axis=-1, keepdims=True)
            yn = dlt * lax.rsqrt(var + EPS) * gng_ref[:, cs] + gnb_ref[:, cs]
            o_ref[r0:r0 + row_chunk, cs] = _silu(yn).astype(o_ref.dtype)


HALO = 32


def _conv_prompt_kernel(za_ref, zg_ref, ha_ref, hg_ref, wdw_ref, bdw_ref, gng_ref, gnb_ref,
                        o_ref, newc_ref, hist_ref, *, tm, tiles_per_seq):
    i = pl.program_id(0)
    first = (i % tiles_per_seq) == 0
    a_halo = ha_ref[...] * jax.nn.sigmoid(hg_ref[...])
    hist_ref[0:HALO, :] = jnp.where(first, 0.0, a_halo)
    hist_ref[HALO:HALO + tm, :] = za_ref[...] * jax.nn.sigmoid(zg_ref[...])
    conv_w = wdw_ref.shape[0]
    _conv_gn_silu(hist_ref, HALO - (conv_w - 1), tm, wdw_ref, bdw_ref, gng_ref, gnb_ref, o_ref, 64)

    @pl.when((i % tiles_per_seq) == tiles_per_seq - 1)
    def _():
        newc_ref[0] = hist_ref[HALO + tm - (conv_w - 1):HALO + tm, :]


def _conv_prompt(z, n_seq, seq, d_conv, wdw, bdw, gng, gnb, tm=256):
    m = z.shape[0]
    conv_w = wdw.shape[0]
    tiles_per_seq = seq // tm
    cb = 1
    halo_idx = lambda i: jnp.maximum(i * (tm // HALO) - 1, 0)
    vec = pl.BlockSpec((1, d_conv), lambda i: (0, 0))
    return pl.pallas_call(
        functools.partial(_conv_prompt_kernel, tm=tm, tiles_per_seq=tiles_per_seq),
        grid=(m // tm,),
        in_specs=[pl.BlockSpec((tm, d_conv), lambda i: (i, 0)),
                  pl.BlockSpec((tm, d_conv), lambda i: (i, cb)),
                  pl.BlockSpec((HALO, d_conv), lambda i: (halo_idx(i), 0)),
                  pl.BlockSpec((HALO, d_conv), lambda i: (halo_idx(i), cb)),
                  pl.BlockSpec((conv_w, d_conv), lambda i: (0, 0)),
                  vec, vec, vec],
        out_specs=[pl.BlockSpec((tm, d_conv), lambda i: (i, 0)),
                   pl.BlockSpec((1, conv_w - 1, d_conv), lambda i: (i // tiles_per_seq, 0, 0))],
        out_shape=[jax.ShapeDtypeStruct((m, d_conv), BF16),
                   jax.ShapeDtypeStruct((n_seq, conv_w - 1, d_conv), F32)],
        scratch_shapes=[pltpu.VMEM((HALO + tm, d_conv), F32)],
        compiler_params=_cparams("arbitrary"),
        name="conv_prompt",
    )(z, z, z, z, wdw, bdw.reshape(1, -1), gng.reshape(1, -1), gnb.reshape(1, -1))


def _conv_sample_kernel(za_ref, zg_ref, st_ref, wdw_ref, bdw_ref, gng_ref, gnb_ref,
                        o_ref, newc_ref, hist_ref, *, t_new):
    conv_w = wdw_ref.shape[0]
    hist_ref[0:conv_w - 1, :] = st_ref[0]
    hist_ref[conv_w - 1:conv_w - 1 + t_new, :] = za_ref[...] * jax.nn.sigmoid(zg_ref[...])
    _conv_gn_silu(hist_ref, 0, t_new, wdw_ref, bdw_ref, gng_ref, gnb_ref, o_ref, t_new)
    newc_ref[0] = hist_ref[t_new:t_new + conv_w - 1, :]


def _conv_sample(z, state, d_conv, wdw, bdw, gng, gnb):
    n_seq, hist_rows, _ = state.shape
    conv_w = wdw.shape[0]
    m = z.shape[0]
    t_new = m // n_seq
    vec = pl.BlockSpec((1, d_conv), lambda b: (0, 0))
    return pl.pallas_call(
        functools.partial(_conv_sample_kernel, t_new=t_new),
        grid=(n_seq,),
        in_specs=[pl.BlockSpec((t_new, d_conv), lambda b: (b, 0)),
                  pl.BlockSpec((t_new, d_conv), lambda b: (b, 1)),
                  pl.BlockSpec((1, hist_rows, d_conv), lambda b: (b, 0, 0)),
                  pl.BlockSpec((conv_w, d_conv), lambda b: (0, 0)),
                  vec, vec, vec],
        out_specs=[pl.BlockSpec((t_new, d_conv), lambda b: (b, 0)),
                   pl.BlockSpec((1, hist_rows, d_conv), lambda b: (b, 0, 0))],
        out_shape=[jax.ShapeDtypeStruct((m, d_conv), F32),
                   jax.ShapeDtypeStruct((n_seq, hist_rows, d_conv), F32)],
        scratch_shapes=[pltpu.VMEM((hist_rows + t_new + SUBLANES, d_conv), F32)],
        compiler_params=_cparams("arbitrary"),
        name="conv_sample",
    )(z, z, state, wdw, bdw.reshape(1, -1), gng.reshape(1, -1), gnb.reshape(1, -1))


def _qk_kernel(zq_ref, zkv_ref, zqi0_ref, zqi1_ref, zt_ref, cos_ref, sin_ref, cosi_ref, sini_ref,
               gq_ref, gk_ref, qhm_ref, qihm_ref, k_ref, kbf_ref, vt_ref, ki_ref, kibf_ref, *, n_heads):
    cos, sin = cos_ref[...], sin_ref[...]
    cosi, sini = cosi_ref[...], sini_ref[...]
    tm = cos.shape[0]
    lane = lax.broadcasted_iota(I32, (tm, LANES), 1)
    low_half = (lane % IDX_DIM) < (IDX_DIM // 2)

    def norm_rope(x, g):
        y = x * lax.rsqrt(jnp.mean(x * x, axis=-1, keepdims=True) + EPS) * g
        return y * cos + pltpu.roll(y, LANES // 2, 1) * sin

    def rope_idx(x):
        r = jnp.where(low_half, pltpu.roll(x, LANES - IDX_DIM // 2, 1), pltpu.roll(x, IDX_DIM // 2, 1))
        return x * cosi + r * sini

    gq, gk = gq_ref[...], gk_ref[...]
    for h in range(n_heads):
        qhm_ref[0, h] = norm_rope(zq_ref[:, h * LANES:(h + 1) * LANES], gq).astype(BF16)
    kv_w = N_KV_HEADS * LANES
    for g in range(N_KV_HEADS):
        kg = norm_rope(zkv_ref[:, g * LANES:(g + 1) * LANES], gk)
        k_ref[:, g * LANES:(g + 1) * LANES] = kg
        kbf_ref[:, g * LANES:(g + 1) * LANES] = kg.astype(BF16)
    vt_ref[0] = zkv_ref[:, kv_w:2 * kv_w].T.astype(BF16)
    half = (N_IDX_HEADS * IDX_DIM) // 2
    for j in range(N_IDX_HEADS // 2):
        src = zqi0_ref if j * LANES < half else zqi1_ref
        c0 = (j * LANES) % half
        y = rope_idx(src[:, c0:c0 + LANES])
        qihm_ref[0, 2 * j] = y[:, :IDX_DIM].astype(BF16)
        qihm_ref[0, 2 * j + 1] = y[:, IDX_DIM:].astype(BF16)
    yk = rope_idx(zt_ref[...])[:, :IDX_DIM]
    ki_ref[...] = yk
    kibf_ref[...] = yk.astype(BF16)


def _qk_epilogue(z, zt, tabs, gq, gk, n_seq, seq, n_heads):
    m = z.shape[0]
    tm = QB
    head_dim = LANES
    cos, sin, cosi, sini = tabs
    tab_tiles = cos.shape[0] // tm
    kv_w = N_KV_HEADS * head_dim
    d_attn = n_heads * head_dim
    d_conv = d_attn
    q_cb = (2 * d_conv) // d_attn
    kv_cb = (2 * d_conv + d_attn) // (2 * kv_w)
    qi_w = (N_IDX_HEADS * IDX_DIM) // 2
    qi_cb = (2 * d_conv + d_attn + 2 * kv_w) // qi_w
    tiles_per_seq = seq // tm
    tab = pl.BlockSpec((tm, LANES), lambda i: (i % tab_tiles, 0))
    vec = pl.BlockSpec((1, LANES), lambda i: (0, 0))
    return pl.pallas_call(
        functools.partial(_qk_kernel, n_heads=n_heads),
        grid=(m // tm,),
        in_specs=[pl.BlockSpec((tm, d_attn), lambda i: (i, q_cb)),
                  pl.BlockSpec((tm, 2 * kv_w), lambda i: (i, kv_cb)),
                  pl.BlockSpec((tm, qi_w), lambda i: (i, qi_cb)),
                  pl.BlockSpec((tm, qi_w), lambda i: (i, qi_cb + 1)),
                  pl.BlockSpec((tm, LANES), lambda i: (i, 0)),
                  tab, tab, tab, tab, vec, vec],
        out_specs=[pl.BlockSpec((1, n_heads, tm, head_dim), lambda i: (i, 0, 0, 0)),
                   pl.BlockSpec((1, N_IDX_HEADS, tm, IDX_DIM), lambda i: (i, 0, 0, 0)),
                   pl.BlockSpec((tm, kv_w), lambda i: (i, 0)),
                   pl.BlockSpec((tm, kv_w), lambda i: (i, 0)),
                   pl.BlockSpec((1, kv_w, tm), lambda i: (i // tiles_per_seq, 0, i % tiles_per_seq)),
                   pl.BlockSpec((tm, IDX_DIM), lambda i: (i, 0)),
                   pl.BlockSpec((tm, IDX_DIM), lambda i: (i, 0))],
        out_shape=[jax.ShapeDtypeStruct((m // tm, n_heads, tm, head_dim), BF16),
                   jax.ShapeDtypeStruct((m // tm, N_IDX_HEADS, tm, IDX_DIM), BF16),
                   jax.ShapeDtypeStruct((m, kv_w), F32),
                   jax.ShapeDtypeStruct((m, kv_w), BF16),
                   jax.ShapeDtypeStruct((n_seq, kv_w, seq), BF16),
                   jax.ShapeDtypeStruct((m, IDX_DIM), F32),
                   jax.ShapeDtypeStruct((m, IDX_DIM), BF16)],
        compiler_params=_cparams("arbitrary"),
        name="qk_epilogue",
    )(z, z, z, z, zt, cos, sin, cosi, sini, gq.reshape(1, -1), gk.reshape(1, -1))


KT = 2 * QB


def _attn_prompt_kernel(qi_ref, w_ref, ki_ref, q_ref, k_ref, vt_ref, o_ref,
                        key_ref, m_ref, l_ref, acc_ref, cut_ref, lg_ref, s_ref, *, n_sel, c_exp, n_heads, idx_bits):
    i = pl.program_id(1)
    n_kt = lax.div(i * QB + QB + KT - 1, KT)
    row_iota = lax.broadcasted_iota(I32, (KT, QB), 0)
    w_all = w_ref[0] * IDX_SCALE
    heads_per_kv = n_heads // N_KV_HEADS
    hc = 4

    def score_tile(kt, carry):
        ki_t = ki_ref[0, pl.ds(pl.multiple_of(kt * KT, KT), KT), :]
        for h0 in range(0, N_IDX_HEADS, hc):
            s_ref[:, h0 * QB:(h0 + hc) * QB] = lax.dot_general(
                ki_t, qi_ref[0, h0:h0 + hc].reshape(hc * QB, IDX_DIM), _CONTRACT_LAST,
                preferred_element_type=F32)
        for half in range(KT // QB):
            rs = slice(half * QB, (half + 1) * QB)
            ks = pl.multiple_of(kt * KT + half * QB, QB)
            acc = jnp.zeros((QB, QB), F32)
            for h in range(N_IDX_HEADS):
                acc = acc + jnp.maximum(s_ref[rs, h * QB:(h + 1) * QB], 0.0) * w_all[h:h + 1, :]
            acc = jnp.where(ks + lax.broadcasted_iota(I32, (QB, QB), 0) <=
                            i * QB + lax.broadcasted_iota(I32, (QB, QB), 1), acc, -jnp.inf)
            key_ref[pl.ds(ks, QB), :] = _order_key(acc)
        return carry

    lax.fori_loop(0, n_kt, score_tile, 0)

    def count(pred):
        def body(kt, cnt):
            ks = pl.multiple_of(kt * KT, KT)
            hit = jnp.where(pred(key_ref[pl.ds(ks, KT), :], ks + row_iota), 1, 0)
            return cnt + jnp.sum(hit.reshape(KT // SUBLANES, SUBLANES, QB), axis=0)
        cnt = lax.fori_loop(0, n_kt, body, jnp.zeros((SUBLANES, QB), I32))
        return jnp.sum(cnt, axis=0, keepdims=True)

    def search(it, thr):
        cand = thr + lax.shift_left(jnp.int32(1), 31 - it)
        c = count(lambda kk, pos: kk >= cand)
        return jnp.where(c >= n_sel, cand, thr)

    thr = lax.fori_loop(0, 32, search, jnp.full((1, QB), INT_MIN, I32))
    cnt_ge = count(lambda kk, pos: kk >= thr)
    need = n_sel - count(lambda kk, pos: kk > thr)
    cut_ref[...] = jnp.full((1, QB), INT_MAX, I32)

    @pl.when(jnp.max(cnt_ge) > n_sel)
    def _():
        def search_pos(it, p):
            cand = p + lax.shift_left(jnp.int32(1), idx_bits - 1 - it)
            c = count(lambda kk, pos: jnp.where(kk == thr, pos, INT_MAX) < cand)
            return jnp.where(c < need, cand, p)
        cut_ref[...] = lax.fori_loop(0, idx_bits, search_pos, jnp.zeros((1, QB), I32))

    few = thr <= KEY_NEG_INF
    thr_eff = jnp.where(few, KEY_NEG_INF + 1, thr)
    cut = jnp.where(few, INT_MAX, cut_ref[...])

    m_ref[...] = jnp.full(m_ref.shape, M_INIT, F32)
    l_ref[...] = jnp.zeros(l_ref.shape, F32)
    acc_ref[...] = jnp.zeros(acc_ref.shape, F32)

    def attn_tile(kt, carry):
        ks = pl.multiple_of(kt * KT, KT)
        kk = key_ref[pl.ds(ks, KT), :]
        sel = (kk - jnp.where(ks + row_iota > cut, 1, 0)) >= thr_eff
        m_old, l_old = m_ref[...], l_ref[...]
        m_parts, l_parts = [], []
        for g in range(N_KV_HEADS):
            k_t = k_ref[0, pl.ds(ks, KT), g * LANES:(g + 1) * LANES]
            q_g = q_ref[0, g * heads_per_kv:(g + 1) * heads_per_kv].reshape(heads_per_kv * QB, LANES)
            lg_ref[:, g * heads_per_kv * QB:(g + 1) * heads_per_kv * QB] = lax.dot_general(
                k_t, q_g, _CONTRACT_LAST, preferred_element_type=F32)
        for g in range(N_KV_HEADS):
            v_t = vt_ref[0, g * LANES:(g + 1) * LANES, pl.ds(ks, KT)]
            for hh in range(heads_per_kv):
                cs = slice((g * heads_per_kv + hh) * QB, (g * heads_per_kv + hh + 1) * QB)
                raw = jnp.where(sel, lg_ref[:, cs], NEG_BIG)
                m_new = jnp.maximum(m_old[:, cs], jnp.max(raw, axis=0, keepdims=True))
                alpha = jnp.exp2((m_old[:, cs] - m_new) * c_exp)
                p = jnp.exp2((raw - m_new) * c_exp)
                m_parts.append(m_new)
                l_parts.append(alpha * l_old[:, cs] + jnp.sum(p, axis=0, keepdims=True))
                acc_ref[:, cs] = alpha * acc_ref[:, cs] + jnp.dot(v_t, p.astype(BF16),
                                                                   preferred_element_type=F32)
        m_ref[...] = jnp.concatenate(m_parts, axis=1)
        l_ref[...] = jnp.concatenate(l_parts, axis=1)
        return carry

    lax.fori_loop(0, n_kt, attn_tile, 0)
    for h in range(n_heads):
        cs = slice(h * QB, (h + 1) * QB)
        o_ref[:, cs] = (acc_ref[:, cs] / l_ref[:, cs]).T


def _attn_prompt(qi_hm, w_hm, ki_bf, q_hm, k_bf, vt_bf, n_seq, seq, n_sel):
    n_heads = q_hm.shape[1]
    head_dim = q_hm.shape[3]
    nblk = seq // QB
    kv_w = k_bf.shape[-1]
    blk = lambda b, i: (b * nblk + i, 0, 0, 0)
    return pl.pallas_call(
        functools.partial(_attn_prompt_kernel, n_sel=n_sel, c_exp=head_dim ** -0.5 * LOG2_E, n_heads=n_heads,
                          idx_bits=int(seq).bit_length()),
        grid=(n_seq, nblk),
        in_specs=[pl.BlockSpec((1, N_IDX_HEADS, QB, IDX_DIM), blk),
                  pl.BlockSpec((1, N_IDX_HEADS, QB), lambda b, i: (b * nblk + i, 0, 0)),
                  pl.BlockSpec((1, seq, IDX_DIM), lambda b, i: (b, 0, 0)),
                  pl.BlockSpec((1, n_heads, QB, head_dim), blk),
                  pl.BlockSpec((1, seq, kv_w), lambda b, i: (b, 0, 0)),
                  pl.BlockSpec((1, kv_w, seq), lambda b, i: (b, 0, 0))],
        out_specs=pl.BlockSpec((QB, n_heads * head_dim), lambda b, i: (b * nblk + i, 0)),
        out_shape=jax.ShapeDtypeStruct((n_seq * seq, n_heads * head_dim), F32),
        scratch_shapes=[pltpu.VMEM((seq, QB), I32),
                        pltpu.VMEM((1, n_heads * QB), F32),
                        pltpu.VMEM((1, n_heads * QB), F32),
                        pltpu.VMEM((head_dim, n_heads * QB), F32),
                        pltpu.VMEM((1, QB), I32),
                        pltpu.VMEM((KT, n_heads * QB), F32),
                        pltpu.VMEM((KT, N_IDX_HEADS * QB), F32)],
        compiler_params=_cparams("arbitrary", "arbitrary"),
        name="attn_prompt",
    )(qi_hm, w_hm, ki_bf.reshape(n_seq, seq, IDX_DIM), q_hm, k_bf.reshape(n_seq, seq, kv_w), vt_bf)


SCORE_PAGES = 32
ATTN_PAGES = 16
ATTN_SEQS = 2


def _sample_score_kernel(pt_ref, qi_ref, w_ref, kin_ref, *refs, t_new):
    pages, (sp_ref, sn_ref) = refs[:SCORE_PAGES], refs[SCORE_PAGES:]
    page = pages[0].shape[2]
    qi = qi_ref[0]
    rows = qi.shape[0]
    wb = jnp.broadcast_to(w_ref[0] * IDX_SCALE, (rows, page))

    def score(keys_t_bf):
        s = jnp.dot(qi, keys_t_bf, preferred_element_type=F32)
        r = jnp.maximum(s, 0.0) * wb
        return jnp.sum(r.reshape(N_IDX_HEADS, t_new, page), axis=0)

    for j in range(SCORE_PAGES):
        sp_ref[0, :, j * page:(j + 1) * page] = score(pages[j][0].astype(BF16))

    @pl.when(pl.program_id(1) == 0)
    def _():
        sn = score(kin_ref[0])
        s_idx = lax.broadcasted_iota(I32, (t_new, page), 1)
        t_idx = lax.broadcasted_iota(I32, (t_new, page), 0)
        sn_ref[0] = jnp.where(s_idx <= t_idx, sn, -jnp.inf)


def _sample_scores(page_table, qi_s, w_s, ki_new_t, cache_kidx_t, t_new):
    n_seq, n_pages = page_table.shape
    page = cache_kidx_t.shape[2]
    rows = qi_s.shape[1]
    page_specs = [pl.BlockSpec((1, IDX_DIM, page),
                               functools.partial(lambda b, c, pt, j: (pt[b, c * SCORE_PAGES + j], 0, 0), j=j))
                  for j in range(SCORE_PAGES)]
    grid_spec = pltpu.PrefetchScalarGridSpec(
        num_scalar_prefetch=1,
        grid=(n_seq, n_pages // SCORE_PAGES),
        in_specs=[pl.BlockSpec((1, rows, IDX_DIM), lambda b, c, pt: (b, 0, 0)),
                  pl.BlockSpec((1, rows, 1), lambda b, c, pt: (b, 0, 0)),
                  pl.BlockSpec((1, IDX_DIM, page), lambda b, c, pt: (b, 0, 0))] + page_specs,
        out_specs=[pl.BlockSpec((1, t_new, SCORE_PAGES * page), lambda b, c, pt: (b, 0, c)),
                   pl.BlockSpec((1, t_new, page), lambda b, c, pt: (b, 0, 0))],
    )
    return pl.pallas_call(
        functools.partial(_sample_score_kernel, t_new=t_new),
        grid_spec=grid_spec,
        out_shape=[jax.ShapeDtypeStruct((n_seq, t_new, n_pages * page), F32),
                   jax.ShapeDtypeStruct((n_seq, t_new, page), F32)],
        compiler_params=_cparams("arbitrary", "arbitrary"),
        name="sample_scores",
    )(page_table, qi_s, w_s, ki_new_t, *([cache_kidx_t] * SCORE_PAGES))


def _sample_thresh_kernel(sp_ref, sn_ref, ex_ref, mask_ref, key_ref, cut_ref, *, n_sel, n_past, idx_bits):
    rows = sp_ref.shape[0]
    n_tiles = n_past // LANES + 1
    key_ref[:, 0:n_past] = _order_key(sp_ref[...])
    key_ref[:, n_past:n_past + LANES] = _order_key(sn_ref[...])
    lane = lax.broadcasted_iota(I32, (rows, LANES), 1)

    def count(pred):
        def body(j, cnt):
            c0 = pl.multiple_of(j * LANES, LANES)
            return cnt + jnp.where(pred(key_ref[:, pl.ds(c0, LANES)], c0 + lane), 1, 0)
        cnt = lax.fori_loop(0, n_tiles, body, jnp.zeros((rows, LANES), I32), unroll=8)
        return jnp.broadcast_to(jnp.sum(cnt, axis=1, keepdims=True), (rows, LANES))

    def search(it, thr):
        cand = thr + lax.shift_left(jnp.int32(1), 31 - it)
        c = count(lambda kk, pos: kk >= cand)
        return jnp.where(c >= n_sel, cand, thr)

    thr = lax.fori_loop(0, 32, search, jnp.full((rows, LANES), INT_MIN, I32))
    cnt_ge = count(lambda kk, pos: kk >= thr)
    need = n_sel - count(lambda kk, pos: kk > thr)
    cut_ref[...] = jnp.full((rows, LANES), INT_MAX, I32)

    @pl.when(jnp.max(cnt_ge) > n_sel)
    def _():
        def search_pos(it, p):
            cand = p + lax.shift_left(jnp.int32(1), idx_bits - 1 - it)
            c = count(lambda kk, pos: jnp.where(kk == thr, pos, INT_MAX) < cand)
            return jnp.where(c < need, cand, p)
        cut_ref[...] = lax.fori_loop(0, idx_bits, search_pos, jnp.zeros((rows, LANES), I32))

    few = thr <= KEY_NEG_INF
    thr_eff = jnp.where(few, KEY_NEG_INF + 1, thr)
    cut = jnp.where(few, INT_MAX, cut_ref[...])

    ex = ex_ref[...]
    width = ex.shape[1]

    def emit(j, carry):
        c0 = pl.multiple_of(j * LANES, LANES)
        hit = jnp.where((key_ref[:, pl.ds(c0, LANES)] - jnp.where(c0 + lane > cut, 1, 0)) >= thr_eff, 1.0, 0.0)
        mask_ref[:, pl.ds(pl.multiple_of(j * width, width), width)] = jnp.dot(
            hit.astype(BF16), ex, preferred_element_type=F32)
        return carry

    lax.fori_loop(0, n_tiles, emit, 0, unroll=4)


def _sample_thresh(sp, sn, n_sel, rows_per_step=64):
    m, n_past = sp.shape
    rows_per_step = min(rows_per_step, m)
    tok = jnp.arange(LANES, dtype=I32)[:, None]
    col = jnp.arange(LANES * N_KV_HEADS, dtype=I32)[None, :]
    expand = (col // N_KV_HEADS == tok).astype(BF16)
    width = (n_past + LANES) * N_KV_HEADS
    return pl.pallas_call(
        functools.partial(_sample_thresh_kernel, n_sel=n_sel, n_past=n_past,
                          idx_bits=int(n_past + LANES).bit_length()),
        grid=(m // rows_per_step,),
        in_specs=[pl.BlockSpec((rows_per_step, n_past), lambda r: (r, 0)),
                  pl.BlockSpec((rows_per_step, LANES), lambda r: (r, 0)),
                  pl.BlockSpec(expand.shape, lambda r: (0, 0))],
        out_specs=pl.BlockSpec((rows_per_step, width), lambda r: (r, 0)),
        out_shape=jax.ShapeDtypeStruct((m, width), F32),
        scratch_shapes=[pltpu.VMEM((rows_per_step, n_past + LANES), I32),
                        pltpu.VMEM((rows_per_step, LANES), I32)],
        compiler_params=_cparams("arbitrary"),
        name="sample_thresh",
    )(sp, sn, expand)


def _sample_attn_kernel(pt_ref, q_ref, mask_ref, maskn_ref, kn_ref, vn_ref, *refs, c_exp, group_rows):
    n_pg = ATTN_SEQS * ATTN_PAGES
    k_pages, v_pages = refs[:n_pg], refs[n_pg:2 * n_pg]
    o_ref, m_ref, l_ref, acc_ref = refs[2 * n_pg:]
    c = pl.program_id(1)
    rows = q_ref.shape[1]
    width = maskn_ref.shape[2]
    reps = rows // mask_ref.shape[1]
    own_head = jnp.where(lax.broadcasted_iota(I32, (rows, width), 1) % N_KV_HEADS ==
                         lax.broadcasted_iota(I32, (rows, width), 0) // group_rows, 1.0, 0.0)

    @pl.when(c == 0)
    def _():
        m_ref[...] = jnp.full(m_ref.shape, M_INIT, F32)
        l_ref[...] = jnp.zeros(l_ref.shape, F32)
        acc_ref[...] = jnp.zeros(acc_ref.shape, F32)

    def select(flags):
        return jnp.concatenate([flags] * reps, axis=0) * own_head > 0.5

    def attend(s, tiles):
        q = q_ref[s]
        raws = [jnp.where(sel, lax.dot_general(q, k, _CONTRACT_LAST, preferred_element_type=F32), NEG_BIG)
                for sel, k, _ in tiles]
        m_old = m_ref[s]
        m_new = jnp.maximum(m_old, jnp.max(functools.reduce(jnp.maximum, raws), axis=1, keepdims=True))
        alpha = jnp.exp2((m_old - m_new) * c_exp)
        ps = [jnp.exp2((raw - m_new) * c_exp) for raw in raws]
        l_ref[s] = alpha * l_ref[s] + jnp.sum(functools.reduce(jnp.add, ps), axis=1, keepdims=True)
        m_ref[s] = m_new
        pv = functools.reduce(jnp.add, [jnp.dot(p.astype(BF16), v, preferred_element_type=F32)
                                        for p, (_, _, v) in zip(ps, tiles)])
        acc_ref[s] = alpha * acc_ref[s] + pv

    for s in range(ATTN_SEQS):
        attend(s, [(select(mask_ref[s, :, j * width:(j + 1) * width]),
                    k_pages[s * ATTN_PAGES + j][0].astype(BF16), v_pages[s * ATTN_PAGES + j][0].astype(BF16))
                   for j in range(ATTN_PAGES)])

    @pl.when(c == pl.num_programs(1) - 1)
    def _():
        for s in range(ATTN_SEQS):
            attend(s, [(select(maskn_ref[s]), kn_ref[s], vn_ref[s])])
            o_ref[s] = acc_ref[s] / l_ref[s]


def _sample_attn(page_table, q_s, mask, k_new, v_new, cache_k, cache_v, t_new):
    n_seq, n_pages = page_table.shape
    page_rows, head_dim = cache_k.shape[1], cache_k.shape[2]
    rows = q_s.shape[1]
    page_specs = [pl.BlockSpec((1, page_rows, head_dim),
                               functools.partial(lambda b, c, pt, s, j: (pt[b * ATTN_SEQS + s, c * ATTN_PAGES + j], 0, 0),
                                                 s=s, j=j))
                  for s in range(ATTN_SEQS) for j in range(ATTN_PAGES)]
    per_seq = lambda shape: pl.BlockSpec((ATTN_SEQS,) + shape, lambda b, c, pt: (b, 0, 0))
    grid_spec = pltpu.PrefetchScalarGridSpec(
        num_scalar_prefetch=1,
        grid=(n_seq // ATTN_SEQS, n_pages // ATTN_PAGES),
        in_specs=[per_seq((rows, head_dim)),
                  pl.BlockSpec((ATTN_SEQS, t_new, ATTN_PAGES * page_rows), lambda b, c, pt: (b, 0, c)),
                  pl.BlockSpec((ATTN_SEQS, t_new, page_rows), lambda b, c, pt: (b, 0, n_pages)),
                  per_seq((page_rows, head_dim)), per_seq((page_rows, head_dim))] + page_specs + page_specs,
        out_specs=per_seq((rows, head_dim)),
        scratch_shapes=[pltpu.VMEM((ATTN_SEQS, rows, 1), F32),
                        pltpu.VMEM((ATTN_SEQS, rows, 1), F32),
                        pltpu.VMEM((ATTN_SEQS, rows, head_dim), F32)],
    )
    return pl.pallas_call(
        functools.partial(_sample_attn_kernel, c_exp=head_dim ** -0.5 * LOG2_E, group_rows=rows // N_KV_HEADS),
        grid_spec=grid_spec,
        out_shape=jax.ShapeDtypeStruct((n_seq, rows, head_dim), F32),
        compiler_params=_cparams("arbitrary", "arbitrary"),
        name="sample_attn",
    )(page_table, q_s, mask, mask, k_new, v_new,
      *([cache_k] * (ATTN_SEQS * ATTN_PAGES)), *([cache_v] * (ATTN_SEQS * ATTN_PAGES)))


def _merge_kernel(x_ref, gt_ref, co_ref, ao_ref, go_ref, wc_ref, wa_ref, o_ref, on_ref, *, n_heads):
    @pl.when(pl.program_id(1) == 0)
    def _():
        g = go_ref[...]
        for h in range(n_heads):
            cs = slice(h * LANES, (h + 1) * LANES)
            o = ao_ref[:, cs]
            on_ref[:, cs] = (o * lax.rsqrt(jnp.mean(o * o, axis=-1, keepdims=True) + EPS) * g).astype(BF16)

    y = jnp.dot(co_ref[...].astype(BF16), wc_ref[...].astype(BF16), preferred_element_type=F32)
    y = y + jnp.dot(on_ref[...], wa_ref[...].astype(BF16), preferred_element_type=F32)
    o_ref[...] = x_ref[...] + gt_ref[0] * y


def _merge(x, gt, conv_o, attn_o, g_o, w_out, tm, tn=512):
    m, d = x.shape
    g, r, _ = gt.shape
    d_conv = conv_o.shape[1]
    d_attn = attn_o.shape[1]
    tiles_per_group = (m // tm) // g
    rb = d_conv // d_attn
    return pl.pallas_call(
        functools.partial(_merge_kernel, n_heads=d_attn // LANES),
        grid=(m // tm, d // tn),
        in_specs=[pl.BlockSpec((tm, tn), lambda i, j: (i, j)),
                  pl.BlockSpec((1, r, tn), lambda i, j: (i // tiles_per_group, 0, j)),
                  pl.BlockSpec((tm, d_conv), lambda i, j: (i, 0)),
                  pl.BlockSpec((tm, d_attn), lambda i, j: (i, 0)),
                  pl.BlockSpec((1, LANES), lambda i, j: (0, 0)),
                  pl.BlockSpec((d_conv, tn), lambda i, j: (0, j)),
                  pl.BlockSpec((d_attn, tn), lambda i, j: (rb, j))],
        out_specs=pl.BlockSpec((tm, tn), lambda i, j: (i, j)),
        out_shape=jax.ShapeDtypeStruct((m, d), F32),
        scratch_shapes=[pltpu.VMEM((tm, d_attn), BF16)],
        compiler_params=_cparams("arbitrary", "arbitrary"),
        name="merge",
    )(x, gt, conv_o, attn_o, g_o.reshape(1, -1), w_out, w_out)


def _ffn_act(cur_g, p1_g, p2_g, cur_v, p1_v, p2_v, wg, wv, bg, bv):
    gate = p2_g * wg[0:1, :] + p1_g * wg[1:2, :] + cur_g * wg[2:3, :] + bg
    val = p2_v * wv[0:1, :] + p1_v * wv[1:2, :] + cur_v * wv[2:3, :] + bv
    return (_silu(gate) * val).astype(BF16)


def _ffn_finish(f, act, wd_ref, x_ref, gt_ref, o_ref, acc_ref):
    @pl.when(f == 0)
    def _():
        acc_ref[...] = jnp.zeros(acc_ref.shape, F32)

    acc_ref[...] += jnp.dot(act, wd_ref[...].astype(BF16), preferred_element_type=F32)

    @pl.when(f == pl.num_programs(1) - 1)
    def _():
        o_ref[...] = x_ref[...] + gt_ref[0] * acc_ref[...]


def _to_bf16_kernel(x_ref, o_ref):
    o_ref[...] = x_ref[...].astype(BF16)


def _to_bf16(w, cols_per_step=512):
    r, c = w.shape
    return pl.pallas_call(
        _to_bf16_kernel,
        grid=(c // cols_per_step,),
        in_specs=[pl.BlockSpec((r, cols_per_step), lambda j: (0, j))],
        out_specs=pl.BlockSpec((r, cols_per_step), lambda j: (0, j)),
        out_shape=jax.ShapeDtypeStruct((r, c), BF16),
        compiler_params=_cparams("arbitrary"),
        name="to_bf16",
    )(w)


FFN_HALO = 16
FFN_PARTS = 2


def _ffn_prompt_kernel(x_ref, xh_ref, g_ref, sc_ref, sh_ref, wug_ref, wuv_ref, wg_ref, wv_ref, bg_ref, bv_ref,
                       wd_ref, gt_ref, o_ref, tg_ref, tv_ref, h_ref, hist_g, hist_v, act_ref, *, tm, tiles_per_seq):
    i, f = pl.program_id(0), pl.program_id(1)

    def normmod(x):
        y = x * lax.rsqrt(jnp.mean(x * x, axis=-1, keepdims=True) + EPS) * g_ref[...]
        return (y * (1.0 + sc_ref[0]) + sh_ref[0]).astype(BF16)

    @pl.when(f == 0)
    def _():
        h_ref[0:FFN_HALO, :] = normmod(xh_ref[...])
        h_ref[FFN_HALO:FFN_HALO + tm, :] = normmod(x_ref[...])
        o_ref[...] = jnp.zeros(o_ref.shape, F32)

    n_parts, _, pw = hist_g.shape
    for s in range(n_parts):
        cs = slice(s * pw, (s + 1) * pw)
        hist_g[s] = jnp.dot(h_ref[...], wug_ref[:, cs], preferred_element_type=F32)
        hist_v[s] = jnp.dot(h_ref[...], wuv_ref[:, cs], preferred_element_type=F32)

    @pl.when((i % tiles_per_seq) == 0)
    def _():
        hist_g[:, 0:FFN_HALO, :] = jnp.zeros((n_parts, FFN_HALO, pw), F32)
        hist_v[:, 0:FFN_HALO, :] = jnp.zeros((n_parts, FFN_HALO, pw), F32)

    chunk = 64
    down = None
    for s in range(n_parts):
        cs = slice(s * pw, (s + 1) * pw)
        wg, wv, bg, bv = wg_ref[:, cs], wv_ref[:, cs], bg_ref[:, cs], bv_ref[:, cs]
        for r0 in range(0, tm, chunk):
            rows = lambda hist, k: hist[s, FFN_HALO + r0 - k:FFN_HALO + r0 - k + chunk, :]
            act_ref[s, r0:r0 + chunk, :] = _ffn_act(rows(hist_g, 0), rows(hist_g, 1), rows(hist_g, 2),
                                                    rows(hist_v, 0), rows(hist_v, 1), rows(hist_v, 2),
                                                    wg, wv, bg, bv)
        tg_ref[0, :, cs] = hist_g[s, tm + FFN_HALO - SUBLANES:tm + FFN_HALO, :]
        tv_ref[0, :, cs] = hist_v[s, tm + FFN_HALO - SUBLANES:tm + FFN_HALO, :]
        part = jnp.dot(act_ref[s], wd_ref[cs, :], preferred_element_type=F32)
        down = part if down is None else down + part
    o_ref[...] += down

    @pl.when(f == pl.num_programs(1) - 1)
    def _():
        o_ref[...] = x_ref[...] + gt_ref[0] * o_ref[...]


def _ffn_prompt(x1, gain, sc, sh, gt, w_up_bf, w_dw, b_dw, w_down_bf, seq, tm, tf=512):
    m, d = x1.shape
    d_ff = w_down_bf.shape[0]
    nf = d_ff // tf
    tiles_per_seq = seq // tm
    fw = w_dw.shape[0]
    halo_idx = lambda i: jnp.maximum(i * (tm // FFN_HALO) - 1, 0)
    mod = pl.BlockSpec((1, 1, d), lambda i, f: (i // tiles_per_seq, 0, 0))
    tail = pl.BlockSpec((1, SUBLANES, tf), lambda i, f: (i, 0, f))
    return pl.pallas_call(
        functools.partial(_ffn_prompt_kernel, tm=tm, tiles_per_seq=tiles_per_seq),
        grid=(m // tm, nf),
        in_specs=[pl.BlockSpec((tm, d), lambda i, f: (i, 0)),
                  pl.BlockSpec((FFN_HALO, d), lambda i, f: (halo_idx(i), 0)),
                  pl.BlockSpec((1, d), lambda i, f: (0, 0)),
                  mod, mod,
                  pl.BlockSpec((d, tf), lambda i, f: (0, f)),
                  pl.BlockSpec((d, tf), lambda i, f: (0, f + nf)),
                  pl.BlockSpec((fw, tf), lambda i, f: (0, f)),
                  pl.BlockSpec((fw, tf), lambda i, f: (0, f + nf)),
                  pl.BlockSpec((1, tf), lambda i, f: (0, f)),
                  pl.BlockSpec((1, tf), lambda i, f: (0, f + nf)),
                  pl.BlockSpec((tf, d), lambda i, f: (f, 0)),
                  mod],
        out_specs=[pl.BlockSpec((tm, d), lambda i, f: (i, 0)), tail, tail],
        out_shape=[jax.ShapeDtypeStruct((m, d), F32),
                   jax.ShapeDtypeStruct((m // tm, SUBLANES, d_ff), F32),
                   jax.ShapeDtypeStruct((m // tm, SUBLANES, d_ff), F32)],
        scratch_shapes=[pltpu.VMEM((FFN_HALO + tm, d), BF16),
                        pltpu.VMEM((FFN_PARTS, FFN_HALO + tm, tf // FFN_PARTS), F32),
                        pltpu.VMEM((FFN_PARTS, FFN_HALO + tm, tf // FFN_PARTS), F32),
                        pltpu.VMEM((FFN_PARTS, tm, tf // FFN_PARTS), BF16)],
        compiler_params=_cparams("arbitrary", "arbitrary"),
        name="ffn_prompt",
    )(x1, x1, gain.reshape(1, d), sc, sh, w_up_bf, w_up_bf, w_dw, w_dw, b_dw.reshape(1, -1), b_dw.reshape(1, -1),
      w_down_bf, gt)


def _ffn_sample_kernel(cg_ref, cv_ref, p1g_ref, p1v_ref, p2g_ref, p2v_ref, wg_ref, wv_ref, bg_ref, bv_ref,
                       wd_ref, x_ref, gt_ref, o_ref, acc_ref):
    act = _ffn_act(cg_ref[...], p1g_ref[...], p2g_ref[...], cv_ref[...], p1v_ref[...], p2v_ref[...],
                   wg_ref[...], wv_ref[...], bg_ref[...], bv_ref[...])
    _ffn_finish(pl.program_id(1), act, wd_ref, x_ref, gt_ref, o_ref, acc_ref)


def _ffn_sample(cur, prev1, prev2, x1, gt, w_dw, b_dw, w_down, tf=512):
    m, d = x1.shape
    d_ff = w_down.shape[0]
    nf = d_ff // tf
    fw = w_dw.shape[0]
    lo = pl.BlockSpec((m, tf), lambda i, f: (0, f))
    hi = pl.BlockSpec((m, tf), lambda i, f: (0, f + nf))
    return pl.pallas_call(
        _ffn_sample_kernel,
        grid=(1, nf),
        in_specs=[lo, hi, lo, hi, lo, hi,
                  pl.BlockSpec((fw, tf), lambda i, f: (0, f)),
                  pl.BlockSpec((fw, tf), lambda i, f: (0, f + nf)),
                  pl.BlockSpec((1, tf), lambda i, f: (0, f)),
                  pl.BlockSpec((1, tf), lambda i, f: (0, f + nf)),
                  pl.BlockSpec((tf, d), lambda i, f: (f, 0)),
                  pl.BlockSpec((m, d), lambda i, f: (0, 0)),
                  pl.BlockSpec((1, m, d), lambda i, f: (0, 0, 0))],
        out_specs=pl.BlockSpec((m, d), lambda i, f: (0, 0)),
        out_shape=jax.ShapeDtypeStruct((m, d), F32),
        scratch_shapes=[pltpu.VMEM((m, d), F32)],
        compiler_params=_cparams("arbitrary", "arbitrary"),
        name="ffn_sample",
    )(cur, cur, prev1, prev1, prev2, prev2, w_dw, w_dw, b_dw.reshape(1, -1), b_dw.reshape(1, -1),
      w_down, x1, gt)


def _rope_tables(pos):
    def tab(dim):
        inv = jnp.power(ROPE_THETA, -jnp.arange(0, dim, 2, dtype=F32) / dim)
        ang = pos.astype(F32)[:, None] * inv[None, :]
        cos, sin = jnp.cos(ang), jnp.sin(ang)
        reps = LANES // dim
        return (jnp.tile(jnp.concatenate([cos, cos], axis=-1), (1, reps)),
                jnp.tile(jnp.concatenate([-sin, sin], axis=-1), (1, reps)))
    return tab(LANES) + tab(IDX_DIM)


def _in_proj(x2d, norm1, sc, sh, w_in_t, n_main, tm):
    z = _normmod_matmul(x2d, norm1, sc, sh, w_in_t, n_main, tm, 512, "in_proj", w_transposed=True)
    w_tail = jnp.pad(w_in_t[n_main:], ((0, LANES - (w_in_t.shape[0] - n_main)), (0, 0)))
    zt = _normmod_matmul(x2d, norm1, sc, sh, w_tail, LANES, tm, LANES, "in_proj_tail", w_transposed=True)
    return z, zt


def kernel(x_prompt, x_sample, cache_k, cache_v, cache_kidx, state_conv, state_ffn, page_table, c_prompt, c_sample, norm1, w_ada, b_ada, w_in, g_q, g_k, w_dw_a, b_dw_a, gn_g, gn_b, g_o, w_out, norm2, w_up, w_dw_f, b_dw_f, w_down):
    n_b, seq, d = x_prompt.shape
    n_s, t_new, _ = x_sample.shape
    depth = norm1.shape[0]
    assert depth == 1
    head_dim = g_q.shape[-1]
    assert head_dim == LANES
    d_conv = w_dw_a.shape[-1]
    d_attn = w_out.shape[1] - d_conv
    n_heads = d_attn // head_dim
    kv_w = N_KV_HEADS * head_dim
    n_main = 2 * d_conv + d_attn + 2 * kv_w + N_IDX_HEADS * IDX_DIM
    n_pool, page = cache_k.shape[1], cache_k.shape[2]
    n_pages = page_table.shape[1]
    n_past = n_pages * page
    mp, ms = n_b * seq, n_s * t_new
    heads_per_kv = n_heads // N_KV_HEADS

    n_c = n_b + n_s
    pad_c = (-n_c) % SUBLANES
    c_all = jnp.concatenate([c_prompt, c_sample, jnp.zeros((pad_c, d), F32)], axis=0)
    mods = _ada(c_all, w_ada[0], b_ada[0])
    mp6 = mods[:n_b].reshape(n_b, 6, 1, d)
    sh1p, sc1p, gt1p, sh2p, sc2p, gt2p = [mp6[:, k] for k in range(6)]
    ms6 = jnp.repeat(mods[n_b:n_c].reshape(n_s, 6, d), t_new, axis=0)
    sh1s, sc1s, gt1s, sh2s, sc2s, gt2s = [ms6[:, k][None] for k in range(6)]

    w_in_t = _to_bf16(jnp.swapaxes(w_in[0], 0, 1))
    w_out_bf = _to_bf16(w_out[0])

    xp = x_prompt.reshape(mp, d)
    tm_p = 1024 if seq % 1024 == 0 else QB
    z, zt = _in_proj(xp, norm1[0], sc1p, sh1p, w_in_t, n_main, tm_p)
    conv_o, new_conv_p = _conv_prompt(z, n_b, seq, d_conv, w_dw_a[0], b_dw_a[0], gn_g[0], gn_b[0],
                                      tm=256 if seq % 256 == 0 else QB)
    tabs_p = _rope_tables(jnp.arange(seq, dtype=I32))
    q_hm, qi_hm, k_p, k_bf, vt_bf, ki_p, ki_bf = _qk_epilogue(z, zt, tabs_p, g_q[0], g_k[0], n_b, seq, n_heads)
    w_hm = zt[:, IDX_DIM:IDX_DIM + N_IDX_HEADS].reshape(mp // QB, QB, N_IDX_HEADS).transpose(0, 2, 1)
    n_sel_p = min(TOPK_MAX, seq // 4)
    attn_p = _attn_prompt(qi_hm, w_hm, ki_bf, q_hm, k_bf, vt_bf, n_b, seq, n_sel_p)
    x1p = _merge(xp, gt1p, conv_o, attn_p, g_o[0], w_out_bf,tm_p)
    w_up_bf, w_down_bf = _to_bf16(w_up[0]), _to_bf16(w_down[0])
    tm_f = 512 if seq % 512 == 0 else QB
    y_p, u_tail_g, u_tail_v = _ffn_prompt(x1p, norm2[0], sc2p, sh2p, gt2p, w_up_bf, w_dw_f[0], b_dw_f[0],
                                          w_down_bf, seq, tm_f)
    v_p = z[:, 2 * d_conv + d_attn + kv_w:2 * d_conv + d_attn + 2 * kv_w]
    fw = w_dw_f.shape[1]
    last_tile = (jnp.arange(n_b) + 1) * (seq // tm_f) - 1
    new_ffn_p = jnp.concatenate([u_tail_g[last_tile], u_tail_v[last_tile]], axis=-1)[:, SUBLANES - (fw - 1):]

    xs = x_sample.reshape(ms, d)
    zs, zts = _in_proj(xs, norm1[0], sc1s, sh1s, w_in_t, n_main, ms)
    conv_os, new_conv_s = _conv_sample(zs, state_conv[0], d_conv, w_dw_a[0], b_dw_a[0], gn_g[0], gn_b[0])
    pos_s = jnp.tile(n_past + jnp.arange(t_new, dtype=I32), ms // t_new)
    tabs_s = _rope_tables(pos_s)
    q_hm_s, qi_hm_s, k_s, k_bf_s, _, ki_s, ki_bf_s = _qk_epilogue(zs, zts, tabs_s, g_q[0], g_k[0], 1, ms, n_heads)
    v_s = zs[:, 2 * d_conv + d_attn + kv_w:2 * d_conv + d_attn + 2 * kv_w]
    seq_rows = lambda a, nh: (a.reshape(ms // QB, nh, QB // t_new, t_new, a.shape[-1])
                              .transpose(0, 2, 1, 3, 4).reshape(n_s, nh * t_new, a.shape[-1]))
    qi_s = seq_rows(qi_hm_s, N_IDX_HEADS)
    q_s = seq_rows(q_hm_s, n_heads)
    w_s = (zts[:, IDX_DIM:IDX_DIM + N_IDX_HEADS].reshape(n_s, t_new, N_IDX_HEADS)
           .transpose(0, 2, 1).reshape(n_s, N_IDX_HEADS * t_new, 1))
    ki_new_t = jnp.pad(ki_bf_s.reshape(n_s, t_new, IDX_DIM).transpose(0, 2, 1), ((0, 0), (0, 0), (0, page - t_new)))
    kv_new = lambda a: jnp.pad(a.reshape(n_s, t_new * N_KV_HEADS, head_dim),
                               ((0, 0), (0, (page - t_new) * N_KV_HEADS), (0, 0)))
    kidx_t = jnp.swapaxes(cache_kidx[0], 1, 2)
    sp, sn = _sample_scores(page_table, qi_s, w_s, ki_new_t, kidx_t, t_new)
    n_sel_s = min(TOPK_MAX, (n_past + t_new) // 4)
    mask = _sample_thresh(sp.reshape(ms, n_past), sn.reshape(ms, page), n_sel_s)
    o_s = _sample_attn(page_table, q_s, mask.reshape(n_s, t_new, -1), kv_new(k_bf_s), kv_new(v_s.astype(BF16)),
                       cache_k[0].reshape(n_pool, page * N_KV_HEADS, head_dim),
                       cache_v[0].reshape(n_pool, page * N_KV_HEADS, head_dim), t_new)
    attn_s = o_s.reshape(n_s, n_heads, t_new, head_dim).transpose(0, 2, 1, 3).reshape(ms, d_attn)
    x1s = _merge(xs, gt1s, conv_os, attn_s, g_o[0], w_out_bf,ms)
    u_s = _normmod_matmul(x1s, norm2[0], sc2s, sh2s, w_up_bf, w_up.shape[-1], ms, 512, "up_proj")
    u_hist = jnp.concatenate([state_ffn[0], u_s.reshape(n_s, t_new, -1)], axis=1)
    shifted = lambda k: u_hist[:, k:k + t_new].reshape(ms, -1)
    y_s = _ffn_sample(shifted(2), shifted(1), shifted(0), x1s, gt2s, w_dw_f[0], b_dw_f[0], w_down_bf)
    new_ffn_s = u_hist[:, t_new:]

    return (y_p.reshape(n_b, seq, d), y_s.reshape(n_s, t_new, d),
            k_p.reshape(1, n_b, seq, N_KV_HEADS, head_dim), v_p.reshape(1, n_b, seq, N_KV_HEADS, head_dim),
            ki_p.reshape(1, n_b, seq, IDX_DIM), new_conv_p[None], new_ffn_p[None],
            k_s.reshape(1, n_s, t_new, N_KV_HEADS, head_dim), v_s.reshape(1, n_s, t_new, N_KV_HEADS, head_dim),
            ki_s.reshape(1, n_s, t_new, IDX_DIM), new_conv_s[None], new_ffn_s[None])
```

```python
import functools

import jax
import jax.numpy as jnp
from jax import lax
from jax.experimental import pallas as pl
from jax.experimental.pallas import tpu as pltpu

F32 = jnp.float32
BF16 = jnp.bfloat16
I32 = jnp.int32

CONV_GROUPS = 8
N_KV_HEADS = 2
N_IDX_HEADS = 16
IDX_DIM = 64
TOPK_MAX = 256
ROPE_THETA = 10000.0
EPS = 1e-6
IDX_SCALE = (IDX_DIM ** -0.5) * (N_IDX_HEADS ** -0.5)

LANES = 128
SUBLANES = 8
QB = 128
VMEM_LIMIT_BYTES = 56 * 1024 * 1024
NEG_BIG = -1e30
M_INIT = -1e29
LOG2_E = 1.4426950408889634
INT_MIN = -2 ** 31
INT_MAX = 2 ** 31 - 1
KEY_NEG_INF = -2139095041


def _cparams(*sem):
    return pltpu.CompilerParams(dimension_semantics=sem, vmem_limit_bytes=VMEM_LIMIT_BYTES)


def _silu(x):
    return x * jax.nn.sigmoid(x)


def _order_key(x):
    bits = pltpu.bitcast(x, I32)
    return bits ^ ((bits >> 31) & INT_MAX)


def _ada_kernel(c_ref, w_ref, b_ref, o_ref):
    s = _silu(c_ref[...]).astype(BF16)
    o_ref[...] = jnp.dot(s, w_ref[...].astype(BF16), preferred_element_type=F32) + b_ref[...]


def _ada(c_all, w_ada, b_ada, tn=1024):
    r, d = c_all.shape
    n = w_ada.shape[1]
    return pl.pallas_call(
        _ada_kernel,
        grid=(n // tn,),
        in_specs=[pl.BlockSpec((r, d), lambda j: (0, 0)),
                  pl.BlockSpec((d, tn), lambda j: (0, j)),
                  pl.BlockSpec((1, tn), lambda j: (0, j))],
        out_specs=pl.BlockSpec((r, tn), lambda j: (0, j)),
        out_shape=jax.ShapeDtypeStruct((r, n), F32),
        compiler_params=_cparams("arbitrary"),
        name="ada",
    )(c_all, w_ada, b_ada.reshape(1, n))


_CONTRACT_LAST = (((1,), (1,)), ((), ()))


def _normmod_matmul_kernel(x_ref, g_ref, sc_ref, sh_ref, w_ref, o_ref, h_ref, *, w_transposed):
    @pl.when(pl.program_id(1) == 0)
    def _():
        x = x_ref[...]
        y = x * lax.rsqrt(jnp.mean(x * x, axis=-1, keepdims=True) + EPS) * g_ref[...]
        h_ref[...] = (y * (1.0 + sc_ref[0]) + sh_ref[0]).astype(BF16)

    w = w_ref[...].astype(BF16)
    if w_transposed:
        o_ref[...] = lax.dot_general(h_ref[...], w, _CONTRACT_LAST, preferred_element_type=F32)
    else:
        o_ref[...] = jnp.dot(h_ref[...], w, preferred_element_type=F32)


def _normmod_matmul(x, gain, sc, sh, w, ncols, tm, tn, name, w_transposed=False):
    m, d = x.shape
    g, r, _ = sc.shape
    tiles_per_group = (m // tm) // g
    mod_spec = pl.BlockSpec((1, r, d), lambda i, j: (i // tiles_per_group, 0, 0))
    w_spec = pl.BlockSpec((tn, d), lambda i, j: (j, 0)) if w_transposed else pl.BlockSpec((d, tn), lambda i, j: (0, j))
    return pl.pallas_call(
        functools.partial(_normmod_matmul_kernel, w_transposed=w_transposed),
        grid=(m // tm, ncols // tn),
        in_specs=[pl.BlockSpec((tm, d), lambda i, j: (i, 0)),
                  pl.BlockSpec((1, d), lambda i, j: (0, 0)),
                  mod_spec, mod_spec,
                  w_spec],
        out_specs=pl.BlockSpec((tm, tn), lambda i, j: (i, j)),
        out_shape=jax.ShapeDtypeStruct((m, ncols), F32),
        scratch_shapes=[pltpu.VMEM((tm, d), BF16)],
        compiler_params=_cparams("arbitrary", "arbitrary"),
        name=name,
    )(x, gain.reshape(1, d), sc, sh, w)


def _conv_gn_silu(hist_ref, off, rows, wdw_ref, bdw_ref, gng_ref, gnb_ref, o_ref, row_chunk):
    conv_w = wdw_ref.shape[0]
    for r0 in range(0, rows, row_chunk):
        for c in range(CONV_GROUPS):
            cs = slice(c * LANES, (c + 1) * LANES)
            acc = jnp.zeros((row_chunk, LANES), F32)
            for r in range(SUBLANES):
                n_rows = row_chunk if r == 0 else row_chunk + SUBLANES
                part = jnp.zeros((n_rows, LANES), F32)
                for w in range((r - off) % SUBLANES, conv_w, SUBLANES):
                    a0 = off + r0 + w - r
                    part = part + hist_ref[a0:a0 + n_rows, cs] * wdw_ref[w:w + 1, cs]
                acc = acc + part[r:r + row_chunk]
            y = acc + bdw_ref[:, cs]
            mu = jnp.mean(y, axis=-1, keepdims=True)
            dlt = y - mu
            var = jnp.mean(dlt * dlt, axis=-1, keepdims=True)
            yn = dlt * lax.rsqrt(var + EPS) * gng_ref[:, cs] + gnb_ref[:, cs]
            o_ref[r0:r0 + row_chunk, cs] = _silu(yn).astype(o_ref.dtype)


HALO = 32


def _conv_prompt_kernel(za_ref, zg_ref, ha_ref, hg_ref, wdw_ref, bdw_ref, gng_ref, gnb_ref,
                        o_ref, newc_ref, hist_ref, *, tm, tiles_per_seq):
    i = pl.program_id(0)
    first = (i % tiles_per_seq) == 0
    a_halo = ha_ref[...] * jax.nn.sigmoid(hg_ref[...])
    hist_ref[0:HALO, :] = jnp.where(first, 0.0, a_halo)
    hist_ref[HALO:HALO + tm, :] = za_ref[...] * jax.nn.sigmoid(zg_ref[...])
    conv_w = wdw_ref.shape[0]
    _conv_gn_silu(hist_ref, HALO - (conv_w - 1), tm, wdw_ref, bdw_ref, gng_ref, gnb_ref, o_ref, 64)

    @pl.when((i % tiles_per_seq) == tiles_per_seq - 1)
    def _():
        newc_ref[0] = hist_ref[HALO + tm - (conv_w - 1):HALO + tm, :]


def _conv_prompt(z, n_seq, seq, d_conv, wdw, bdw, gng, gnb, tm=256):
    m = z.shape[0]
    conv_w = wdw.shape[0]
    tiles_per_seq = seq // tm
    cb = 1
    halo_idx = lambda i: jnp.maximum(i * (tm // HALO) - 1, 0)
    vec = pl.BlockSpec((1, d_conv), lambda i: (0, 0))
    return pl.pallas_call(
        functools.partial(_conv_prompt_kernel, tm=tm, tiles_per_seq=tiles_per_seq),
        grid=(m // tm,),
        in_specs=[pl.BlockSpec((tm, d_conv), lambda i: (i, 0)),
                  pl.BlockSpec((tm, d_conv), lambda i: (i, cb)),
                  pl.BlockSpec((HALO, d_conv), lambda i: (halo_idx(i), 0)),
                  pl.BlockSpec((HALO, d_conv), lambda i: (halo_idx(i), cb)),
                  pl.BlockSpec((conv_w, d_conv), lambda i: (0, 0)),
                  vec, vec, vec],
        out_specs=[pl.BlockSpec((tm, d_conv), lambda i: (i, 0)),
                   pl.BlockSpec((1, conv_w - 1, d_conv), lambda i: (i // tiles_per_seq, 0, 0))],
        out_shape=[jax.ShapeDtypeStruct((m, d_conv), BF16),
                   jax.ShapeDtypeStruct((n_seq, conv_w - 1, d_conv), F32)],
        scratch_shapes=[pltpu.VMEM((HALO + tm, d_conv), F32)],
        compiler_params=_cparams("arbitrary"),
        name="conv_prompt",
    )(z, z, z, z, wdw, bdw.reshape(1, -1), gng.reshape(1, -1), gnb.reshape(1, -1))


def _conv_sample_kernel(za_ref, zg_ref, st_ref, wdw_ref, bdw_ref, gng_ref, gnb_ref,
                        o_ref, newc_ref, hist_ref, *, t_new):
    conv_w = wdw_ref.shape[0]
    hist_ref[0:conv_w - 1, :] = st_ref[0]
    hist_ref[conv_w - 1:conv_w - 1 + t_new, :] = za_ref[...] * jax.nn.sigmoid(zg_ref[...])
    _conv_gn_silu(hist_ref, 0, t_new, wdw_ref, bdw_ref, gng_ref, gnb_ref, o_ref, t_new)
    newc_ref[0] = hist_ref[t_new:t_new + conv_w - 1, :]


def _conv_sample(z, state, d_conv, wdw, bdw, gng, gnb):
    n_seq, hist_rows, _ = state.shape
    conv_w = wdw.shape[0]
    m = z.shape[0]
    t_new = m // n_seq
    vec = pl.BlockSpec((1, d_conv), lambda b: (0, 0))
    return pl.pallas_call(
        functools.partial(_conv_sample_kernel, t_new=t_new),
        grid=(n_seq,),
        in_specs=[pl.BlockSpec((t_new, d_conv), lambda b: (b, 0)),
                  pl.BlockSpec((t_new, d_conv), lambda b: (b, 1)),
                  pl.BlockSpec((1, hist_rows, d_conv), lambda b: (b, 0, 0)),
                  pl.BlockSpec((conv_w, d_conv), lambda b: (0, 0)),
                  vec, vec, vec],
        out_specs=[pl.BlockSpec((t_new, d_conv), lambda b: (b, 0)),
                   pl.BlockSpec((1, hist_rows, d_conv), lambda b: (b, 0, 0))],
        out_shape=[jax.ShapeDtypeStruct((m, d_conv), F32),
                   jax.ShapeDtypeStruct((n_seq, hist_rows, d_conv), F32)],
        scratch_shapes=[pltpu.VMEM((hist_rows + t_new + SUBLANES, d_conv), F32)],
        compiler_params=_cparams("arbitrary"),
        name="conv_sample",
    )(z, z, state, wdw, bdw.reshape(1, -1), gng.reshape(1, -1), gnb.reshape(1, -1))


def _qk_kernel(zq_ref, zkv_ref, zqi0_ref, zqi1_ref, zt_ref, cos_ref, sin_ref, cosi_ref, sini_ref,
               gq_ref, gk_ref, qhm_ref, qihm_ref, k_ref, kbf_ref, vt_ref, ki_ref, kibf_ref, *, n_heads):
    cos, sin = cos_ref[...], sin_ref[...]
    cosi, sini = cosi_ref[...], sini_ref[...]
    tm = cos.shape[0]
    lane = lax.broadcasted_iota(I32, (tm, LANES), 1)
    low_half = (lane % IDX_DIM) < (IDX_DIM // 2)

    def norm_rope(x, g):
        y = x * lax.rsqrt(jnp.mean(x * x, axis=-1, keepdims=True) + EPS) * g
        return y * cos + pltpu.roll(y, LANES // 2, 1) * sin

    def rope_idx(x):
        r = jnp.where(low_half, pltpu.roll(x, LANES - IDX_DIM // 2, 1), pltpu.roll(x, IDX_DIM // 2, 1))
        return x * cosi + r * sini

    gq, gk = gq_ref[...], gk_ref[...]
    for h in range(n_heads):
        qhm_ref[0, h] = norm_rope(zq_ref[:, h * LANES:(h + 1) * LANES], gq).astype(BF16)
    kv_w = N_KV_HEADS * LANES
    for g in range(N_KV_HEADS):
        kg = norm_rope(zkv_ref[:, g * LANES:(g + 1) * LANES], gk)
        k_ref[:, g * LANES:(g + 1) * LANES] = kg
        kbf_ref[:, g * LANES:(g + 1) * LANES] = kg.astype(BF16)
    vt_ref[0] = zkv_ref[:, kv_w:2 * kv_w].T.astype(BF16)
    half = (N_IDX_HEADS * IDX_DIM) // 2
    for j in range(N_IDX_HEADS // 2):
        src = zqi0_ref if j * LANES < half else zqi1_ref
        c0 = (j * LANES) % half
        y = rope_idx(src[:, c0:c0 + LANES])
        qihm_ref[0, 2 * j] = y[:, :IDX_DIM].astype(BF16)
        qihm_ref[0, 2 * j + 1] = y[:, IDX_DIM:].astype(BF16)
    yk = rope_idx(zt_ref[...])[:, :IDX_DIM]
    ki_ref[...] = yk
    kibf_ref[...] = yk.astype(BF16)


def _qk_epilogue(z, zt, tabs, gq, gk, n_seq, seq, n_heads):
    m = z.shape[0]
    tm = QB
    head_dim = LANES
    cos, sin, cosi, sini = tabs
    tab_tiles = cos.shape[0] // tm
    kv_w = N_KV_HEADS * head_dim
    d_attn = n_heads * head_dim
    d_conv = d_attn
    q_cb = (2 * d_conv) // d_attn
    kv_cb = (2 * d_conv + d_attn) // (2 * kv_w)
    qi_w = (N_IDX_HEADS * IDX_DIM) // 2
    qi_cb = (2 * d_conv + d_attn + 2 * kv_w) // qi_w
    tiles_per_seq = seq // tm
    tab = pl.BlockSpec((tm, LANES), lambda i: (i % tab_tiles, 0))
    vec = pl.BlockSpec((1, LANES), lambda i: (0, 0))
    return pl.pallas_call(
        functools.partial(_qk_kernel, n_heads=n_heads),
        grid=(m // tm,),
        in_specs=[pl.BlockSpec((tm, d_attn), lambda i: (i, q_cb)),
                  pl.BlockSpec((tm, 2 * kv_w), lambda i: (i, kv_cb)),
                  pl.BlockSpec((tm, qi_w), lambda i: (i, qi_cb)),
                  pl.BlockSpec((tm, qi_w), lambda i: (i, qi_cb + 1)),
                  pl.BlockSpec((tm, LANES), lambda i: (i, 0)),
                  tab, tab, tab, tab, vec, vec],
        out_specs=[pl.BlockSpec((1, n_heads, tm, head_dim), lambda i: (i, 0, 0, 0)),
                   pl.BlockSpec((1, N_IDX_HEADS, tm, IDX_DIM), lambda i: (i, 0, 0, 0)),
                   pl.BlockSpec((tm, kv_w), lambda i: (i, 0)),
                   pl.BlockSpec((tm, kv_w), lambda i: (i, 0)),
                   pl.BlockSpec((1, kv_w, tm), lambda i: (i // tiles_per_seq, 0, i % tiles_per_seq)),
                   pl.BlockSpec((tm, IDX_DIM), lambda i: (i, 0)),
                   pl.BlockSpec((tm, IDX_DIM), lambda i: (i, 0))],
        out_shape=[jax.ShapeDtypeStruct((m // tm, n_heads, tm, head_dim), BF16),
                   jax.ShapeDtypeStruct((m // tm, N_IDX_HEADS, tm, IDX_DIM), BF16),
                   jax.ShapeDtypeStruct((m, kv_w), F32),
                   jax.ShapeDtypeStruct((m, kv_w), BF16),
                   jax.ShapeDtypeStruct((n_seq, kv_w, seq), BF16),
                   jax.ShapeDtypeStruct((m, IDX_DIM), F32),
                   jax.ShapeDtypeStruct((m, IDX_DIM), BF16)],
        compiler_params=_cparams("arbitrary"),
        name="qk_epilogue",
    )(z, z, z, z, zt, cos, sin, cosi, sini, gq.reshape(1, -1), gk.reshape(1, -1))


def _kth_largest_key(lo, hi, count_ge, n_sel):
    n_bits = jnp.max(32 - lax.clz(hi - lo))

    def step(it, thr):
        cand = thr + lax.shift_left(jnp.int32(1), n_bits - 1 - it)
        keep = count_ge(cand) >= n_sel
        return jnp.where(cand > thr, jnp.where(keep, cand, thr), thr)

    return lax.fori_loop(0, n_bits, step, lo)


KT = 2 * QB


def _attn_prompt_kernel(qi_ref, w_ref, ki_ref, q_ref, k_ref, vt_ref, o_ref,
                        key_ref, m_ref, l_ref, acc_ref, cut_ref, lg_ref, s_ref, gmax_ref, *, n_sel, c_exp, n_heads,
                        idx_bits):
    i = pl.program_id(1)
    n_kt = lax.div(i * QB + QB + KT - 1, KT)
    row_iota = lax.broadcasted_iota(I32, (KT, QB), 0)
    w_all = w_ref[0] * IDX_SCALE
    heads_per_kv = n_heads // N_KV_HEADS
    hc = 4

    def score_tile(kt, carry):
        ki_t = ki_ref[0, pl.ds(pl.multiple_of(kt * KT, KT), KT), :]
        for h0 in range(0, N_IDX_HEADS, hc):
            s_ref[:, h0 * QB:(h0 + hc) * QB] = lax.dot_general(
                ki_t, qi_ref[0, h0:h0 + hc].reshape(hc * QB, IDX_DIM), _CONTRACT_LAST,
                preferred_element_type=F32)
        for half in range(KT // QB):
            rs = slice(half * QB, (half + 1) * QB)
            ks = pl.multiple_of(kt * KT + half * QB, QB)
            acc = jnp.zeros((QB, QB), F32)
            for h in range(N_IDX_HEADS):
                acc = acc + jnp.maximum(s_ref[rs, h * QB:(h + 1) * QB], 0.0) * w_all[h:h + 1, :]
            acc = jnp.where(ks + lax.broadcasted_iota(I32, (QB, QB), 0) <=
                            i * QB + lax.broadcasted_iota(I32, (QB, QB), 1), acc, -jnp.inf)
            key = _order_key(acc)
            key_ref[pl.ds(ks, QB), :] = key
            gmax_ref[rs, :] = jnp.maximum(gmax_ref[rs, :], key)
        return carry

    gmax_ref[...] = jnp.full(gmax_ref.shape, INT_MIN, I32)
    lax.fori_loop(0, n_kt, score_tile, 0)

    def count(pred):
        def tile(kt, cnt):
            ks = pl.multiple_of(kt * KT, KT)
            hit = jnp.where(pred(key_ref[pl.ds(ks, KT), :], ks + row_iota), 1, 0)
            return cnt + jnp.sum(hit.reshape(KT // SUBLANES, SUBLANES, QB), axis=0)
        n_pairs = lax.shift_right_logical(n_kt, 1)
        cnt = lax.fori_loop(0, n_pairs, lambda p, cnt: tile(2 * p + 1, tile(2 * p, cnt)),
                            jnp.zeros((SUBLANES, QB), I32))
        cnt = lax.fori_loop(2 * n_pairs, n_kt, tile, cnt)
        return jnp.sum(cnt, axis=0, keepdims=True)

    gmax = gmax_ref[...]
    thr = _kth_largest_key(jnp.min(gmax, axis=0, keepdims=True), jnp.max(gmax, axis=0, keepdims=True),
                           lambda cand: count(lambda kk, pos: kk >= cand), n_sel)
    cnt_ge = count(lambda kk, pos: kk >= thr)
    need = n_sel - count(lambda kk, pos: kk > thr)
    cut_ref[...] = jnp.full((1, QB), INT_MAX, I32)

    @pl.when(jnp.max(cnt_ge) > n_sel)
    def _():
        def search_pos(it, p):
            cand = p + lax.shift_left(jnp.int32(1), idx_bits - 1 - it)
            c = count(lambda kk, pos: jnp.where(kk == thr, pos, INT_MAX) < cand)
            return jnp.where(c < need, cand, p)
        cut_ref[...] = lax.fori_loop(0, idx_bits, search_pos, jnp.zeros((1, QB), I32))

    few = thr <= KEY_NEG_INF
    thr_eff = jnp.where(few, KEY_NEG_INF + 1, thr)
    cut = jnp.where(few, INT_MAX, cut_ref[...])

    m_ref[...] = jnp.full(m_ref.shape, M_INIT, F32)
    l_ref[...] = jnp.zeros(l_ref.shape, F32)
    acc_ref[...] = jnp.zeros(acc_ref.shape, F32)

    def attn_tile(kt, carry):
        ks = pl.multiple_of(kt * KT, KT)
        kk = key_ref[pl.ds(ks, KT), :]
        sel = (kk - jnp.where(ks + row_iota > cut, 1, 0)) >= thr_eff
        m_old, l_old = m_ref[...], l_ref[...]
        m_parts, l_parts = [], []
        for g in range(N_KV_HEADS):
            k_t = k_ref[0, pl.ds(ks, KT), g * LANES:(g + 1) * LANES]
            q_g = q_ref[0, g * heads_per_kv:(g + 1) * heads_per_kv].reshape(heads_per_kv * QB, LANES)
            lg_ref[:, g * heads_per_kv * QB:(g + 1) * heads_per_kv * QB] = lax.dot_general(
                k_t, q_g, _CONTRACT_LAST, preferred_element_type=F32)
        for g in range(N_KV_HEADS):
            v_t = vt_ref[0, g * LANES:(g + 1) * LANES, pl.ds(ks, KT)]
            for hh in range(heads_per_kv):
                cs = slice((g * heads_per_kv + hh) * QB, (g * heads_per_kv + hh + 1) * QB)
                raw = jnp.where(sel, lg_ref[:, cs], NEG_BIG)
                m_new = jnp.maximum(m_old[:, cs], jnp.max(raw, axis=0, keepdims=True))
                alpha = jnp.exp2((m_old[:, cs] - m_new) * c_exp)
                p = jnp.exp2((raw - m_new) * c_exp)
                m_parts.append(m_new)
                l_parts.append(alpha * l_old[:, cs] + jnp.sum(p, axis=0, keepdims=True))
                acc_ref[:, cs] = alpha * acc_ref[:, cs] + jnp.dot(v_t, p.astype(BF16),
                                                                   preferred_element_type=F32)
        m_ref[...] = jnp.concatenate(m_parts, axis=1)
        l_ref[...] = jnp.concatenate(l_parts, axis=1)
        return carry

    lax.fori_loop(0, n_kt, attn_tile, 0)
    for h in range(n_heads):
        cs = slice(h * QB, (h + 1) * QB)
        o_ref[:, cs] = (acc_ref[:, cs] / l_ref[:, cs]).T


def _attn_prompt(qi_hm, w_hm, ki_bf, q_hm, k_bf, vt_bf, n_seq, seq, n_sel):
    n_heads = q_hm.shape[1]
    head_dim = q_hm.shape[3]
    nblk = seq // QB
    kv_w = k_bf.shape[-1]
    assert n_sel <= KT and seq % KT == 0
    blk = lambda b, i: (b * nblk + i, 0, 0, 0)
    return pl.pallas_call(
        functools.partial(_attn_prompt_kernel, n_sel=n_sel, c_exp=head_dim ** -0.5 * LOG2_E, n_heads=n_heads,
                          idx_bits=int(seq).bit_length()),
        grid=(n_seq, nblk),
        in_specs=[pl.BlockSpec((1, N_IDX_HEADS, QB, IDX_DIM), blk),
                  pl.BlockSpec((1, N_IDX_HEADS, QB), lambda b, i: (b * nblk + i, 0, 0)),
                  pl.BlockSpec((1, seq, IDX_DIM), lambda b, i: (b, 0, 0)),
                  pl.BlockSpec((1, n_heads, QB, head_dim), blk),
                  pl.BlockSpec((1, seq, kv_w), lambda b, i: (b, 0, 0)),
                  pl.BlockSpec((1, kv_w, seq), lambda b, i: (b, 0, 0))],
        out_specs=pl.BlockSpec((QB, n_heads * head_dim), lambda b, i: (b * nblk + i, 0)),
        out_shape=jax.ShapeDtypeStruct((n_seq * seq, n_heads * head_dim), F32),
        scratch_shapes=[pltpu.VMEM((seq, QB), I32),
                        pltpu.VMEM((1, n_heads * QB), F32),
                        pltpu.VMEM((1, n_heads * QB), F32),
                        pltpu.VMEM((head_dim, n_heads * QB), F32),
                        pltpu.VMEM((1, QB), I32),
                        pltpu.VMEM((KT, n_heads * QB), F32),
                        pltpu.VMEM((KT, N_IDX_HEADS * QB), F32),
                        pltpu.VMEM((KT, QB), I32)],
        compiler_params=_cparams("arbitrary", "arbitrary"),
        name="attn_prompt",
    )(qi_hm, w_hm, ki_bf.reshape(n_seq, seq, IDX_DIM), q_hm, k_bf.reshape(n_seq, seq, kv_w), vt_bf)


SCORE_PAGES = 32
ATTN_PAGES = 16
ATTN_SEQS = 2


def _sample_score_kernel(pt_ref, qi_ref, w_ref, kin_ref, *refs, t_new):
    pages, (sp_ref, sn_ref) = refs[:SCORE_PAGES], refs[SCORE_PAGES:]
    page = pages[0].shape[2]
    qi = qi_ref[0]
    rows = qi.shape[0]
    wb = jnp.broadcast_to(w_ref[0] * IDX_SCALE, (rows, page))

    def score(keys_t_bf):
        s = jnp.dot(qi, keys_t_bf, preferred_element_type=F32)
        r = jnp.maximum(s, 0.0) * wb
        return jnp.sum(r.reshape(N_IDX_HEADS, t_new, page), axis=0)

    for j in range(SCORE_PAGES):
        sp_ref[0, :, j * page:(j + 1) * page] = score(pages[j][0].astype(BF16))

    @pl.when(pl.program_id(1) == 0)
    def _():
        sn = score(kin_ref[0])
        s_idx = lax.broadcasted_iota(I32, (t_new, page), 1)
        t_idx = lax.broadcasted_iota(I32, (t_new, page), 0)
        sn_ref[0] = jnp.where(s_idx <= t_idx, sn, -jnp.inf)


def _sample_scores(page_table, qi_s, w_s, ki_new_t, cache_kidx_t, t_new):
    n_seq, n_pages = page_table.shape
    page = cache_kidx_t.shape[2]
    rows = qi_s.shape[1]
    page_specs = [pl.BlockSpec((1, IDX_DIM, page),
                               functools.partial(lambda b, c, pt, j: (pt[b, c * SCORE_PAGES + j], 0, 0), j=j))
                  for j in range(SCORE_PAGES)]
    grid_spec = pltpu.PrefetchScalarGridSpec(
        num_scalar_prefetch=1,
        grid=(n_seq, n_pages // SCORE_PAGES),
        in_specs=[pl.BlockSpec((1, rows, IDX_DIM), lambda b, c, pt: (b, 0, 0)),
                  pl.BlockSpec((1, rows, 1), lambda b, c, pt: (b, 0, 0)),
                  pl.BlockSpec((1, IDX_DIM, page), lambda b, c, pt: (b, 0, 0))] + page_specs,
        out_specs=[pl.BlockSpec((1, t_new, SCORE_PAGES * page), lambda b, c, pt: (b, 0, c)),
                   pl.BlockSpec((1, t_new, page), lambda b, c, pt: (b, 0, 0))],
    )
    return pl.pallas_call(
        functools.partial(_sample_score_kernel, t_new=t_new),
        grid_spec=grid_spec,
        out_shape=[jax.ShapeDtypeStruct((n_seq, t_new, n_pages * page), F32),
                   jax.ShapeDtypeStruct((n_seq, t_new, page), F32)],
        compiler_params=_cparams("arbitrary", "arbitrary"),
        name="sample_scores",
    )(page_table, qi_s, w_s, ki_new_t, *([cache_kidx_t] * SCORE_PAGES))


def _sample_thresh_kernel(sp_ref, sn_ref, ex_ref, mask_ref, key_ref, cut_ref, *, n_sel, n_past, idx_bits):
    rows = sp_ref.shape[0]
    n_tiles = n_past // LANES + 1
    key_ref[:, 0:n_past] = _order_key(sp_ref[...])
    key_ref[:, n_past:n_past + LANES] = _order_key(sn_ref[...])
    lane = lax.broadcasted_iota(I32, (rows, LANES), 1)

    def count(pred):
        def body(j, cnt):
            c0 = pl.multiple_of(j * LANES, LANES)
            return cnt + jnp.where(pred(key_ref[:, pl.ds(c0, LANES)], c0 + lane), 1, 0)
        cnt = lax.fori_loop(0, n_tiles, body, jnp.zeros((rows, LANES), I32), unroll=8)
        return jnp.broadcast_to(jnp.sum(cnt, axis=1, keepdims=True), (rows, LANES))

    def class_max(j, carry):
        c0 = pl.multiple_of(j * 2 * LANES, 2 * LANES)
        return (jnp.maximum(carry[0], key_ref[:, pl.ds(c0, LANES)]),
                jnp.maximum(carry[1], key_ref[:, pl.ds(c0 + LANES, LANES)]))

    floor = jnp.full((rows, LANES), INT_MIN, I32)
    even, odd = lax.fori_loop(0, n_tiles // 2, class_max, (floor, floor), unroll=8)
    if n_tiles % 2:
        even = jnp.maximum(even, key_ref[:, (n_tiles - 1) * LANES:n_tiles * LANES])
    lo = jnp.broadcast_to(jnp.min(jnp.minimum(even, odd), axis=1, keepdims=True), (rows, LANES))
    hi = jnp.broadcast_to(jnp.max(jnp.maximum(even, odd), axis=1, keepdims=True), (rows, LANES))
    thr = _kth_largest_key(lo, hi, lambda cand: count(lambda kk, pos: kk >= cand), n_sel)
    cnt_ge = count(lambda kk, pos: kk >= thr)
    need = n_sel - count(lambda kk, pos: kk > thr)
    cut_ref[...] = jnp.full((rows, LANES), INT_MAX, I32)

    @pl.when(jnp.max(cnt_ge) > n_sel)
    def _():
        def search_pos(it, p):
            cand = p + lax.shift_left(jnp.int32(1), idx_bits - 1 - it)
            c = count(lambda kk, pos: jnp.where(kk == thr, pos, INT_MAX) < cand)
            return jnp.where(c < need, cand, p)
        cut_ref[...] = lax.fori_loop(0, idx_bits, search_pos, jnp.zeros((rows, LANES), I32))

    few = thr <= KEY_NEG_INF
    thr_eff = jnp.where(few, KEY_NEG_INF + 1, thr)
    cut = jnp.where(few, INT_MAX, cut_ref[...])

    ex = ex_ref[...]
    width = ex.shape[1]

    def emit(j, carry):
        c0 = pl.multiple_of(j * LANES, LANES)
        hit = jnp.where((key_ref[:, pl.ds(c0, LANES)] - jnp.where(c0 + lane > cut, 1, 0)) >= thr_eff, 1.0, 0.0)
        mask_ref[:, pl.ds(pl.multiple_of(j * width, width), width)] = jnp.dot(
            hit.astype(BF16), ex, preferred_element_type=F32)
        return carry

    lax.fori_loop(0, n_tiles, emit, 0, unroll=4)


def _sample_thresh(sp, sn, n_sel, rows_per_step=64):
    m, n_past = sp.shape
    assert n_sel <= 2 * LANES and n_past >= LANES
    rows_per_step = min(rows_per_step, m)
    tok = jnp.arange(LANES, dtype=I32)[:, None]
    col = jnp.arange(LANES * N_KV_HEADS, dtype=I32)[None, :]
    expand = (col // N_KV_HEADS == tok).astype(BF16)
    width = (n_past + LANES) * N_KV_HEADS
    return pl.pallas_call(
        functools.partial(_sample_thresh_kernel, n_sel=n_sel, n_past=n_past,
                          idx_bits=int(n_past + LANES).bit_length()),
        grid=(m // rows_per_step,),
        in_specs=[pl.BlockSpec((rows_per_step, n_past), lambda r: (r, 0)),
                  pl.BlockSpec((rows_per_step, LANES), lambda r: (r, 0)),
                  pl.BlockSpec(expand.shape, lambda r: (0, 0))],
        out_specs=pl.BlockSpec((rows_per_step, width), lambda r: (r, 0)),
        out_shape=jax.ShapeDtypeStruct((m, width), F32),
        scratch_shapes=[pltpu.VMEM((rows_per_step, n_past + LANES), I32),
                        pltpu.VMEM((rows_per_step, LANES), I32)],
        compiler_params=_cparams("arbitrary"),
        name="sample_thresh",
    )(sp, sn, expand)


def _sample_attn_kernel(pt_ref, q_ref, mask_ref, maskn_ref, kn_ref, vn_ref, *refs, c_exp, group_rows):
    n_pg = ATTN_SEQS * ATTN_PAGES
    k_pages, v_pages = refs[:n_pg], refs[n_pg:2 * n_pg]
    o_ref, m_ref, l_ref, acc_ref = refs[2 * n_pg:]
    c = pl.program_id(1)
    rows = q_ref.shape[1]
    width = maskn_ref.shape[2]
    reps = rows // mask_ref.shape[1]
    own_head = jnp.where(lax.broadcasted_iota(I32, (rows, width), 1) % N_KV_HEADS ==
                         lax.broadcasted_iota(I32, (rows, width), 0) // group_rows, 1.0, 0.0)

    @pl.when(c == 0)
    def _():
        m_ref[...] = jnp.full(m_ref.shape, M_INIT, F32)
        l_ref[...] = jnp.zeros(l_ref.shape, F32)
        acc_ref[...] = jnp.zeros(acc_ref.shape, F32)

    def select(flags):
        return jnp.concatenate([flags] * reps, axis=0) * own_head > 0.5

    def attend(s, tiles):
        q = q_ref[s]
        raws = [jnp.where(sel, lax.dot_general(q, k, _CONTRACT_LAST, preferred_element_type=F32), NEG_BIG)
                for sel, k, _ in tiles]
        m_old = m_ref[s]
        m_new = jnp.maximum(m_old, jnp.max(functools.reduce(jnp.maximum, raws), axis=1, keepdims=True))
        alpha = jnp.exp2((m_old - m_new) * c_exp)
        ps = [jnp.exp2((raw - m_new) * c_exp) for raw in raws]
        l_ref[s] = alpha * l_ref[s] + jnp.sum(functools.reduce(jnp.add, ps), axis=1, keepdims=True)
        m_ref[s] = m_new
        pv = functools.reduce(jnp.add, [jnp.dot(p.astype(BF16), v, preferred_element_type=F32)
                                        for p, (_, _, v) in zip(ps, tiles)])
        acc_ref[s] = alpha * acc_ref[s] + pv

    for s in range(ATTN_SEQS):
        attend(s, [(select(mask_ref[s, :, j * width:(j + 1) * width]),
                    k_pages[s * ATTN_PAGES + j][0].astype(BF16), v_pages[s * ATTN_PAGES + j][0].astype(BF16))
                   for j in range(ATTN_PAGES)])

    @pl.when(c == pl.num_programs(1) - 1)
    def _():
        for s in range(ATTN_SEQS):
            attend(s, [(select(maskn_ref[s]), kn_ref[s], vn_ref[s])])
            o_ref[s] = acc_ref[s] / l_ref[s]


def _sample_attn(page_table, q_s, mask, k_new, v_new, cache_k, cache_v, t_new):
    n_seq, n_pages = page_table.shape
    page_rows, head_dim = cache_k.shape[1], cache_k.shape[2]
    rows = q_s.shape[1]
    page_specs = [pl.BlockSpec((1, page_rows, head_dim),
                               functools.partial(lambda b, c, pt, s, j: (pt[b * ATTN_SEQS + s, c * ATTN_PAGES + j], 0, 0),
                                                 s=s, j=j))
                  for s in range(ATTN_SEQS) for j in range(ATTN_PAGES)]
    per_seq = lambda shape: pl.BlockSpec((ATTN_SEQS,) + shape, lambda b, c, pt: (b, 0, 0))
    grid_spec = pltpu.PrefetchScalarGridSpec(
        num_scalar_prefetch=1,
        grid=(n_seq // ATTN_SEQS, n_pages // ATTN_PAGES),
        in_specs=[per_seq((rows, head_dim)),
                  pl.BlockSpec((ATTN_SEQS, t_new, ATTN_PAGES * page_rows), lambda b, c, pt: (b, 0, c)),
                  pl.BlockSpec((ATTN_SEQS, t_new, page_rows), lambda b, c, pt: (b, 0, n_pages)),
                  per_seq((page_rows, head_dim)), per_seq((page_rows, head_dim))] + page_specs + page_specs,
        out_specs=per_seq((rows, head_dim)),
        scratch_shapes=[pltpu.VMEM((ATTN_SEQS, rows, 1), F32),
                        pltpu.VMEM((ATTN_SEQS, rows, 1), F32),
                        pltpu.VMEM((ATTN_SEQS, rows, head_dim), F32)],
    )
    return pl.pallas_call(
        functools.partial(_sample_attn_kernel, c_exp=head_dim ** -0.5 * LOG2_E, group_rows=rows // N_KV_HEADS),
        grid_spec=grid_spec,
        out_shape=jax.ShapeDtypeStruct((n_seq, rows, head_dim), F32),
        compiler_params=_cparams("arbitrary", "arbitrary"),
        name="sample_attn",
    )(page_table, q_s, mask, mask, k_new, v_new,
      *([cache_k] * (ATTN_SEQS * ATTN_PAGES)), *([cache_v] * (ATTN_SEQS * ATTN_PAGES)))


def _merge_kernel(x_ref, gt_ref, co_ref, ao_ref, go_ref, wc_ref, wa_ref, o_ref, on_ref, *, n_heads):
    @pl.when(pl.program_id(1) == 0)
    def _():
        g = go_ref[...]
        for h in range(n_heads):
            cs = slice(h * LANES, (h + 1) * LANES)
            o = ao_ref[:, cs]
            on_ref[:, cs] = (o * lax.rsqrt(jnp.mean(o * o, axis=-1, keepdims=True) + EPS) * g).astype(BF16)

    y = jnp.dot(co_ref[...].astype(BF16), wc_ref[...].astype(BF16), preferred_element_type=F32)
    y = y + jnp.dot(on_ref[...], wa_ref[...].astype(BF16), preferred_element_type=F32)
    o_ref[...] = x_ref[...] + gt_ref[0] * y


def _merge(x, gt, conv_o, attn_o, g_o, w_out, tm, tn=512):
    m, d = x.shape
    g, r, _ = gt.shape
    d_conv = conv_o.shape[1]
    d_attn = attn_o.shape[1]
    tiles_per_group = (m // tm) // g
    rb = d_conv // d_attn
    return pl.pallas_call(
        functools.partial(_merge_kernel, n_heads=d_attn // LANES),
        grid=(m // tm, d // tn),
        in_specs=[pl.BlockSpec((tm, tn), lambda i, j: (i, j)),
                  pl.BlockSpec((1, r, tn), lambda i, j: (i // tiles_per_group, 0, j)),
                  pl.BlockSpec((tm, d_conv), lambda i, j: (i, 0)),
                  pl.BlockSpec((tm, d_attn), lambda i, j: (i, 0)),
                  pl.BlockSpec((1, LANES), lambda i, j: (0, 0)),
                  pl.BlockSpec((d_conv, tn), lambda i, j: (0, j)),
                  pl.BlockSpec((d_attn, tn), lambda i, j: (rb, j))],
        out_specs=pl.BlockSpec((tm, tn), lambda i, j: (i, j)),
        out_shape=jax.ShapeDtypeStruct((m, d), F32),
        scratch_shapes=[pltpu.VMEM((tm, d_attn), BF16)],
        compiler_params=_cparams("arbitrary", "arbitrary"),
        name="merge",
    )(x, gt, conv_o, attn_o, g_o.reshape(1, -1), w_out, w_out)


def _ffn_act(cur_g, p1_g, p2_g, cur_v, p1_v, p2_v, wg, wv, bg, bv):
    gate = p2_g * wg[0:1, :] + p1_g * wg[1:2, :] + cur_g * wg[2:3, :] + bg
    val = p2_v * wv[0:1, :] + p1_v * wv[1:2, :] + cur_v * wv[2:3, :] + bv
    return (_silu(gate) * val).astype(BF16)


def _ffn_finish(f, act, wd_ref, x_ref, gt_ref, o_ref, acc_ref):
    @pl.when(f == 0)
    def _():
        acc_ref[...] = jnp.zeros(acc_ref.shape, F32)

    acc_ref[...] += jnp.dot(act, wd_ref[...].astype(BF16), preferred_element_type=F32)

    @pl.when(f == pl.num_programs(1) - 1)
    def _():
        o_ref[...] = x_ref[...] + gt_ref[0] * acc_ref[...]


def _to_bf16_kernel(x_ref, o_ref):
    o_ref[...] = x_ref[...].astype(BF16)


def _to_bf16(w, cols_per_step=512):
    r, c = w.shape
    return pl.pallas_call(
        _to_bf16_kernel,
        grid=(c // cols_per_step,),
        in_specs=[pl.BlockSpec((r, cols_per_step), lambda j: (0, j))],
        out_specs=pl.BlockSpec((r, cols_per_step), lambda j: (0, j)),
        out_shape=jax.ShapeDtypeStruct((r, c), BF16),
        compiler_params=_cparams("arbitrary"),
        name="to_bf16",
    )(w)


FFN_HALO = 16
FFN_PARTS = 2


def _ffn_prompt_kernel(x_ref, xh_ref, g_ref, sc_ref, sh_ref, wug_ref, wuv_ref, wg_ref, wv_ref, bg_ref, bv_ref,
                       wd_ref, gt_ref, o_ref, tg_ref, tv_ref, h_ref, hist_g, hist_v, act_ref, *, tm, tiles_per_seq):
    i, f = pl.program_id(0), pl.program_id(1)

    def normmod(x):
        y = x * lax.rsqrt(jnp.mean(x * x, axis=-1, keepdims=True) + EPS) * g_ref[...]
        return (y * (1.0 + sc_ref[0]) + sh_ref[0]).astype(BF16)

    @pl.when(f == 0)
    def _():
        h_ref[0:FFN_HALO, :] = normmod(xh_ref[...])
        h_ref[FFN_HALO:FFN_HALO + tm, :] = normmod(x_ref[...])
        o_ref[...] = jnp.zeros(o_ref.shape, F32)

    n_parts, _, pw = hist_g.shape
    for s in range(n_parts):
        cs = slice(s * pw, (s + 1) * pw)
        hist_g[s] = jnp.dot(h_ref[...], wug_ref[:, cs], preferred_element_type=F32)
        hist_v[s] = jnp.dot(h_ref[...], wuv_ref[:, cs], preferred_element_type=F32)

    @pl.when((i % tiles_per_seq) == 0)
    def _():
        hist_g[:, 0:FFN_HALO, :] = jnp.zeros((n_parts, FFN_HALO, pw), F32)
        hist_v[:, 0:FFN_HALO, :] = jnp.zeros((n_parts, FFN_HALO, pw), F32)

    chunk = 64
    down = None
    for s in range(n_parts):
        cs = slice(s * pw, (s + 1) * pw)
        wg, wv, bg, bv = wg_ref[:, cs], wv_ref[:, cs], bg_ref[:, cs], bv_ref[:, cs]
        for r0 in range(0, tm, chunk):
            rows = lambda hist, k: hist[s, FFN_HALO + r0 - k:FFN_HALO + r0 - k + chunk, :]
            act_ref[s, r0:r0 + chunk, :] = _ffn_act(rows(hist_g, 0), rows(hist_g, 1), rows(hist_g, 2),
                                                    rows(hist_v, 0), rows(hist_v, 1), rows(hist_v, 2),
                                                    wg, wv, bg, bv)
        tg_ref[0, :, cs] = hist_g[s, tm + FFN_HALO - SUBLANES:tm + FFN_HALO, :]
        tv_ref[0, :, cs] = hist_v[s, tm + FFN_HALO - SUBLANES:tm + FFN_HALO, :]
        part = jnp.dot(act_ref[s], wd_ref[cs, :], preferred_element_type=F32)
        down = part if down is None else down + part
    o_ref[...] += down

    @pl.when(f == pl.num_programs(1) - 1)
    def _():
        o_ref[...] = x_ref[...] + gt_ref[0] * o_ref[...]


def _ffn_prompt(x1, gain, sc, sh, gt, w_up_bf, w_dw, b_dw, w_down_bf, seq, tm, tf=512):
    m, d = x1.shape
    d_ff = w_down_bf.shape[0]
    nf = d_ff // tf
    tiles_per_seq = seq // tm
    fw = w_dw.shape[0]
    halo_idx = lambda i: jnp.maximum(i * (tm // FFN_HALO) - 1, 0)
    mod = pl.BlockSpec((1, 1, d), lambda i, f: (i // tiles_per_seq, 0, 0))
    tail = pl.BlockSpec((1, SUBLANES, tf), lambda i, f: (i, 0, f))
    return pl.pallas_call(
        functools.partial(_ffn_prompt_kernel, tm=tm, tiles_per_seq=tiles_per_seq),
        grid=(m // tm, nf),
        in_specs=[pl.BlockSpec((tm, d), lambda i, f: (i, 0)),
                  pl.BlockSpec((FFN_HALO, d), lambda i, f: (halo_idx(i), 0)),
                  pl.BlockSpec((1, d), lambda i, f: (0, 0)),
                  mod, mod,
                  pl.BlockSpec((d, tf), lambda i, f: (0, f)),
                  pl.BlockSpec((d, tf), lambda i, f: (0, f + nf)),
                  pl.BlockSpec((fw, tf), lambda i, f: (0, f)),
                  pl.BlockSpec((fw, tf), lambda i, f: (0, f + nf)),
                  pl.BlockSpec((1, tf), lambda i, f: (0, f)),
                  pl.BlockSpec((1, tf), lambda i, f: (0, f + nf)),
                  pl.BlockSpec((tf, d), lambda i, f: (f, 0)),
                  mod],
        out_specs=[pl.BlockSpec((tm, d), lambda i, f: (i, 0)), tail, tail],
        out_shape=[jax.ShapeDtypeStruct((m, d), F32),
                   jax.ShapeDtypeStruct((m // tm, SUBLANES, d_ff), F32),
                   jax.ShapeDtypeStruct((m // tm, SUBLANES, d_ff), F32)],
        scratch_shapes=[pltpu.VMEM((FFN_HALO + tm, d), BF16),
                        pltpu.VMEM((FFN_PARTS, FFN_HALO + tm, tf // FFN_PARTS), F32),
                        pltpu.VMEM((FFN_PARTS, FFN_HALO + tm, tf // FFN_PARTS), F32),
                        pltpu.VMEM((FFN_PARTS, tm, tf // FFN_PARTS), BF16)],
        compiler_params=_cparams("arbitrary", "arbitrary"),
        name="ffn_prompt",
    )(x1, x1, gain.reshape(1, d), sc, sh, w_up_bf, w_up_bf, w_dw, w_dw, b_dw.reshape(1, -1), b_dw.reshape(1, -1),
      w_down_bf, gt)


def _ffn_sample_kernel(cg_ref, cv_ref, p1g_ref, p1v_ref, p2g_ref, p2v_ref, wg_ref, wv_ref, bg_ref, bv_ref,
                       wd_ref, x_ref, gt_ref, o_ref, acc_ref):
    act = _ffn_act(cg_ref[...], p1g_ref[...], p2g_ref[...], cv_ref[...], p1v_ref[...], p2v_ref[...],
                   wg_ref[...], wv_ref[...], bg_ref[...], bv_ref[...])
    _ffn_finish(pl.program_id(1), act, wd_ref, x_ref, gt_ref, o_ref, acc_ref)


def _ffn_sample(cur, prev1, prev2, x1, gt, w_dw, b_dw, w_down, tf=512):
    m, d = x1.shape
    d_ff = w_down.shape[0]
    nf = d_ff // tf
    fw = w_dw.shape[0]
    lo = pl.BlockSpec((m, tf), lambda i, f: (0, f))
    hi = pl.BlockSpec((m, tf), lambda i, f: (0, f + nf))
    return pl.pallas_call(
        _ffn_sample_kernel,
        grid=(1, nf),
        in_specs=[lo, hi, lo, hi, lo, hi,
                  pl.BlockSpec((fw, tf), lambda i, f: (0, f)),
                  pl.BlockSpec((fw, tf), lambda i, f: (0, f + nf)),
                  pl.BlockSpec((1, tf), lambda i, f: (0, f)),
                  pl.BlockSpec((1, tf), lambda i, f: (0, f + nf)),
                  pl.BlockSpec((tf, d), lambda i, f: (f, 0)),
                  pl.BlockSpec((m, d), lambda i, f: (0, 0)),
                  pl.BlockSpec((1, m, d), lambda i, f: (0, 0, 0))],
        out_specs=pl.BlockSpec((m, d), lambda i, f: (0, 0)),
        out_shape=jax.ShapeDtypeStruct((m, d), F32),
        scratch_shapes=[pltpu.VMEM((m, d), F32)],
        compiler_params=_cparams("arbitrary", "arbitrary"),
        name="ffn_sample",
    )(cur, cur, prev1, prev1, prev2, prev2, w_dw, w_dw, b_dw.reshape(1, -1), b_dw.reshape(1, -1),
      w_down, x1, gt)


def _rope_tables(pos):
    def tab(dim):
        inv = jnp.power(ROPE_THETA, -jnp.arange(0, dim, 2, dtype=F32) / dim)
        ang = pos.astype(F32)[:, None] * inv[None, :]
        cos, sin = jnp.cos(ang), jnp.sin(ang)
        reps = LANES // dim
        return (jnp.tile(jnp.concatenate([cos, cos], axis=-1), (1, reps)),
                jnp.tile(jnp.concatenate([-sin, sin], axis=-1), (1, reps)))
    return tab(LANES) + tab(IDX_DIM)


def _in_proj(x2d, norm1, sc, sh, w_in_t, n_main, tm):
    z = _normmod_matmul(x2d, norm1, sc, sh, w_in_t, n_main, tm, 512, "in_proj", w_transposed=True)
    w_tail = jnp.pad(w_in_t[n_main:], ((0, LANES - (w_in_t.shape[0] - n_main)), (0, 0)))
    zt = _normmod_matmul(x2d, norm1, sc, sh, w_tail, LANES, tm, LANES, "in_proj_tail", w_transposed=True)
    return z, zt


def kernel(x_prompt, x_sample, cache_k, cache_v, cache_kidx, state_conv, state_ffn, page_table, c_prompt, c_sample, norm1, w_ada, b_ada, w_in, g_q, g_k, w_dw_a, b_dw_a, gn_g, gn_b, g_o, w_out, norm2, w_up, w_dw_f, b_dw_f, w_down):
    n_b, seq, d = x_prompt.shape
    n_s, t_new, _ = x_sample.shape
    depth = norm1.shape[0]
    assert depth == 1
    head_dim = g_q.shape[-1]
    assert head_dim == LANES
    d_conv = w_dw_a.shape[-1]
    d_attn = w_out.shape[1] - d_conv
    n_heads = d_attn // head_dim
    kv_w = N_KV_HEADS * head_dim
    n_main = 2 * d_conv + d_attn + 2 * kv_w + N_IDX_HEADS * IDX_DIM
    n_pool, page = cache_k.shape[1], cache_k.shape[2]
    n_pages = page_table.shape[1]
    n_past = n_pages * page
    mp, ms = n_b * seq, n_s * t_new
    heads_per_kv = n_heads // N_KV_HEADS

    n_c = n_b + n_s
    pad_c = (-n_c) % SUBLANES
    c_all = jnp.concatenate([c_prompt, c_sample, jnp.zeros((pad_c, d), F32)], axis=0)
    mods = _ada(c_all, w_ada[0], b_ada[0])
    mp6 = mods[:n_b].reshape(n_b, 6, 1, d)
    sh1p, sc1p, gt1p, sh2p, sc2p, gt2p = [mp6[:, k] for k in range(6)]
    ms6 = jnp.repeat(mods[n_b:n_c].reshape(n_s, 6, d), t_new, axis=0)
    sh1s, sc1s, gt1s, sh2s, sc2s, gt2s = [ms6[:, k][None] for k in range(6)]

    w_in_t = jnp.swapaxes(w_in[0], 0, 1)

    xp = x_prompt.reshape(mp, d)
    tm_p = 1024 if seq % 1024 == 0 else QB
    z, zt = _in_proj(xp, norm1[0], sc1p, sh1p, w_in_t, n_main, tm_p)
    conv_o, new_conv_p = _conv_prompt(z, n_b, seq, d_conv, w_dw_a[0], b_dw_a[0], gn_g[0], gn_b[0],
                                      tm=256 if seq % 256 == 0 else QB)
    tabs_p = _rope_tables(jnp.arange(seq, dtype=I32))
    q_hm, qi_hm, k_p, k_bf, vt_bf, ki_p, ki_bf = _qk_epilogue(z, zt, tabs_p, g_q[0], g_k[0], n_b, seq, n_heads)
    w_hm = zt[:, IDX_DIM:IDX_DIM + N_IDX_HEADS].reshape(mp // QB, QB, N_IDX_HEADS).transpose(0, 2, 1)
    n_sel_p = min(TOPK_MAX, seq // 4)
    attn_p = _attn_prompt(qi_hm, w_hm, ki_bf, q_hm, k_bf, vt_bf, n_b, seq, n_sel_p)
    x1p = _merge(xp, gt1p, conv_o, attn_p, g_o[0], w_out[0],tm_p)
    w_up_bf, w_down_bf = _to_bf16(w_up[0]), _to_bf16(w_down[0])
    tm_f = 512 if seq % 512 == 0 else QB
    y_p, u_tail_g, u_tail_v = _ffn_prompt(x1p, norm2[0], sc2p, sh2p, gt2p, w_up_bf, w_dw_f[0], b_dw_f[0],
                                          w_down_bf, seq, tm_f)
    v_p = z[:, 2 * d_conv + d_attn + kv_w:2 * d_conv + d_attn + 2 * kv_w]
    fw = w_dw_f.shape[1]
    last_tile = (jnp.arange(n_b) + 1) * (seq // tm_f) - 1
    new_ffn_p = jnp.concatenate([u_tail_g[last_tile], u_tail_v[last_tile]], axis=-1)[:, SUBLANES - (fw - 1):]

    xs = x_sample.reshape(ms, d)
    zs, zts = _in_proj(xs, norm1[0], sc1s, sh1s, w_in_t, n_main, ms)
    conv_os, new_conv_s = _conv_sample(zs, state_conv[0], d_conv, w_dw_a[0], b_dw_a[0], gn_g[0], gn_b[0])
    pos_s = jnp.tile(n_past + jnp.arange(t_new, dtype=I32), ms // t_new)
    tabs_s = _rope_tables(pos_s)
    q_hm_s, qi_hm_s, k_s, k_bf_s, _, ki_s, ki_bf_s = _qk_epilogue(zs, zts, tabs_s, g_q[0], g_k[0], 1, ms, n_heads)
    v_s = zs[:, 2 * d_conv + d_attn + kv_w:2 * d_conv + d_attn + 2 * kv_w]
    seq_rows = lambda a, nh: (a.reshape(ms // QB, nh, QB // t_new, t_new, a.shape[-1])
                              .transpose(0, 2, 1, 3, 4).reshape(n_s, nh * t_new, a.shape[-1]))
    qi_s = seq_rows(qi_hm_s, N_IDX_HEADS)
    q_s = seq_rows(q_hm_s, n_heads)
    w_s = (zts[:, IDX_DIM:IDX_DIM + N_IDX_HEADS].reshape(n_s, t_new, N_IDX_HEADS)
           .transpose(0, 2, 1).reshape(n_s, N_IDX_HEADS * t_new, 1))
    ki_new_t = jnp.pad(ki_bf_s.reshape(n_s, t_new, IDX_DIM).transpose(0, 2, 1), ((0, 0), (0, 0), (0, page - t_new)))
    kv_new = lambda a: jnp.pad(a.reshape(n_s, t_new * N_KV_HEADS, head_dim),
                               ((0, 0), (0, (page - t_new) * N_KV_HEADS), (0, 0)))
    kidx_t = jnp.swapaxes(cache_kidx[0], 1, 2)
    sp, sn = _sample_scores(page_table, qi_s, w_s, ki_new_t, kidx_t, t_new)
    n_sel_s = min(TOPK_MAX, (n_past + t_new) // 4)
    mask = _sample_thresh(sp.reshape(ms, n_past), sn.reshape(ms, page), n_sel_s)
    o_s = _sample_attn(page_table, q_s, mask.reshape(n_s, t_new, -1), kv_new(k_bf_s), kv_new(v_s.astype(BF16)),
                       cache_k[0].reshape(n_pool, page * N_KV_HEADS, head_dim),
                       cache_v[0].reshape(n_pool, page * N_KV_HEADS, head_dim), t_new)
    attn_s = o_s.reshape(n_s, n_heads, t_new, head_dim).transpose(0, 2, 1, 3).reshape(ms, d_attn)
    x1s = _merge(xs, gt1s, conv_os, attn_s, g_o[0], w_out[0],ms)
    u_s = _normmod_matmul(x1s, norm2[0], sc2s, sh2s, w_up_bf, w_up.shape[-1], ms, 512, "up_proj")
    u_hist = jnp.concatenate([state_ffn[0], u_s.reshape(n_s, t_new, -1)], axis=1)
    shifted = lambda k: u_hist[:, k:k + t_new].reshape(ms, -1)
    y_s = _ffn_sample(shifted(2), shifted(1), shifted(0), x1s, gt2s, w_dw_f[0], b_dw_f[0], w_down_bf)
    new_ffn_s = u_hist[:, t_new:]

    return (y_p.reshape(n_b, seq, d), y_s.reshape(n_s, t_new, d),
            k_p.reshape(1, n_b, seq, N_KV_HEADS, head_dim), v_p.reshape(1, n_b, seq, N_KV_HEADS, head_dim),
            ki_p.reshape(1, n_b, seq, IDX_DIM), new_conv_p[None], new_ffn_p[None],
            k_s.reshape(1, n_s, t_new, N_KV_HEADS, head_dim), v_s.reshape(1, n_s, t_new, N_KV_HEADS, head_dim),
            ki_s.reshape(1, n_s, t_new, IDX_DIM), new_conv_s[None], new_ffn_s[None])
```

```python
import functools

import jax
import jax.numpy as jnp
from jax import lax
from jax.experimental import pallas as pl
from jax.experimental.pallas import tpu as pltpu

F32 = jnp.float32
BF16 = jnp.bfloat16
I32 = jnp.int32

CONV_GROUPS = 8
N_KV_HEADS = 2
N_IDX_HEADS = 16
IDX_DIM = 64
TOPK_MAX = 256
ROPE_THETA = 10000.0
EPS = 1e-6
IDX_SCALE = (IDX_DIM ** -0.5) * (N_IDX_HEADS ** -0.5)

LANES = 128
SUBLANES = 8
QB = 128
VMEM_LIMIT_BYTES = 56 * 1024 * 1024
NEG_BIG = -1e30
M_INIT = -1e29
LOG2_E = 1.4426950408889634
INT_MIN = -2 ** 31
INT_MAX = 2 ** 31 - 1
KEY_NEG_INF = -2139095041


def _cparams(*sem):
    return pltpu.CompilerParams(dimension_semantics=sem, vmem_limit_bytes=VMEM_LIMIT_BYTES)


def _silu(x):
    return x * jax.nn.sigmoid(x)


def _order_key(x):
    bits = pltpu.bitcast(x, I32)
    return bits ^ ((bits >> 31) & INT_MAX)


def _ada_kernel(c_ref, w_ref, b_ref, o_ref):
    s = _silu(c_ref[...]).astype(BF16)
    o_ref[...] = jnp.dot(s, w_ref[...].astype(BF16), preferred_element_type=F32) + b_ref[...]


def _ada(c_all, w_ada, b_ada, tn=1024):
    r, d = c_all.shape
    n = w_ada.shape[1]
    return pl.pallas_call(
        _ada_kernel,
        grid=(n // tn,),
        in_specs=[pl.BlockSpec((r, d), lambda j: (0, 0)),
                  pl.BlockSpec((d, tn), lambda j: (0, j)),
                  pl.BlockSpec((1, tn), lambda j: (0, j))],
        out_specs=pl.BlockSpec((r, tn), lambda j: (0, j)),
        out_shape=jax.ShapeDtypeStruct((r, n), F32),
        compiler_params=_cparams("arbitrary"),
        name="ada",
    )(c_all, w_ada, b_ada.reshape(1, n))


_CONTRACT_LAST = (((1,), (1,)), ((), ()))


def _normmod_matmul_kernel(x_ref, g_ref, sc_ref, sh_ref, w_ref, o_ref, h_ref, *, w_transposed):
    @pl.when(pl.program_id(1) == 0)
    def _():
        x = x_ref[...]
        y = x * lax.rsqrt(jnp.mean(x * x, axis=-1, keepdims=True) + EPS) * g_ref[...]
        h_ref[...] = (y * (1.0 + sc_ref[0]) + sh_ref[0]).astype(BF16)

    w = w_ref[...].astype(BF16)
    if w_transposed:
        o_ref[...] = lax.dot_general(h_ref[...], w, _CONTRACT_LAST, preferred_element_type=F32)
    else:
        o_ref[...] = jnp.dot(h_ref[...], w, preferred_element_type=F32)


def _normmod_matmul(x, gain, sc, sh, w, ncols, tm, tn, name, w_transposed=False):
    m, d = x.shape
    g, r, _ = sc.shape
    tiles_per_group = (m // tm) // g
    mod_spec = pl.BlockSpec((1, r, d), lambda i, j: (i // tiles_per_group, 0, 0))
    w_spec = pl.BlockSpec((tn, d), lambda i, j: (j, 0)) if w_transposed else pl.BlockSpec((d, tn), lambda i, j: (0, j))
    return pl.pallas_call(
        functools.partial(_normmod_matmul_kernel, w_transposed=w_transposed),
        grid=(m // tm, ncols // tn),
        in_specs=[pl.BlockSpec((tm, d), lambda i, j: (i, 0)),
                  pl.BlockSpec((1, d), lambda i, j: (0, 0)),
                  mod_spec, mod_spec,
                  w_spec],
        out_specs=pl.BlockSpec((tm, tn), lambda i, j: (i, j)),
        out_shape=jax.ShapeDtypeStruct((m, ncols), F32),
        scratch_shapes=[pltpu.VMEM((tm, d), BF16)],
        compiler_params=_cparams("arbitrary", "arbitrary"),
        name=name,
    )(x, gain.reshape(1, d), sc, sh, w)


def _conv_gn_silu(hist_ref, off, rows, wdw_ref, bdw_ref, gng_ref, gnb_ref, o_ref, row_chunk):
    conv_w = wdw_ref.shape[0]
    for r0 in range(0, rows, row_chunk):
        for c in range(CONV_GROUPS):
            cs = slice(c * LANES, (c + 1) * LANES)
            acc = jnp.zeros((row_chunk, LANES), F32)
            for r in range(SUBLANES):
                n_rows = row_chunk if r == 0 else row_chunk + SUBLANES
                part = jnp.zeros((n_rows, LANES), F32)
                for w in range((r - off) % SUBLANES, conv_w, SUBLANES):
                    a0 = off + r0 + w - r
                    part = part + hist_ref[a0:a0 + n_rows, cs] * wdw_ref[w:w + 1, cs]
                acc = acc + part[r:r + row_chunk]
            y = acc + bdw_ref[:, cs]
            mu = jnp.mean(y, axis=-1, keepdims=True)
            dlt = y - mu
            var = jnp.mean(dlt * dlt, axis=-1, keepdims=True)
            yn = dlt * lax.rsqrt(var + EPS) * gng_ref[:, cs] + gnb_ref[:, cs]
            o_ref[r0:r0 + row_chunk, cs] = _silu(yn).astype(o_ref.dtype)


HALO = 32


def _conv_prompt_kernel(za_ref, zg_ref, ha_ref, hg_ref, wdw_ref, bdw_ref, gng_ref, gnb_ref,
                        o_ref, newc_ref, hist_ref, *, tm, tiles_per_seq):
    i = pl.program_id(0)
    first = (i % tiles_per_seq) == 0
    a_halo = ha_ref[...] * jax.nn.sigmoid(hg_ref[...])
    hist_ref[0:HALO, :] = jnp.where(first, 0.0, a_halo)
    hist_ref[HALO:HALO + tm, :] = za_ref[...] * jax.nn.sigmoid(zg_ref[...])
    conv_w = wdw_ref.shape[0]
    _conv_gn_silu(hist_ref, HALO - (conv_w - 1), tm, wdw_ref, bdw_ref, gng_ref, gnb_ref, o_ref, 64)

    @pl.when((i % tiles_per_seq) == tiles_per_seq - 1)
    def _():
        newc_ref[0] = hist_ref[HALO + tm - (conv_w - 1):HALO + tm, :]


def _conv_prompt(z, n_seq, seq, d_conv, wdw, bdw, gng, gnb, tm=256):
    m = z.shape[0]
    conv_w = wdw.shape[0]
    tiles_per_seq = seq // tm
    cb = 1
    halo_idx = lambda i: jnp.maximum(i * (tm // HALO) - 1, 0)
    vec = pl.BlockSpec((1, d_conv), lambda i: (0, 0))
    return pl.pallas_call(
        functools.partial(_conv_prompt_kernel, tm=tm, tiles_per_seq=tiles_per_seq),
        grid=(m // tm,),
        in_specs=[pl.BlockSpec((tm, d_conv), lambda i: (i, 0)),
                  pl.BlockSpec((tm, d_conv), lambda i: (i, cb)),
                  pl.BlockSpec((HALO, d_conv), lambda i: (halo_idx(i), 0)),
                  pl.BlockSpec((HALO, d_conv), lambda i: (halo_idx(i), cb)),
                  pl.BlockSpec((conv_w, d_conv), lambda i: (0, 0)),
                  vec, vec, vec],
        out_specs=[pl.BlockSpec((tm, d_conv), lambda i: (i, 0)),
                   pl.BlockSpec((1, conv_w - 1, d_conv), lambda i: (i // tiles_per_seq, 0, 0))],
        out_shape=[jax.ShapeDtypeStruct((m, d_conv), BF16),
                   jax.ShapeDtypeStruct((n_seq, conv_w - 1, d_conv), F32)],
        scratch_shapes=[pltpu.VMEM((HALO + tm, d_conv), F32)],
        compiler_params=_cparams("arbitrary"),
        name="conv_prompt",
    )(z, z, z, z, wdw, bdw.reshape(1, -1), gng.reshape(1, -1), gnb.reshape(1, -1))


def _conv_sample_kernel(za_ref, zg_ref, st_ref, wdw_ref, bdw_ref, gng_ref, gnb_ref,
                        o_ref, newc_ref, hist_ref, *, t_new):
    conv_w = wdw_ref.shape[0]
    hist_ref[0:conv_w - 1, :] = st_ref[0]
    hist_ref[conv_w - 1:conv_w - 1 + t_new, :] = za_ref[...] * jax.nn.sigmoid(zg_ref[...])
    _conv_gn_silu(hist_ref, 0, t_new, wdw_ref, bdw_ref, gng_ref, gnb_ref, o_ref, t_new)
    newc_ref[0] = hist_ref[t_new:t_new + conv_w - 1, :]


def _conv_sample(z, state, d_conv, wdw, bdw, gng, gnb):
    n_seq, hist_rows, _ = state.shape
    conv_w = wdw.shape[0]
    m = z.shape[0]
    t_new = m // n_seq
    vec = pl.BlockSpec((1, d_conv), lambda b: (0, 0))
    return pl.pallas_call(
        functools.partial(_conv_sample_kernel, t_new=t_new),
        grid=(n_seq,),
        in_specs=[pl.BlockSpec((t_new, d_conv), lambda b: (b, 0)),
                  pl.BlockSpec((t_new, d_conv), lambda b: (b, 1)),
                  pl.BlockSpec((1, hist_rows, d_conv), lambda b: (b, 0, 0)),
                  pl.BlockSpec((conv_w, d_conv), lambda b: (0, 0)),
                  vec, vec, vec],
        out_specs=[pl.BlockSpec((t_new, d_conv), lambda b: (b, 0)),
                   pl.BlockSpec((1, hist_rows, d_conv), lambda b: (b, 0, 0))],
        out_shape=[jax.ShapeDtypeStruct((m, d_conv), F32),
                   jax.ShapeDtypeStruct((n_seq, hist_rows, d_conv), F32)],
        scratch_shapes=[pltpu.VMEM((hist_rows + t_new + SUBLANES, d_conv), F32)],
        compiler_params=_cparams("arbitrary"),
        name="conv_sample",
    )(z, z, state, wdw, bdw.reshape(1, -1), gng.reshape(1, -1), gnb.reshape(1, -1))


def _qk_kernel(zq_ref, zkv_ref, zqi0_ref, zqi1_ref, zt_ref, cos_ref, sin_ref, cosi_ref, sini_ref,
               gq_ref, gk_ref, qhm_ref, qihm_ref, k_ref, kbf_ref, vt_ref, ki_ref, kibf_ref, *, n_heads):
    cos, sin = cos_ref[...], sin_ref[...]
    cosi, sini = cosi_ref[...], sini_ref[...]
    tm = cos.shape[0]
    lane = lax.broadcasted_iota(I32, (tm, LANES), 1)
    low_half = (lane % IDX_DIM) < (IDX_DIM // 2)

    def norm_rope(x, g):
        y = x * lax.rsqrt(jnp.mean(x * x, axis=-1, keepdims=True) + EPS) * g
        return y * cos + pltpu.roll(y, LANES // 2, 1) * sin

    def rope_idx(x):
        r = jnp.where(low_half, pltpu.roll(x, LANES - IDX_DIM // 2, 1), pltpu.roll(x, IDX_DIM // 2, 1))
        return x * cosi + r * sini

    gq, gk = gq_ref[...], gk_ref[...]
    for h in range(n_heads):
        qhm_ref[0, h] = norm_rope(zq_ref[:, h * LANES:(h + 1) * LANES], gq).astype(BF16)
    kv_w = N_KV_HEADS * LANES
    for g in range(N_KV_HEADS):
        kg = norm_rope(zkv_ref[:, g * LANES:(g + 1) * LANES], gk)
        k_ref[:, g * LANES:(g + 1) * LANES] = kg
        kbf_ref[:, g * LANES:(g + 1) * LANES] = kg.astype(BF16)
    vt_ref[0] = zkv_ref[:, kv_w:2 * kv_w].T.astype(BF16)
    half = (N_IDX_HEADS * IDX_DIM) // 2
    for j in range(N_IDX_HEADS // 2):
        src = zqi0_ref if j * LANES < half else zqi1_ref
        c0 = (j * LANES) % half
        y = rope_idx(src[:, c0:c0 + LANES])
        qihm_ref[0, 2 * j] = y[:, :IDX_DIM].astype(BF16)
        qihm_ref[0, 2 * j + 1] = y[:, IDX_DIM:].astype(BF16)
    yk = rope_idx(zt_ref[...])[:, :IDX_DIM]
    ki_ref[...] = yk
    kibf_ref[...] = yk.astype(BF16)


def _qk_epilogue(z, zt, tabs, gq, gk, n_seq, seq, n_heads):
    m = z.shape[0]
    tm = QB
    head_dim = LANES
    cos, sin, cosi, sini = tabs
    tab_tiles = cos.shape[0] // tm
    kv_w = N_KV_HEADS * head_dim
    d_attn = n_heads * head_dim
    d_conv = d_attn
    q_cb = (2 * d_conv) // d_attn
    kv_cb = (2 * d_conv + d_attn) // (2 * kv_w)
    qi_w = (N_IDX_HEADS * IDX_DIM) // 2
    qi_cb = (2 * d_conv + d_attn + 2 * kv_w) // qi_w
    tiles_per_seq = seq // tm
    tab = pl.BlockSpec((tm, LANES), lambda i: (i % tab_tiles, 0))
    vec = pl.BlockSpec((1, LANES), lambda i: (0, 0))
    return pl.pallas_call(
        functools.partial(_qk_kernel, n_heads=n_heads),
        grid=(m // tm,),
        in_specs=[pl.BlockSpec((tm, d_attn), lambda i: (i, q_cb)),
                  pl.BlockSpec((tm, 2 * kv_w), lambda i: (i, kv_cb)),
                  pl.BlockSpec((tm, qi_w), lambda i: (i, qi_cb)),
                  pl.BlockSpec((tm, qi_w), lambda i: (i, qi_cb + 1)),
                  pl.BlockSpec((tm, LANES), lambda i: (i, 0)),
                  tab, tab, tab, tab, vec, vec],
        out_specs=[pl.BlockSpec((1, n_heads, tm, head_dim), lambda i: (i, 0, 0, 0)),
                   pl.BlockSpec((1, N_IDX_HEADS, tm, IDX_DIM), lambda i: (i, 0, 0, 0)),
                   pl.BlockSpec((tm, kv_w), lambda i: (i, 0)),
                   pl.BlockSpec((tm, kv_w), lambda i: (i, 0)),
                   pl.BlockSpec((1, kv_w, tm), lambda i: (i // tiles_per_seq, 0, i % tiles_per_seq)),
                   pl.BlockSpec((tm, IDX_DIM), lambda i: (i, 0)),
                   pl.BlockSpec((tm, IDX_DIM), lambda i: (i, 0))],
        out_shape=[jax.ShapeDtypeStruct((m // tm, n_heads, tm, head_dim), BF16),
                   jax.ShapeDtypeStruct((m // tm, N_IDX_HEADS, tm, IDX_DIM), BF16),
                   jax.ShapeDtypeStruct((m, kv_w), F32),
                   jax.ShapeDtypeStruct((m, kv_w), BF16),
                   jax.ShapeDtypeStruct((n_seq, kv_w, seq), BF16),
                   jax.ShapeDtypeStruct((m, IDX_DIM), F32),
                   jax.ShapeDtypeStruct((m, IDX_DIM), BF16)],
        compiler_params=_cparams("arbitrary"),
        name="qk_epilogue",
    )(z, z, z, z, zt, cos, sin, cosi, sini, gq.reshape(1, -1), gk.reshape(1, -1))


def _kth_largest_key(lo, hi, count_ge, n_sel):
    n_bits = jnp.max(32 - lax.clz(hi - lo))

    def step(it, thr):
        cand = thr + lax.shift_left(jnp.int32(1), n_bits - 1 - it)
        keep = count_ge(cand) >= n_sel
        return jnp.where(cand > thr, jnp.where(keep, cand, thr), thr)

    return lax.fori_loop(0, n_bits, step, lo)


def _pipelined_pairs(n_pairs, produce, consume):
    produce(0, 0)

    def body(p, carry):
        produce(2 * p + 1, 1)
        consume(2 * p, 0)
        produce(2 * p + 2, 0)
        consume(2 * p + 1, 1)
        return carry

    lax.fori_loop(0, n_pairs - 1, body, 0)
    last = 2 * (n_pairs - 1)
    produce(last + 1, 1)
    consume(last, 0)
    consume(last + 1, 1)


KT = 2 * QB


def _attn_prompt_kernel(qi_ref, w_ref, ki_ref, q_ref, k_ref, vt_ref, o_ref,
                        key_ref, m_ref, l_ref, acc_ref, cut_ref, lg_ref, s_ref, gmax_ref, *, n_sel, c_exp, n_heads,
                        idx_bits):
    i = pl.program_id(1)
    n_kt = lax.div(i * QB + QB + KT - 1, KT)
    row_iota = lax.broadcasted_iota(I32, (KT, QB), 0)
    w_all = w_ref[0] * IDX_SCALE
    heads_per_kv = n_heads // N_KV_HEADS
    hc = 4

    n_pairs = lax.shift_right_logical(n_kt + 1, 1)

    def score_products(kt, slot):
        ki_t = ki_ref[0, pl.ds(pl.multiple_of(kt * KT, KT), KT), :]
        for h0 in range(0, N_IDX_HEADS, hc):
            s_ref[slot, :, h0 * QB:(h0 + hc) * QB] = lax.dot_general(
                ki_t, qi_ref[0, h0:h0 + hc].reshape(hc * QB, IDX_DIM), _CONTRACT_LAST,
                preferred_element_type=F32)

    def score_keys(kt, slot):
        for half in range(KT // QB):
            rs = slice(half * QB, (half + 1) * QB)
            ks = pl.multiple_of(kt * KT + half * QB, QB)
            acc = jnp.zeros((QB, QB), F32)
            for h in range(N_IDX_HEADS):
                acc = acc + jnp.maximum(s_ref[slot, rs, h * QB:(h + 1) * QB], 0.0) * w_all[h:h + 1, :]
            acc = jnp.where(ks + lax.broadcasted_iota(I32, (QB, QB), 0) <=
                            i * QB + lax.broadcasted_iota(I32, (QB, QB), 1), acc, -jnp.inf)
            key = _order_key(acc)
            key_ref[pl.ds(ks, QB), :] = key
            gmax_ref[rs, :] = jnp.maximum(gmax_ref[rs, :], key)

    gmax_ref[...] = jnp.full(gmax_ref.shape, INT_MIN, I32)
    _pipelined_pairs(n_pairs, score_products, score_keys)

    def count(pred):
        def tile(kt, cnt):
            ks = pl.multiple_of(kt * KT, KT)
            hit = jnp.where(pred(key_ref[pl.ds(ks, KT), :], ks + row_iota), 1, 0)
            return cnt + jnp.sum(hit.reshape(KT // SUBLANES, SUBLANES, QB), axis=0)
        n_pairs = lax.shift_right_logical(n_kt, 1)
        cnt = lax.fori_loop(0, n_pairs, lambda p, cnt: tile(2 * p + 1, tile(2 * p, cnt)),
                            jnp.zeros((SUBLANES, QB), I32))
        cnt = lax.fori_loop(2 * n_pairs, n_kt, tile, cnt)
        return jnp.sum(cnt, axis=0, keepdims=True)

    gmax = gmax_ref[...]
    thr = _kth_largest_key(jnp.min(gmax, axis=0, keepdims=True), jnp.max(gmax, axis=0, keepdims=True),
                           lambda cand: count(lambda kk, pos: kk >= cand), n_sel)
    cnt_ge = count(lambda kk, pos: kk >= thr)
    need = n_sel - count(lambda kk, pos: kk > thr)
    cut_ref[...] = jnp.full((1, QB), INT_MAX, I32)

    @pl.when(jnp.max(cnt_ge) > n_sel)
    def _():
        def search_pos(it, p):
            cand = p + lax.shift_left(jnp.int32(1), idx_bits - 1 - it)
            c = count(lambda kk, pos: jnp.where(kk == thr, pos, INT_MAX) < cand)
            return jnp.where(c < need, cand, p)
        cut_ref[...] = lax.fori_loop(0, idx_bits, search_pos, jnp.zeros((1, QB), I32))

    few = thr <= KEY_NEG_INF
    thr_eff = jnp.where(few, KEY_NEG_INF + 1, thr)
    cut = jnp.where(few, INT_MAX, cut_ref[...])

    m_ref[...] = jnp.full(m_ref.shape, M_INIT, F32)
    l_ref[...] = jnp.zeros(l_ref.shape, F32)
    acc_ref[...] = jnp.zeros(acc_ref.shape, F32)

    def qk_products(kt, slot):
        ks = pl.multiple_of(kt * KT, KT)
        for g in range(N_KV_HEADS):
            k_t = k_ref[0, pl.ds(ks, KT), g * LANES:(g + 1) * LANES]
            q_g = q_ref[0, g * heads_per_kv:(g + 1) * heads_per_kv].reshape(heads_per_kv * QB, LANES)
            lg_ref[slot, :, g * heads_per_kv * QB:(g + 1) * heads_per_kv * QB] = lax.dot_general(
                k_t, q_g, _CONTRACT_LAST, preferred_element_type=F32)

    def attn_tile(kt, slot):
        ks = pl.multiple_of(kt * KT, KT)
        kk = key_ref[pl.ds(ks, KT), :]
        sel = (kk - jnp.where(ks + row_iota > cut, 1, 0)) >= thr_eff
        m_old, l_old = m_ref[...], l_ref[...]
        m_parts, l_parts = [], []
        for g in range(N_KV_HEADS):
            v_t = vt_ref[0, g * LANES:(g + 1) * LANES, pl.ds(ks, KT)]
            for hh in range(heads_per_kv):
                cs = slice((g * heads_per_kv + hh) * QB, (g * heads_per_kv + hh + 1) * QB)
                raw = jnp.where(sel, lg_ref[slot, :, cs], NEG_BIG)
                m_new = jnp.maximum(m_old[:, cs], jnp.max(raw, axis=0, keepdims=True))
                alpha = jnp.exp2((m_old[:, cs] - m_new) * c_exp)
                p = jnp.exp2((raw - m_new) * c_exp)
                m_parts.append(m_new)
                l_parts.append(alpha * l_old[:, cs] + jnp.sum(p, axis=0, keepdims=True))
                acc_ref[:, cs] = alpha * acc_ref[:, cs] + jnp.dot(v_t, p.astype(BF16),
                                                                   preferred_element_type=F32)
        m_ref[...] = jnp.concatenate(m_parts, axis=1)
        l_ref[...] = jnp.concatenate(l_parts, axis=1)

    _pipelined_pairs(n_pairs, qk_products, attn_tile)
    for h in range(n_heads):
        cs = slice(h * QB, (h + 1) * QB)
        o_ref[:, cs] = (acc_ref[:, cs] / l_ref[:, cs]).T


def _attn_prompt(qi_hm, w_hm, ki_bf, q_hm, k_bf, vt_bf, n_seq, seq, n_sel):
    n_heads = q_hm.shape[1]
    head_dim = q_hm.shape[3]
    nblk = seq // QB
    kv_w = k_bf.shape[-1]
    assert n_sel <= KT and seq % (2 * KT) == 0
    blk = lambda b, i: (b * nblk + i, 0, 0, 0)
    return pl.pallas_call(
        functools.partial(_attn_prompt_kernel, n_sel=n_sel, c_exp=head_dim ** -0.5 * LOG2_E, n_heads=n_heads,
                          idx_bits=int(seq).bit_length()),
        grid=(n_seq, nblk),
        in_specs=[pl.BlockSpec((1, N_IDX_HEADS, QB, IDX_DIM), blk),
                  pl.BlockSpec((1, N_IDX_HEADS, QB), lambda b, i: (b * nblk + i, 0, 0)),
                  pl.BlockSpec((1, seq, IDX_DIM), lambda b, i: (b, 0, 0)),
                  pl.BlockSpec((1, n_heads, QB, head_dim), blk),
                  pl.BlockSpec((1, seq, kv_w), lambda b, i: (b, 0, 0)),
                  pl.BlockSpec((1, kv_w, seq), lambda b, i: (b, 0, 0))],
        out_specs=pl.BlockSpec((QB, n_heads * head_dim), lambda b, i: (b * nblk + i, 0)),
        out_shape=jax.ShapeDtypeStruct((n_seq * seq, n_heads * head_dim), F32),
        scratch_shapes=[pltpu.VMEM((seq, QB), I32),
                        pltpu.VMEM((1, n_heads * QB), F32),
                        pltpu.VMEM((1, n_heads * QB), F32),
                        pltpu.VMEM((head_dim, n_heads * QB), F32),
                        pltpu.VMEM((1, QB), I32),
                        pltpu.VMEM((2, KT, n_heads * QB), F32),
                        pltpu.VMEM((2, KT, N_IDX_HEADS * QB), F32),
                        pltpu.VMEM((KT, QB), I32)],
        compiler_params=_cparams("arbitrary", "arbitrary"),
        name="attn_prompt",
    )(qi_hm, w_hm, ki_bf.reshape(n_seq, seq, IDX_DIM), q_hm, k_bf.reshape(n_seq, seq, kv_w), vt_bf)


SCORE_PAGES = 32
ATTN_PAGES = 16
ATTN_SEQS = 2


def _sample_score_kernel(pt_ref, qi_ref, w_ref, kin_ref, *refs, t_new):
    pages, (sp_ref, sn_ref) = refs[:SCORE_PAGES], refs[SCORE_PAGES:]
    page = pages[0].shape[2]
    qi = qi_ref[0]
    rows = qi.shape[0]
    wb = jnp.broadcast_to(w_ref[0] * IDX_SCALE, (rows, page))

    def score(keys_t_bf):
        s = jnp.dot(qi, keys_t_bf, preferred_element_type=F32)
        r = jnp.maximum(s, 0.0) * wb
        return jnp.sum(r.reshape(N_IDX_HEADS, t_new, page), axis=0)

    for j in range(SCORE_PAGES):
        sp_ref[0, :, j * page:(j + 1) * page] = score(pages[j][0].astype(BF16))

    @pl.when(pl.program_id(1) == 0)
    def _():
        sn = score(kin_ref[0])
        s_idx = lax.broadcasted_iota(I32, (t_new, page), 1)
        t_idx = lax.broadcasted_iota(I32, (t_new, page), 0)
        sn_ref[0] = jnp.where(s_idx <= t_idx, sn, -jnp.inf)


def _sample_scores(page_table, qi_s, w_s, ki_new_t, cache_kidx_t, t_new):
    n_seq, n_pages = page_table.shape
    page = cache_kidx_t.shape[2]
    rows = qi_s.shape[1]
    page_specs = [pl.BlockSpec((1, IDX_DIM, page),
                               functools.partial(lambda b, c, pt, j: (pt[b, c * SCORE_PAGES + j], 0, 0), j=j))
                  for j in range(SCORE_PAGES)]
    grid_spec = pltpu.PrefetchScalarGridSpec(
        num_scalar_prefetch=1,
        grid=(n_seq, n_pages // SCORE_PAGES),
        in_specs=[pl.BlockSpec((1, rows, IDX_DIM), lambda b, c, pt: (b, 0, 0)),
                  pl.BlockSpec((1, rows, 1), lambda b, c, pt: (b, 0, 0)),
                  pl.BlockSpec((1, IDX_DIM, page), lambda b, c, pt: (b, 0, 0))] + page_specs,
        out_specs=[pl.BlockSpec((1, t_new, SCORE_PAGES * page), lambda b, c, pt: (b, 0, c)),
                   pl.BlockSpec((1, t_new, page), lambda b, c, pt: (b, 0, 0))],
    )
    return pl.pallas_call(
        functools.partial(_sample_score_kernel, t_new=t_new),
        grid_spec=grid_spec,
        out_shape=[jax.ShapeDtypeStruct((n_seq, t_new, n_pages * page), F32),
                   jax.ShapeDtypeStruct((n_seq, t_new, page), F32)],
        compiler_params=_cparams("arbitrary", "arbitrary"),
        name="sample_scores",
    )(page_table, qi_s, w_s, ki_new_t, *([cache_kidx_t] * SCORE_PAGES))


def _sample_thresh_kernel(sp_ref, sn_ref, ex_ref, mask_ref, key_ref, cut_ref, *, n_sel, n_past, idx_bits):
    rows = sp_ref.shape[0]
    n_tiles = n_past // LANES + 1
    key_ref[:, 0:n_past] = _order_key(sp_ref[...])
    key_ref[:, n_past:n_past + LANES] = _order_key(sn_ref[...])
    lane = lax.broadcasted_iota(I32, (rows, LANES), 1)

    def count(pred):
        def body(j, cnt):
            c0 = pl.multiple_of(j * LANES, LANES)
            return cnt + jnp.where(pred(key_ref[:, pl.ds(c0, LANES)], c0 + lane), 1, 0)
        cnt = lax.fori_loop(0, n_tiles, body, jnp.zeros((rows, LANES), I32), unroll=8)
        return jnp.broadcast_to(jnp.sum(cnt, axis=1, keepdims=True), (rows, LANES))

    def class_max(j, carry):
        c0 = pl.multiple_of(j * 2 * LANES, 2 * LANES)
        return (jnp.maximum(carry[0], key_ref[:, pl.ds(c0, LANES)]),
                jnp.maximum(carry[1], key_ref[:, pl.ds(c0 + LANES, LANES)]))

    floor = jnp.full((rows, LANES), INT_MIN, I32)
    even, odd = lax.fori_loop(0, n_tiles // 2, class_max, (floor, floor), unroll=8)
    if n_tiles % 2:
        even = jnp.maximum(even, key_ref[:, (n_tiles - 1) * LANES:n_tiles * LANES])
    lo = jnp.broadcast_to(jnp.min(jnp.minimum(even, odd), axis=1, keepdims=True), (rows, LANES))
    hi = jnp.broadcast_to(jnp.max(jnp.maximum(even, odd), axis=1, keepdims=True), (rows, LANES))
    thr = _kth_largest_key(lo, hi, lambda cand: count(lambda kk, pos: kk >= cand), n_sel)
    cnt_ge = count(lambda kk, pos: kk >= thr)
    need = n_sel - count(lambda kk, pos: kk > thr)
    cut_ref[...] = jnp.full((rows, LANES), INT_MAX, I32)

    @pl.when(jnp.max(cnt_ge) > n_sel)
    def _():
        def search_pos(it, p):
            cand = p + lax.shift_left(jnp.int32(1), idx_bits - 1 - it)
            c = count(lambda kk, pos: jnp.where(kk == thr, pos, INT_MAX) < cand)
            return jnp.where(c < need, cand, p)
        cut_ref[...] = lax.fori_loop(0, idx_bits, search_pos, jnp.zeros((rows, LANES), I32))

    few = thr <= KEY_NEG_INF
    thr_eff = jnp.where(few, KEY_NEG_INF + 1, thr)
    cut = jnp.where(few, INT_MAX, cut_ref[...])

    ex = ex_ref[...]
    width = ex.shape[1]

    def emit(j, carry):
        c0 = pl.multiple_of(j * LANES, LANES)
        hit = jnp.where((key_ref[:, pl.ds(c0, LANES)] - jnp.where(c0 + lane > cut, 1, 0)) >= thr_eff, 1.0, 0.0)
        mask_ref[:, pl.ds(pl.multiple_of(j * width, width), width)] = jnp.dot(
            hit.astype(BF16), ex, preferred_element_type=F32)
        return carry

    lax.fori_loop(0, n_tiles, emit, 0, unroll=4)


def _sample_thresh(sp, sn, n_sel, rows_per_step=64):
    m, n_past = sp.shape
    assert n_sel <= 2 * LANES and n_past >= LANES
    rows_per_step = min(rows_per_step, m)
    tok = jnp.arange(LANES, dtype=I32)[:, None]
    col = jnp.arange(LANES * N_KV_HEADS, dtype=I32)[None, :]
    expand = (col // N_KV_HEADS == tok).astype(BF16)
    width = (n_past + LANES) * N_KV_HEADS
    return pl.pallas_call(
        functools.partial(_sample_thresh_kernel, n_sel=n_sel, n_past=n_past,
                          idx_bits=int(n_past + LANES).bit_length()),
        grid=(m // rows_per_step,),
        in_specs=[pl.BlockSpec((rows_per_step, n_past), lambda r: (r, 0)),
                  pl.BlockSpec((rows_per_step, LANES), lambda r: (r, 0)),
                  pl.BlockSpec(expand.shape, lambda r: (0, 0))],
        out_specs=pl.BlockSpec((rows_per_step, width), lambda r: (r, 0)),
        out_shape=jax.ShapeDtypeStruct((m, width), F32),
        scratch_shapes=[pltpu.VMEM((rows_per_step, n_past + LANES), I32),
                        pltpu.VMEM((rows_per_step, LANES), I32)],
        compiler_params=_cparams("arbitrary"),
        name="sample_thresh",
    )(sp, sn, expand)


def _sample_attn_kernel(pt_ref, q_ref, mask_ref, maskn_ref, kn_ref, vn_ref, *refs, c_exp, group_rows):
    n_pg = ATTN_SEQS * ATTN_PAGES
    k_pages, v_pages = refs[:n_pg], refs[n_pg:2 * n_pg]
    o_ref, m_ref, l_ref, acc_ref = refs[2 * n_pg:]
    c = pl.program_id(1)
    rows = q_ref.shape[1]
    width = maskn_ref.shape[2]
    reps = rows // mask_ref.shape[1]
    own_head = jnp.where(lax.broadcasted_iota(I32, (rows, width), 1) % N_KV_HEADS ==
                         lax.broadcasted_iota(I32, (rows, width), 0) // group_rows, 1.0, 0.0)

    @pl.when(c == 0)
    def _():
        m_ref[...] = jnp.full(m_ref.shape, M_INIT, F32)
        l_ref[...] = jnp.zeros(l_ref.shape, F32)
        acc_ref[...] = jnp.zeros(acc_ref.shape, F32)

    def select(flags):
        return jnp.concatenate([flags] * reps, axis=0) * own_head > 0.5

    def attend(s, tiles):
        q = q_ref[s]
        raws = [jnp.where(sel, lax.dot_general(q, k, _CONTRACT_LAST, preferred_element_type=F32), NEG_BIG)
                for sel, k, _ in tiles]
        m_old = m_ref[s]
        m_new = jnp.maximum(m_old, jnp.max(functools.reduce(jnp.maximum, raws), axis=1, keepdims=True))
        alpha = jnp.exp2((m_old - m_new) * c_exp)
        ps = [jnp.exp2((raw - m_new) * c_exp) for raw in raws]
        l_ref[s] = alpha * l_ref[s] + jnp.sum(functools.reduce(jnp.add, ps), axis=1, keepdims=True)
        m_ref[s] = m_new
        pv = functools.reduce(jnp.add, [jnp.dot(p.astype(BF16), v, preferred_element_type=F32)
                                        for p, (_, _, v) in zip(ps, tiles)])
        acc_ref[s] = alpha * acc_ref[s] + pv

    for s in range(ATTN_SEQS):
        attend(s, [(select(mask_ref[s, :, j * width:(j + 1) * width]),
                    k_pages[s * ATTN_PAGES + j][0].astype(BF16), v_pages[s * ATTN_PAGES + j][0].astype(BF16))
                   for j in range(ATTN_PAGES)])

    @pl.when(c == pl.num_programs(1) - 1)
    def _():
        for s in range(ATTN_SEQS):
            attend(s, [(select(maskn_ref[s]), kn_ref[s], vn_ref[s])])
            o_ref[s] = acc_ref[s] / l_ref[s]


def _sample_attn(page_table, q_s, mask, k_new, v_new, cache_k, cache_v, t_new):
    n_seq, n_pages = page_table.shape
    page_rows, head_dim = cache_k.shape[1], cache_k.shape[2]
    rows = q_s.shape[1]
    page_specs = [pl.BlockSpec((1, page_rows, head_dim),
                               functools.partial(lambda b, c, pt, s, j: (pt[b * ATTN_SEQS + s, c * ATTN_PAGES + j], 0, 0),
                                                 s=s, j=j))
                  for s in range(ATTN_SEQS) for j in range(ATTN_PAGES)]
    per_seq = lambda shape: pl.BlockSpec((ATTN_SEQS,) + shape, lambda b, c, pt: (b, 0, 0))
    grid_spec = pltpu.PrefetchScalarGridSpec(
        num_scalar_prefetch=1,
        grid=(n_seq // ATTN_SEQS, n_pages // ATTN_PAGES),
        in_specs=[per_seq((rows, head_dim)),
                  pl.BlockSpec((ATTN_SEQS, t_new, ATTN_PAGES * page_rows), lambda b, c, pt: (b, 0, c)),
                  pl.BlockSpec((ATTN_SEQS, t_new, page_rows), lambda b, c, pt: (b, 0, n_pages)),
                  per_seq((page_rows, head_dim)), per_seq((page_rows, head_dim))] + page_specs + page_specs,
        out_specs=per_seq((rows, head_dim)),
        scratch_shapes=[pltpu.VMEM((ATTN_SEQS, rows, 1), F32),
                        pltpu.VMEM((ATTN_SEQS, rows, 1), F32),
                        pltpu.VMEM((ATTN_SEQS, rows, head_dim), F32)],
    )
    return pl.pallas_call(
        functools.partial(_sample_attn_kernel, c_exp=head_dim ** -0.5 * LOG2_E, group_rows=rows // N_KV_HEADS),
        grid_spec=grid_spec,
        out_shape=jax.ShapeDtypeStruct((n_seq, rows, head_dim), F32),
        compiler_params=_cparams("arbitrary", "arbitrary"),
        name="sample_attn",
    )(page_table, q_s, mask, mask, k_new, v_new,
      *([cache_k] * (ATTN_SEQS * ATTN_PAGES)), *([cache_v] * (ATTN_SEQS * ATTN_PAGES)))


def _merge_kernel(x_ref, gt_ref, co_ref, ao_ref, go_ref, wc_ref, wa_ref, o_ref, on_ref, *, n_heads):
    @pl.when(pl.program_id(1) == 0)
    def _():
        g = go_ref[...]
        for h in range(n_heads):
            cs = slice(h * LANES, (h + 1) * LANES)
            o = ao_ref[:, cs]
            on_ref[:, cs] = (o * lax.rsqrt(jnp.mean(o * o, axis=-1, keepdims=True) + EPS) * g).astype(BF16)

    y = jnp.dot(co_ref[...].astype(BF16), wc_ref[...].astype(BF16), preferred_element_type=F32)
    y = y + jnp.dot(on_ref[...], wa_ref[...].astype(BF16), preferred_element_type=F32)
    o_ref[...] = x_ref[...] + gt_ref[0] * y


def _merge(x, gt, conv_o, attn_o, g_o, w_out, tm, tn=512):
    m, d = x.shape
    g, r, _ = gt.shape
    d_conv = conv_o.shape[1]
    d_attn = attn_o.shape[1]
    tiles_per_group = (m // tm) // g
    rb = d_conv // d_attn
    return pl.pallas_call(
        functools.partial(_merge_kernel, n_heads=d_attn // LANES),
        grid=(m // tm, d // tn),
        in_specs=[pl.BlockSpec((tm, tn), lambda i, j: (i, j)),
                  pl.BlockSpec((1, r, tn), lambda i, j: (i // tiles_per_group, 0, j)),
                  pl.BlockSpec((tm, d_conv), lambda i, j: (i, 0)),
                  pl.BlockSpec((tm, d_attn), lambda i, j: (i, 0)),
                  pl.BlockSpec((1, LANES), lambda i, j: (0, 0)),
                  pl.BlockSpec((d_conv, tn), lambda i, j: (0, j)),
                  pl.BlockSpec((d_attn, tn), lambda i, j: (rb, j))],
        out_specs=pl.BlockSpec((tm, tn), lambda i, j: (i, j)),
        out_shape=jax.ShapeDtypeStruct((m, d), F32),
        scratch_shapes=[pltpu.VMEM((tm, d_attn), BF16)],
        compiler_params=_cparams("arbitrary", "arbitrary"),
        name="merge",
    )(x, gt, conv_o, attn_o, g_o.reshape(1, -1), w_out, w_out)


def _ffn_act(cur_g, p1_g, p2_g, cur_v, p1_v, p2_v, wg, wv, bg, bv):
    gate = p2_g * wg[0:1, :] + p1_g * wg[1:2, :] + cur_g * wg[2:3, :] + bg
    val = p2_v * wv[0:1, :] + p1_v * wv[1:2, :] + cur_v * wv[2:3, :] + bv
    return (_silu(gate) * val).astype(BF16)


def _ffn_finish(f, act, wd_ref, x_ref, gt_ref, o_ref, acc_ref):
    @pl.when(f == 0)
    def _():
        acc_ref[...] = jnp.zeros(acc_ref.shape, F32)

    acc_ref[...] += jnp.dot(act, wd_ref[...].astype(BF16), preferred_element_type=F32)

    @pl.when(f == pl.num_programs(1) - 1)
    def _():
        o_ref[...] = x_ref[...] + gt_ref[0] * acc_ref[...]


def _to_bf16_kernel(x_ref, o_ref):
    o_ref[...] = x_ref[...].astype(BF16)


def _to_bf16(w, cols_per_step=512):
    r, c = w.shape
    return pl.pallas_call(
        _to_bf16_kernel,
        grid=(c // cols_per_step,),
        in_specs=[pl.BlockSpec((r, cols_per_step), lambda j: (0, j))],
        out_specs=pl.BlockSpec((r, cols_per_step), lambda j: (0, j)),
        out_shape=jax.ShapeDtypeStruct((r, c), BF16),
        compiler_params=_cparams("arbitrary"),
        name="to_bf16",
    )(w)


FFN_HALO = 16
FFN_PARTS = 2


def _ffn_prompt_kernel(x_ref, xh_ref, g_ref, sc_ref, sh_ref, wug_ref, wuv_ref, wg_ref, wv_ref, bg_ref, bv_ref,
                       wd_ref, gt_ref, o_ref, tg_ref, tv_ref, h_ref, hist_g, hist_v, act_ref, *, tm, tiles_per_seq):
    i, f = pl.program_id(0), pl.program_id(1)

    def normmod(x):
        y = x * lax.rsqrt(jnp.mean(x * x, axis=-1, keepdims=True) + EPS) * g_ref[...]
        return (y * (1.0 + sc_ref[0]) + sh_ref[0]).astype(BF16)

    @pl.when(f == 0)
    def _():
        h_ref[0:FFN_HALO, :] = normmod(xh_ref[...])
        h_ref[FFN_HALO:FFN_HALO + tm, :] = normmod(x_ref[...])
        o_ref[...] = jnp.zeros(o_ref.shape, F32)

    n_parts, _, pw = hist_g.shape
    for s in range(n_parts):
        cs = slice(s * pw, (s + 1) * pw)
        hist_g[s] = jnp.dot(h_ref[...], wug_ref[:, cs], preferred_element_type=F32)
        hist_v[s] = jnp.dot(h_ref[...], wuv_ref[:, cs], preferred_element_type=F32)

    @pl.when((i % tiles_per_seq) == 0)
    def _():
        hist_g[:, 0:FFN_HALO, :] = jnp.zeros((n_parts, FFN_HALO, pw), F32)
        hist_v[:, 0:FFN_HALO, :] = jnp.zeros((n_parts, FFN_HALO, pw), F32)

    chunk = 64
    down = None
    for s in range(n_parts):
        cs = slice(s * pw, (s + 1) * pw)
        wg, wv, bg, bv = wg_ref[:, cs], wv_ref[:, cs], bg_ref[:, cs], bv_ref[:, cs]
        for r0 in range(0, tm, chunk):
            rows = lambda hist, k: hist[s, FFN_HALO + r0 - k:FFN_HALO + r0 - k + chunk, :]
            act_ref[s, r0:r0 + chunk, :] = _ffn_act(rows(hist_g, 0), rows(hist_g, 1), rows(hist_g, 2),
                                                    rows(hist_v, 0), rows(hist_v, 1), rows(hist_v, 2),
                                                    wg, wv, bg, bv)
        tg_ref[0, :, cs] = hist_g[s, tm + FFN_HALO - SUBLANES:tm + FFN_HALO, :]
        tv_ref[0, :, cs] = hist_v[s, tm + FFN_HALO - SUBLANES:tm + FFN_HALO, :]
        part = jnp.dot(act_ref[s], wd_ref[cs, :], preferred_element_type=F32)
        down = part if down is None else down + part
    o_ref[...] += down

    @pl.when(f == pl.num_programs(1) - 1)
    def _():
        o_ref[...] = x_ref[...] + gt_ref[0] * o_ref[...]


def _ffn_prompt(x1, gain, sc, sh, gt, w_up_bf, w_dw, b_dw, w_down_bf, seq, tm, tf=512):
    m, d = x1.shape
    d_ff = w_down_bf.shape[0]
    nf = d_ff // tf
    tiles_per_seq = seq // tm
    fw = w_dw.shape[0]
    halo_idx = lambda i: jnp.maximum(i * (tm // FFN_HALO) - 1, 0)
    mod = pl.BlockSpec((1, 1, d), lambda i, f: (i // tiles_per_seq, 0, 0))
    tail = pl.BlockSpec((1, SUBLANES, tf), lambda i, f: (i, 0, f))
    return pl.pallas_call(
        functools.partial(_ffn_prompt_kernel, tm=tm, tiles_per_seq=tiles_per_seq),
        grid=(m // tm, nf),
        in_specs=[pl.BlockSpec((tm, d), lambda i, f: (i, 0)),
                  pl.BlockSpec((FFN_HALO, d), lambda i, f: (halo_idx(i), 0)),
                  pl.BlockSpec((1, d), lambda i, f: (0, 0)),
                  mod, mod,
                  pl.BlockSpec((d, tf), lambda i, f: (0, f)),
                  pl.BlockSpec((d, tf), lambda i, f: (0, f + nf)),
                  pl.BlockSpec((fw, tf), lambda i, f: (0, f)),
                  pl.BlockSpec((fw, tf), lambda i, f: (0, f + nf)),
                  pl.BlockSpec((1, tf), lambda i, f: (0, f)),
                  pl.BlockSpec((1, tf), lambda i, f: (0, f + nf)),
                  pl.BlockSpec((tf, d), lambda i, f: (f, 0)),
                  mod],
        out_specs=[pl.BlockSpec((tm, d), lambda i, f: (i, 0)), tail, tail],
        out_shape=[jax.ShapeDtypeStruct((m, d), F32),
                   jax.ShapeDtypeStruct((m // tm, SUBLANES, d_ff), F32),
                   jax.ShapeDtypeStruct((m // tm, SUBLANES, d_ff), F32)],
        scratch_shapes=[pltpu.VMEM((FFN_HALO + tm, d), BF16),
                        pltpu.VMEM((FFN_PARTS, FFN_HALO + tm, tf // FFN_PARTS), F32),
                        pltpu.VMEM((FFN_PARTS, FFN_HALO + tm, tf // FFN_PARTS), F32),
                        pltpu.VMEM((FFN_PARTS, tm, tf // FFN_PARTS), BF16)],
        compiler_params=_cparams("arbitrary", "arbitrary"),
        name="ffn_prompt",
    )(x1, x1, gain.reshape(1, d), sc, sh, w_up_bf, w_up_bf, w_dw, w_dw, b_dw.reshape(1, -1), b_dw.reshape(1, -1),
      w_down_bf, gt)


def _ffn_sample_kernel(cg_ref, cv_ref, p1g_ref, p1v_ref, p2g_ref, p2v_ref, wg_ref, wv_ref, bg_ref, bv_ref,
                       wd_ref, x_ref, gt_ref, o_ref, acc_ref):
    act = _ffn_act(cg_ref[...], p1g_ref[...], p2g_ref[...], cv_ref[...], p1v_ref[...], p2v_ref[...],
                   wg_ref[...], wv_ref[...], bg_ref[...], bv_ref[...])
    _ffn_finish(pl.program_id(1), act, wd_ref, x_ref, gt_ref, o_ref, acc_ref)


def _ffn_sample(cur, prev1, prev2, x1, gt, w_dw, b_dw, w_down, tf=512):
    m, d = x1.shape
    d_ff = w_down.shape[0]
    nf = d_ff // tf
    fw = w_dw.shape[0]
    lo = pl.BlockSpec((m, tf), lambda i, f: (0, f))
    hi = pl.BlockSpec((m, tf), lambda i, f: (0, f + nf))
    return pl.pallas_call(
        _ffn_sample_kernel,
        grid=(1, nf),
        in_specs=[lo, hi, lo, hi, lo, hi,
                  pl.BlockSpec((fw, tf), lambda i, f: (0, f)),
                  pl.BlockSpec((fw, tf), lambda i, f: (0, f + nf)),
                  pl.BlockSpec((1, tf), lambda i, f: (0, f)),
                  pl.BlockSpec((1, tf), lambda i, f: (0, f + nf)),
                  pl.BlockSpec((tf, d), lambda i, f: (f, 0)),
                  pl.BlockSpec((m, d), lambda i, f: (0, 0)),
                  pl.BlockSpec((1, m, d), lambda i, f: (0, 0, 0))],
        out_specs=pl.BlockSpec((m, d), lambda i, f: (0, 0)),
        out_shape=jax.ShapeDtypeStruct((m, d), F32),
        scratch_shapes=[pltpu.VMEM((m, d), F32)],
        compiler_params=_cparams("arbitrary", "arbitrary"),
        name="ffn_sample",
    )(cur, cur, prev1, prev1, prev2, prev2, w_dw, w_dw, b_dw.reshape(1, -1), b_dw.reshape(1, -1),
      w_down, x1, gt)


def _rope_tables(pos):
    def tab(dim):
        inv = jnp.power(ROPE_THETA, -jnp.arange(0, dim, 2, dtype=F32) / dim)
        ang = pos.astype(F32)[:, None] * inv[None, :]
        cos, sin = jnp.cos(ang), jnp.sin(ang)
        reps = LANES // dim
        return (jnp.tile(jnp.concatenate([cos, cos], axis=-1), (1, reps)),
                jnp.tile(jnp.concatenate([-sin, sin], axis=-1), (1, reps)))
    return tab(LANES) + tab(IDX_DIM)


def _in_proj(x2d, norm1, sc, sh, w_in_t, n_main, tm):
    z = _normmod_matmul(x2d, norm1, sc, sh, w_in_t, n_main, tm, 512, "in_proj", w_transposed=True)
    w_tail = jnp.pad(w_in_t[n_main:], ((0, LANES - (w_in_t.shape[0] - n_main)), (0, 0)))
    zt = _normmod_matmul(x2d, norm1, sc, sh, w_tail, LANES, tm, LANES, "in_proj_tail", w_transposed=True)
    return z, zt


def kernel(x_prompt, x_sample, cache_k, cache_v, cache_kidx, state_conv, state_ffn, page_table, c_prompt, c_sample, norm1, w_ada, b_ada, w_in, g_q, g_k, w_dw_a, b_dw_a, gn_g, gn_b, g_o, w_out, norm2, w_up, w_dw_f, b_dw_f, w_down):
    n_b, seq, d = x_prompt.shape
    n_s, t_new, _ = x_sample.shape
    depth = norm1.shape[0]
    assert depth == 1
    head_dim = g_q.shape[-1]
    assert head_dim == LANES
    d_conv = w_dw_a.shape[-1]
    d_attn = w_out.shape[1] - d_conv
    n_heads = d_attn // head_dim
    kv_w = N_KV_HEADS * head_dim
    n_main = 2 * d_conv + d_attn + 2 * kv_w + N_IDX_HEADS * IDX_DIM
    n_pool, page = cache_k.shape[1], cache_k.shape[2]
    n_pages = page_table.shape[1]
    n_past = n_pages * page
    mp, ms = n_b * seq, n_s * t_new
    heads_per_kv = n_heads // N_KV_HEADS

    n_c = n_b + n_s
    pad_c = (-n_c) % SUBLANES
    c_all = jnp.concatenate([c_prompt, c_sample, jnp.zeros((pad_c, d), F32)], axis=0)
    mods = _ada(c_all, w_ada[0], b_ada[0])
    mp6 = mods[:n_b].reshape(n_b, 6, 1, d)
    sh1p, sc1p, gt1p, sh2p, sc2p, gt2p = [mp6[:, k] for k in range(6)]
    ms6 = jnp.repeat(mods[n_b:n_c].reshape(n_s, 6, d), t_new, axis=0)
    sh1s, sc1s, gt1s, sh2s, sc2s, gt2s = [ms6[:, k][None] for k in range(6)]

    w_in_t = jnp.swapaxes(w_in[0], 0, 1)

    xp = x_prompt.reshape(mp, d)
    tm_p = 1024 if seq % 1024 == 0 else QB
    z, zt = _in_proj(xp, norm1[0], sc1p, sh1p, w_in_t, n_main, tm_p)
    conv_o, new_conv_p = _conv_prompt(z, n_b, seq, d_conv, w_dw_a[0], b_dw_a[0], gn_g[0], gn_b[0],
                                      tm=256 if seq % 256 == 0 else QB)
    tabs_p = _rope_tables(jnp.arange(seq, dtype=I32))
    q_hm, qi_hm, k_p, k_bf, vt_bf, ki_p, ki_bf = _qk_epilogue(z, zt, tabs_p, g_q[0], g_k[0], n_b, seq, n_heads)
    w_hm = zt[:, IDX_DIM:IDX_DIM + N_IDX_HEADS].reshape(mp // QB, QB, N_IDX_HEADS).transpose(0, 2, 1)
    n_sel_p = min(TOPK_MAX, seq // 4)
    attn_p = _attn_prompt(qi_hm, w_hm, ki_bf, q_hm, k_bf, vt_bf, n_b, seq, n_sel_p)
    x1p = _merge(xp, gt1p, conv_o, attn_p, g_o[0], w_out[0],tm_p)
    w_up_bf, w_down_bf = _to_bf16(w_up[0]), _to_bf16(w_down[0])
    tm_f = 512 if seq % 512 == 0 else QB
    y_p, u_tail_g, u_tail_v = _ffn_prompt(x1p, norm2[0], sc2p, sh2p, gt2p, w_up_bf, w_dw_f[0], b_dw_f[0],
                                          w_down_bf, seq, tm_f)
    v_p = z[:, 2 * d_conv + d_attn + kv_w:2 * d_conv + d_attn + 2 * kv_w]
    fw = w_dw_f.shape[1]
    last_tile = (jnp.arange(n_b) + 1) * (seq // tm_f) - 1
    new_ffn_p = jnp.concatenate([u_tail_g[last_tile], u_tail_v[last_tile]], axis=-1)[:, SUBLANES - (fw - 1):]

    xs = x_sample.reshape(ms, d)
    zs, zts = _in_proj(xs, norm1[0], sc1s, sh1s, w_in_t, n_main, ms)
    conv_os, new_conv_s = _conv_sample(zs, state_conv[0], d_conv, w_dw_a[0], b_dw_a[0], gn_g[0], gn_b[0])
    pos_s = jnp.tile(n_past + jnp.arange(t_new, dtype=I32), ms // t_new)
    tabs_s = _rope_tables(pos_s)
    q_hm_s, qi_hm_s, k_s, k_bf_s, _, ki_s, ki_bf_s = _qk_epilogue(zs, zts, tabs_s, g_q[0], g_k[0], 1, ms, n_heads)
    v_s = zs[:, 2 * d_conv + d_attn + kv_w:2 * d_conv + d_attn + 2 * kv_w]
    seq_rows = lambda a, nh: (a.reshape(ms // QB, nh, QB // t_new, t_new, a.shape[-1])
                              .transpose(0, 2, 1, 3, 4).reshape(n_s, nh * t_new, a.shape[-1]))
    qi_s = seq_rows(qi_hm_s, N_IDX_HEADS)
    q_s = seq_rows(q_hm_s, n_heads)
    w_s = (zts[:, IDX_DIM:IDX_DIM + N_IDX_HEADS].reshape(n_s, t_new, N_IDX_HEADS)
           .transpose(0, 2, 1).reshape(n_s, N_IDX_HEADS * t_new, 1))
    ki_new_t = jnp.pad(ki_bf_s.reshape(n_s, t_new, IDX_DIM).transpose(0, 2, 1), ((0, 0), (0, 0), (0, page - t_new)))
    kv_new = lambda a: jnp.pad(a.reshape(n_s, t_new * N_KV_HEADS, head_dim),
                               ((0, 0), (0, (page - t_new) * N_KV_HEADS), (0, 0)))
    kidx_t = jnp.swapaxes(cache_kidx[0], 1, 2)
    sp, sn = _sample_scores(page_table, qi_s, w_s, ki_new_t, kidx_t, t_new)
    n_sel_s = min(TOPK_MAX, (n_past + t_new) // 4)
    mask = _sample_thresh(sp.reshape(ms, n_past), sn.reshape(ms, page), n_sel_s)
    o_s = _sample_attn(page_table, q_s, mask.reshape(n_s, t_new, -1), kv_new(k_bf_s), kv_new(v_s.astype(BF16)),
                       cache_k[0].reshape(n_pool, page * N_KV_HEADS, head_dim),
                       cache_v[0].reshape(n_pool, page * N_KV_HEADS, head_dim), t_new)
    attn_s = o_s.reshape(n_s, n_heads, t_new, head_dim).transpose(0, 2, 1, 3).reshape(ms, d_attn)
    x1s = _merge(xs, gt1s, conv_os, attn_s, g_o[0], w_out[0],ms)
    u_s = _normmod_matmul(x1s, norm2[0], sc2s, sh2s, w_up_bf, w_up.shape[-1], ms, 512, "up_proj")
    u_hist = jnp.concatenate([state_ffn[0], u_s.reshape(n_s, t_new, -1)], axis=1)
    shifted = lambda k: u_hist[:, k:k + t_new].reshape(ms, -1)
    y_s = _ffn_sample(shifted(2), shifted(1), shifted(0), x1s, gt2s, w_dw_f[0], b_dw_f[0], w_down_bf)
    new_ffn_s = u_hist[:, t_new:]

    return (y_p.reshape(n_b, seq, d), y_s.reshape(n_s, t_new, d),
            k_p.reshape(1, n_b, seq, N_KV_HEADS, head_dim), v_p.reshape(1, n_b, seq, N_KV_HEADS, head_dim),
            ki_p.reshape(1, n_b, seq, IDX_DIM), new_conv_p[None], new_ffn_p[None],
            k_s.reshape(1, n_s, t_new, N_KV_HEADS, head_dim), v_s.reshape(1, n_s, t_new, N_KV_HEADS, head_dim),
            ki_s.reshape(1, n_s, t_new, IDX_DIM), new_conv_s[None], new_ffn_s[None])
```

```python
import functools

import jax
import jax.numpy as jnp
from jax import lax
from jax.experimental import pallas as pl
from jax.experimental.pallas import tpu as pltpu

F32 = jnp.float32
BF16 = jnp.bfloat16
I32 = jnp.int32

CONV_GROUPS = 8
N_KV_HEADS = 2
N_IDX_HEADS = 16
IDX_DIM = 64
TOPK_MAX = 256
ROPE_THETA = 10000.0
EPS = 1e-6
IDX_SCALE = (IDX_DIM ** -0.5) * (N_IDX_HEADS ** -0.5)

LANES = 128
SUBLANES = 8
QB = 128
VMEM_LIMIT_BYTES = 56 * 1024 * 1024
NEG_BIG = -1e30
M_INIT = -1e29
LOG2_E = 1.4426950408889634
INT_MIN = -2 ** 31
INT_MAX = 2 ** 31 - 1
KEY_NEG_INF = -2139095041


def _cparams(*sem):
    return pltpu.CompilerParams(dimension_semantics=sem, vmem_limit_bytes=VMEM_LIMIT_BYTES)


def _silu(x):
    return x * jax.nn.sigmoid(x)


def _order_key(x):
    bits = pltpu.bitcast(x, I32)
    return bits ^ ((bits >> 31) & INT_MAX)


def _ada_kernel(c_ref, w_ref, b_ref, o_ref):
    s = _silu(c_ref[...]).astype(BF16)
    o_ref[...] = jnp.dot(s, w_ref[...].astype(BF16), preferred_element_type=F32) + b_ref[...]


def _ada(c_all, w_ada, b_ada, tn=1024):
    r, d = c_all.shape
    n = w_ada.shape[1]
    return pl.pallas_call(
        _ada_kernel,
        grid=(n // tn,),
        in_specs=[pl.BlockSpec((r, d), lambda j: (0, 0)),
                  pl.BlockSpec((d, tn), lambda j: (0, j)),
                  pl.BlockSpec((1, tn), lambda j: (0, j))],
        out_specs=pl.BlockSpec((r, tn), lambda j: (0, j)),
        out_shape=jax.ShapeDtypeStruct((r, n), F32),
        compiler_params=_cparams("arbitrary"),
        name="ada",
    )(c_all, w_ada, b_ada.reshape(1, n))


_CONTRACT_LAST = (((1,), (1,)), ((), ()))


NORM_CHUNK = 128


def _normmod_to(h_ref, dst0, x_ref, g_ref, sc_ref, sh_ref):
    rows = x_ref.shape[0]
    per_row = sc_ref.shape[1] != 1
    chunk = min(NORM_CHUNK, rows)
    for r0 in range(0, rows, chunk):
        x = x_ref[r0:r0 + chunk, :]
        sc = sc_ref[0, r0:r0 + chunk, :] if per_row else sc_ref[0]
        sh = sh_ref[0, r0:r0 + chunk, :] if per_row else sh_ref[0]
        y = x * lax.rsqrt(jnp.mean(x * x, axis=-1, keepdims=True) + EPS) * g_ref[...]
        h_ref[dst0 + r0:dst0 + r0 + chunk, :] = (y * (1.0 + sc) + sh).astype(BF16)


def _normmod_matmul_kernel(x_ref, g_ref, sc_ref, sh_ref, w_ref, *refs, w_transposed, has_tail):
    if has_tail:
        wt_ref, o_ref, ot_ref, h_ref = refs
    else:
        o_ref, h_ref = refs

    @pl.when(pl.program_id(1) == 0)
    def _():
        _normmod_to(h_ref, 0, x_ref, g_ref, sc_ref, sh_ref)
        if has_tail:
            ot_ref[...] = lax.dot_general(h_ref[...], wt_ref[...].astype(BF16), _CONTRACT_LAST,
                                          preferred_element_type=F32)

    w = w_ref[...].astype(BF16)
    if w_transposed:
        o_ref[...] = lax.dot_general(h_ref[...], w, _CONTRACT_LAST, preferred_element_type=F32)
    else:
        o_ref[...] = jnp.dot(h_ref[...], w, preferred_element_type=F32)


def _normmod_matmul(x, gain, sc, sh, w, ncols, tm, tn, name, w_transposed=False, w_tail=None):
    m, d = x.shape
    g, r, _ = sc.shape
    tiles_per_group = (m // tm) // g
    mod_spec = pl.BlockSpec((1, r, d), lambda i, j: (i // tiles_per_group, 0, 0))
    w_spec = pl.BlockSpec((tn, d), lambda i, j: (j, 0)) if w_transposed else pl.BlockSpec((d, tn), lambda i, j: (0, j))
    in_specs = [pl.BlockSpec((tm, d), lambda i, j: (i, 0)),
                pl.BlockSpec((1, d), lambda i, j: (0, 0)),
                mod_spec, mod_spec, w_spec]
    out_specs = [pl.BlockSpec((tm, tn), lambda i, j: (i, j))]
    out_shape = [jax.ShapeDtypeStruct((m, ncols), F32)]
    operands = [x, gain.reshape(1, d), sc, sh, w]
    if w_tail is not None:
        nt = w_tail.shape[0]
        in_specs.append(pl.BlockSpec((nt, d), lambda i, j: (0, 0)))
        out_specs.append(pl.BlockSpec((tm, nt), lambda i, j: (i, 0)))
        out_shape.append(jax.ShapeDtypeStruct((m, nt), F32))
        operands.append(w_tail)
    out = pl.pallas_call(
        functools.partial(_normmod_matmul_kernel, w_transposed=w_transposed, has_tail=w_tail is not None),
        grid=(m // tm, ncols // tn),
        in_specs=in_specs,
        out_specs=out_specs,
        out_shape=out_shape,
        scratch_shapes=[pltpu.VMEM((tm, d), BF16)],
        compiler_params=_cparams("arbitrary", "arbitrary"),
        name=name,
    )(*operands)
    return out if w_tail is not None else out[0]


def _conv_gn_silu(hist_ref, off, rows, wdw_ref, bdw_ref, gng_ref, gnb_ref, o_ref, row_chunk):
    conv_w = wdw_ref.shape[0]
    for r0 in range(0, rows, row_chunk):
        for c in range(CONV_GROUPS):
            cs = slice(c * LANES, (c + 1) * LANES)
            acc = jnp.zeros((row_chunk, LANES), F32)
            for r in range(SUBLANES):
                n_rows = row_chunk if r == 0 else row_chunk + SUBLANES
                part = jnp.zeros((n_rows, LANES), F32)
                for w in range((r - off) % SUBLANES, conv_w, SUBLANES):
                    a0 = off + r0 + w - r
                    part = part + hist_ref[a0:a0 + n_rows, cs] * wdw_ref[w:w + 1, cs]
                acc = acc + part[r:r + row_chunk]
            y = acc + bdw_ref[:, cs]
            mu = jnp.mean(y, axis=-1, keepdims=True)
            dlt = y - mu
            var = jnp.mean(dlt * dlt, axis=-1, keepdims=True)
            yn = dlt * lax.rsqrt(var + EPS) * gng_ref[:, cs] + gnb_ref[:, cs]
            o_ref[r0:r0 + row_chunk, cs] = _silu(yn).astype(o_ref.dtype)


HALO = 32


def _conv_prompt_kernel(za_ref, zg_ref, ha_ref, hg_ref, wdw_ref, bdw_ref, gng_ref, gnb_ref,
                        o_ref, newc_ref, hist_ref, *, tm, tiles_per_seq):
    i = pl.program_id(0)
    first = (i % tiles_per_seq) == 0
    a_halo = ha_ref[...] * jax.nn.sigmoid(hg_ref[...])
    hist_ref[0:HALO, :] = jnp.where(first, 0.0, a_halo)
    hist_ref[HALO:HALO + tm, :] = za_ref[...] * jax.nn.sigmoid(zg_ref[...])
    conv_w = wdw_ref.shape[0]
    _conv_gn_silu(hist_ref, HALO - (conv_w - 1), tm, wdw_ref, bdw_ref, gng_ref, gnb_ref, o_ref, 64)

    @pl.when((i % tiles_per_seq) == tiles_per_seq - 1)
    def _():
        newc_ref[0] = hist_ref[HALO + tm - (conv_w - 1):HALO + tm, :]


def _conv_prompt(z, n_seq, seq, d_conv, wdw, bdw, gng, gnb, tm=256):
    m = z.shape[0]
    conv_w = wdw.shape[0]
    tiles_per_seq = seq // tm
    cb = 1
    halo_idx = lambda i: jnp.maximum(i * (tm // HALO) - 1, 0)
    vec = pl.BlockSpec((1, d_conv), lambda i: (0, 0))
    return pl.pallas_call(
        functools.partial(_conv_prompt_kernel, tm=tm, tiles_per_seq=tiles_per_seq),
        grid=(m // tm,),
        in_specs=[pl.BlockSpec((tm, d_conv), lambda i: (i, 0)),
                  pl.BlockSpec((tm, d_conv), lambda i: (i, cb)),
                  pl.BlockSpec((HALO, d_conv), lambda i: (halo_idx(i), 0)),
                  pl.BlockSpec((HALO, d_conv), lambda i: (halo_idx(i), cb)),
                  pl.BlockSpec((conv_w, d_conv), lambda i: (0, 0)),
                  vec, vec, vec],
        out_specs=[pl.BlockSpec((tm, d_conv), lambda i: (i, 0)),
                   pl.BlockSpec((1, conv_w - 1, d_conv), lambda i: (i // tiles_per_seq, 0, 0))],
        out_shape=[jax.ShapeDtypeStruct((m, d_conv), BF16),
                   jax.ShapeDtypeStruct((n_seq, conv_w - 1, d_conv), F32)],
        scratch_shapes=[pltpu.VMEM((HALO + tm, d_conv), F32)],
        compiler_params=_cparams("arbitrary"),
        name="conv_prompt",
    )(z, z, z, z, wdw, bdw.reshape(1, -1), gng.reshape(1, -1), gnb.reshape(1, -1))


def _conv_sample_kernel(za_ref, zg_ref, st_ref, wdw_ref, bdw_ref, gng_ref, gnb_ref,
                        o_ref, newc_ref, hist_ref, *, t_new):
    conv_w = wdw_ref.shape[0]
    hist_ref[0:conv_w - 1, :] = st_ref[0]
    hist_ref[conv_w - 1:conv_w - 1 + t_new, :] = za_ref[...] * jax.nn.sigmoid(zg_ref[...])
    _conv_gn_silu(hist_ref, 0, t_new, wdw_ref, bdw_ref, gng_ref, gnb_ref, o_ref, t_new)
    newc_ref[0] = hist_ref[t_new:t_new + conv_w - 1, :]


def _conv_sample(z, state, d_conv, wdw, bdw, gng, gnb):
    n_seq, hist_rows, _ = state.shape
    conv_w = wdw.shape[0]
    m = z.shape[0]
    t_new = m // n_seq
    vec = pl.BlockSpec((1, d_conv), lambda b: (0, 0))
    return pl.pallas_call(
        functools.partial(_conv_sample_kernel, t_new=t_new),
        grid=(n_seq,),
        in_specs=[pl.BlockSpec((t_new, d_conv), lambda b: (b, 0)),
                  pl.BlockSpec((t_new, d_conv), lambda b: (b, 1)),
                  pl.BlockSpec((1, hist_rows, d_conv), lambda b: (b, 0, 0)),
                  pl.BlockSpec((conv_w, d_conv), lambda b: (0, 0)),
                  vec, vec, vec],
        out_specs=[pl.BlockSpec((t_new, d_conv), lambda b: (b, 0)),
                   pl.BlockSpec((1, hist_rows, d_conv), lambda b: (b, 0, 0))],
        out_shape=[jax.ShapeDtypeStruct((m, d_conv), F32),
                   jax.ShapeDtypeStruct((n_seq, hist_rows, d_conv), F32)],
        scratch_shapes=[pltpu.VMEM((hist_rows + t_new + SUBLANES, d_conv), F32)],
        compiler_params=_cparams("arbitrary"),
        name="conv_sample",
    )(z, z, state, wdw, bdw.reshape(1, -1), gng.reshape(1, -1), gnb.reshape(1, -1))


def _qk_kernel(zq_ref, zkv_ref, zqi0_ref, zqi1_ref, zt_ref, cos_ref, sin_ref, cosi_ref, sini_ref,
               gq_ref, gk_ref, qhm_ref, qihm_ref, k_ref, kbf_ref, vt_ref, ki_ref, kibf_ref, *, n_heads):
    cos, sin = cos_ref[...], sin_ref[...]
    cosi, sini = cosi_ref[...], sini_ref[...]
    tm = cos.shape[0]
    lane = lax.broadcasted_iota(I32, (tm, LANES), 1)
    low_half = (lane % IDX_DIM) < (IDX_DIM // 2)

    def norm_rope(x, g):
        y = x * lax.rsqrt(jnp.mean(x * x, axis=-1, keepdims=True) + EPS) * g
        return y * cos + pltpu.roll(y, LANES // 2, 1) * sin

    def rope_idx(x):
        r = jnp.where(low_half, pltpu.roll(x, LANES - IDX_DIM // 2, 1), pltpu.roll(x, IDX_DIM // 2, 1))
        return x * cosi + r * sini

    gq, gk = gq_ref[...], gk_ref[...]
    for h in range(n_heads):
        qhm_ref[0, h] = norm_rope(zq_ref[:, h * LANES:(h + 1) * LANES], gq).astype(BF16)
    kv_w = N_KV_HEADS * LANES
    for g in range(N_KV_HEADS):
        kg = norm_rope(zkv_ref[:, g * LANES:(g + 1) * LANES], gk)
        k_ref[:, g * LANES:(g + 1) * LANES] = kg
        kbf_ref[:, g * LANES:(g + 1) * LANES] = kg.astype(BF16)
    vt_ref[0] = zkv_ref[:, kv_w:2 * kv_w].T.astype(BF16)
    half = (N_IDX_HEADS * IDX_DIM) // 2
    for j in range(N_IDX_HEADS // 2):
        src = zqi0_ref if j * LANES < half else zqi1_ref
        c0 = (j * LANES) % half
        y = rope_idx(src[:, c0:c0 + LANES])
        qihm_ref[0, 2 * j] = y[:, :IDX_DIM].astype(BF16)
        qihm_ref[0, 2 * j + 1] = y[:, IDX_DIM:].astype(BF16)
    yk = rope_idx(zt_ref[...])[:, :IDX_DIM]
    ki_ref[...] = yk
    kibf_ref[...] = yk.astype(BF16)


def _qk_epilogue(z, zt, tabs, gq, gk, n_seq, seq, n_heads):
    m = z.shape[0]
    tm = QB
    head_dim = LANES
    cos, sin, cosi, sini = tabs
    tab_tiles = cos.shape[0] // tm
    kv_w = N_KV_HEADS * head_dim
    d_attn = n_heads * head_dim
    d_conv = d_attn
    q_cb = (2 * d_conv) // d_attn
    kv_cb = (2 * d_conv + d_attn) // (2 * kv_w)
    qi_w = (N_IDX_HEADS * IDX_DIM) // 2
    qi_cb = (2 * d_conv + d_attn + 2 * kv_w) // qi_w
    tiles_per_seq = seq // tm
    tab = pl.BlockSpec((tm, LANES), lambda i: (i % tab_tiles, 0))
    vec = pl.BlockSpec((1, LANES), lambda i: (0, 0))
    return pl.pallas_call(
        functools.partial(_qk_kernel, n_heads=n_heads),
        grid=(m // tm,),
        in_specs=[pl.BlockSpec((tm, d_attn), lambda i: (i, q_cb)),
                  pl.BlockSpec((tm, 2 * kv_w), lambda i: (i, kv_cb)),
                  pl.BlockSpec((tm, qi_w), lambda i: (i, qi_cb)),
                  pl.BlockSpec((tm, qi_w), lambda i: (i, qi_cb + 1)),
                  pl.BlockSpec((tm, LANES), lambda i: (i, 0)),
                  tab, tab, tab, tab, vec, vec],
        out_specs=[pl.BlockSpec((1, n_heads, tm, head_dim), lambda i: (i, 0, 0, 0)),
                   pl.BlockSpec((1, N_IDX_HEADS, tm, IDX_DIM), lambda i: (i, 0, 0, 0)),
                   pl.BlockSpec((tm, kv_w), lambda i: (i, 0)),
                   pl.BlockSpec((tm, kv_w), lambda i: (i, 0)),
                   pl.BlockSpec((1, kv_w, tm), lambda i: (i // tiles_per_seq, 0, i % tiles_per_seq)),
                   pl.BlockSpec((tm, IDX_DIM), lambda i: (i, 0)),
                   pl.BlockSpec((tm, IDX_DIM), lambda i: (i, 0))],
        out_shape=[jax.ShapeDtypeStruct((m // tm, n_heads, tm, head_dim), BF16),
                   jax.ShapeDtypeStruct((m // tm, N_IDX_HEADS, tm, IDX_DIM), BF16),
                   jax.ShapeDtypeStruct((m, kv_w), F32),
                   jax.ShapeDtypeStruct((m, kv_w), BF16),
                   jax.ShapeDtypeStruct((n_seq, kv_w, seq), BF16),
                   jax.ShapeDtypeStruct((m, IDX_DIM), F32),
                   jax.ShapeDtypeStruct((m, IDX_DIM), BF16)],
        compiler_params=_cparams("arbitrary"),
        name="qk_epilogue",
    )(z, z, z, z, zt, cos, sin, cosi, sini, gq.reshape(1, -1), gk.reshape(1, -1))


def _kth_largest_key(lo, hi, count_ge, n_sel):
    n_bits = jnp.max(32 - lax.clz(hi - lo))

    def step(it, thr):
        cand = thr + lax.shift_left(jnp.int32(1), n_bits - 1 - it)
        keep = count_ge(cand) >= n_sel
        return jnp.where(cand > thr, jnp.where(keep, cand, thr), thr)

    return lax.fori_loop(0, n_bits, step, lo)


def _pipelined_pairs(n_pairs, produce, consume):
    produce(0, 0)

    def body(p, carry):
        produce(2 * p + 1, 1)
        consume(2 * p, 0)
        produce(2 * p + 2, 0)
        consume(2 * p + 1, 1)
        return carry

    lax.fori_loop(0, n_pairs - 1, body, 0)
    last = 2 * (n_pairs - 1)
    produce(last + 1, 1)
    consume(last, 0)
    consume(last + 1, 1)


KT = 2 * QB


def _attn_prompt_kernel(qi_ref, w_ref, ki_ref, q_ref, k_ref, vt_ref, o_ref,
                        key_ref, m_ref, l_ref, acc_ref, cut_ref, lg_ref, s_ref, gmax_ref, *, n_sel, c_exp, n_heads,
                        idx_bits):
    i = pl.program_id(1)
    n_kt = lax.div(i * QB + QB + KT - 1, KT)
    row_iota = lax.broadcasted_iota(I32, (KT, QB), 0)
    w_all = w_ref[0] * IDX_SCALE
    heads_per_kv = n_heads // N_KV_HEADS
    hc = 4

    n_pairs = lax.shift_right_logical(n_kt + 1, 1)

    def score_products(kt, slot):
        ki_t = ki_ref[0, pl.ds(pl.multiple_of(kt * KT, KT), KT), :]
        for h0 in range(0, N_IDX_HEADS, hc):
            s_ref[slot, :, h0 * QB:(h0 + hc) * QB] = lax.dot_general(
                ki_t, qi_ref[0, h0:h0 + hc].reshape(hc * QB, IDX_DIM), _CONTRACT_LAST,
                preferred_element_type=F32)

    def score_keys(kt, slot):
        for half in range(KT // QB):
            rs = slice(half * QB, (half + 1) * QB)
            ks = pl.multiple_of(kt * KT + half * QB, QB)
            acc = jnp.zeros((QB, QB), F32)
            for h in range(N_IDX_HEADS):
                acc = acc + jnp.maximum(s_ref[slot, rs, h * QB:(h + 1) * QB], 0.0) * w_all[h:h + 1, :]
            acc = jnp.where(ks + lax.broadcasted_iota(I32, (QB, QB), 0) <=
                            i * QB + lax.broadcasted_iota(I32, (QB, QB), 1), acc, -jnp.inf)
            key = _order_key(acc)
            key_ref[pl.ds(ks, QB), :] = key
            gmax_ref[rs, :] = jnp.maximum(gmax_ref[rs, :], key)

    gmax_ref[...] = jnp.full(gmax_ref.shape, INT_MIN, I32)
    _pipelined_pairs(n_pairs, score_products, score_keys)

    def count(pred):
        def tile(kt, cnt):
            ks = pl.multiple_of(kt * KT, KT)
            hit = jnp.where(pred(key_ref[pl.ds(ks, KT), :], ks + row_iota), 1, 0)
            return cnt + jnp.sum(hit.reshape(KT // SUBLANES, SUBLANES, QB), axis=0)
        n_pairs = lax.shift_right_logical(n_kt, 1)
        cnt = lax.fori_loop(0, n_pairs, lambda p, cnt: tile(2 * p + 1, tile(2 * p, cnt)),
                            jnp.zeros((SUBLANES, QB), I32))
        cnt = lax.fori_loop(2 * n_pairs, n_kt, tile, cnt)
        return jnp.sum(cnt, axis=0, keepdims=True)

    gmax = gmax_ref[...]
    thr = _kth_largest_key(jnp.min(gmax, axis=0, keepdims=True), jnp.max(gmax, axis=0, keepdims=True),
                           lambda cand: count(lambda kk, pos: kk >= cand), n_sel)
    cnt_ge = count(lambda kk, pos: kk >= thr)
    need = n_sel - count(lambda kk, pos: kk > thr)
    cut_ref[...] = jnp.full((1, QB), INT_MAX, I32)

    @pl.when(jnp.max(cnt_ge) > n_sel)
    def _():
        def search_pos(it, p):
            cand = p + lax.shift_left(jnp.int32(1), idx_bits - 1 - it)
            c = count(lambda kk, pos: jnp.where(kk == thr, pos, INT_MAX) < cand)
            return jnp.where(c < need, cand, p)
        cut_ref[...] = lax.fori_loop(0, idx_bits, search_pos, jnp.zeros((1, QB), I32))

    few = thr <= KEY_NEG_INF
    thr_eff = jnp.where(few, KEY_NEG_INF + 1, thr)
    cut = jnp.where(few, INT_MAX, cut_ref[...])

    m_ref[...] = jnp.full(m_ref.shape, M_INIT, F32)
    l_ref[...] = jnp.zeros(l_ref.shape, F32)
    acc_ref[...] = jnp.zeros(acc_ref.shape, F32)

    def qk_products(kt, slot):
        ks = pl.multiple_of(kt * KT, KT)
        for g in range(N_KV_HEADS):
            k_t = k_ref[0, pl.ds(ks, KT), g * LANES:(g + 1) * LANES]
            q_g = q_ref[0, g * heads_per_kv:(g + 1) * heads_per_kv].reshape(heads_per_kv * QB, LANES)
            lg_ref[slot, :, g * heads_per_kv * QB:(g + 1) * heads_per_kv * QB] = lax.dot_general(
                k_t, q_g, _CONTRACT_LAST, preferred_element_type=F32)

    def attn_tile(kt, slot):
        ks = pl.multiple_of(kt * KT, KT)
        kk = key_ref[pl.ds(ks, KT), :]
        sel = (kk - jnp.where(ks + row_iota > cut, 1, 0)) >= thr_eff
        m_old, l_old = m_ref[...], l_ref[...]
        m_parts, l_parts = [], []
        for g in range(N_KV_HEADS):
            v_t = vt_ref[0, g * LANES:(g + 1) * LANES, pl.ds(ks, KT)]
            for hh in range(heads_per_kv):
                cs = slice((g * heads_per_kv + hh) * QB, (g * heads_per_kv + hh + 1) * QB)
                raw = jnp.where(sel, lg_ref[slot, :, cs], NEG_BIG)
                m_new = jnp.maximum(m_old[:, cs], jnp.max(raw, axis=0, keepdims=True))
                alpha = jnp.exp2((m_old[:, cs] - m_new) * c_exp)
                p = jnp.exp2((raw - m_new) * c_exp)
                m_parts.append(m_new)
                l_parts.append(alpha * l_old[:, cs] + jnp.sum(p, axis=0, keepdims=True))
                acc_ref[:, cs] = alpha * acc_ref[:, cs] + jnp.dot(v_t, p.astype(BF16),
                                                                   preferred_element_type=F32)
        m_ref[...] = jnp.concatenate(m_parts, axis=1)
        l_ref[...] = jnp.concatenate(l_parts, axis=1)

    _pipelined_pairs(n_pairs, qk_products, attn_tile)
    for h in range(n_heads):
        cs = slice(h * QB, (h + 1) * QB)
        o_ref[:, cs] = (acc_ref[:, cs] / l_ref[:, cs]).T


def _attn_prompt(qi_hm, w_hm, ki_bf, q_hm, k_bf, vt_bf, n_seq, seq, n_sel):
    n_heads = q_hm.shape[1]
    head_dim = q_hm.shape[3]
    nblk = seq // QB
    kv_w = k_bf.shape[-1]
    assert n_sel <= KT and seq % (2 * KT) == 0
    blk = lambda b, i: (b * nblk + i, 0, 0, 0)
    return pl.pallas_call(
        functools.partial(_attn_prompt_kernel, n_sel=n_sel, c_exp=head_dim ** -0.5 * LOG2_E, n_heads=n_heads,
                          idx_bits=int(seq).bit_length()),
        grid=(n_seq, nblk),
        in_specs=[pl.BlockSpec((1, N_IDX_HEADS, QB, IDX_DIM), blk),
                  pl.BlockSpec((1, N_IDX_HEADS, QB), lambda b, i: (b * nblk + i, 0, 0)),
                  pl.BlockSpec((1, seq, IDX_DIM), lambda b, i: (b, 0, 0)),
                  pl.BlockSpec((1, n_heads, QB, head_dim), blk),
                  pl.BlockSpec((1, seq, kv_w), lambda b, i: (b, 0, 0)),
                  pl.BlockSpec((1, kv_w, seq), lambda b, i: (b, 0, 0))],
        out_specs=pl.BlockSpec((QB, n_heads * head_dim), lambda b, i: (b * nblk + i, 0)),
        out_shape=jax.ShapeDtypeStruct((n_seq * seq, n_heads * head_dim), F32),
        scratch_shapes=[pltpu.VMEM((seq, QB), I32),
                        pltpu.VMEM((1, n_heads * QB), F32),
                        pltpu.VMEM((1, n_heads * QB), F32),
                        pltpu.VMEM((head_dim, n_heads * QB), F32),
                        pltpu.VMEM((1, QB), I32),
                        pltpu.VMEM((2, KT, n_heads * QB), F32),
                        pltpu.VMEM((2, KT, N_IDX_HEADS * QB), F32),
                        pltpu.VMEM((KT, QB), I32)],
        compiler_params=_cparams("arbitrary", "arbitrary"),
        name="attn_prompt",
    )(qi_hm, w_hm, ki_bf.reshape(n_seq, seq, IDX_DIM), q_hm, k_bf.reshape(n_seq, seq, kv_w), vt_bf)


SCORE_PAGES = 32
ATTN_PAGES = 16
ATTN_SEQS = 2


def _sample_score_kernel(pt_ref, qi_ref, w_ref, kin_ref, *refs, t_new):
    pages, (sp_ref, sn_ref) = refs[:SCORE_PAGES], refs[SCORE_PAGES:]
    page = pages[0].shape[2]
    qi = qi_ref[0]
    rows = qi.shape[0]
    wb = jnp.broadcast_to(w_ref[0] * IDX_SCALE, (rows, page))

    def score(keys_t_bf):
        s = jnp.dot(qi, keys_t_bf, preferred_element_type=F32)
        r = jnp.maximum(s, 0.0) * wb
        return jnp.sum(r.reshape(N_IDX_HEADS, t_new, page), axis=0)

    for j in range(SCORE_PAGES):
        sp_ref[0, :, j * page:(j + 1) * page] = score(pages[j][0].astype(BF16))

    @pl.when(pl.program_id(1) == 0)
    def _():
        sn = score(kin_ref[0])
        s_idx = lax.broadcasted_iota(I32, (t_new, page), 1)
        t_idx = lax.broadcasted_iota(I32, (t_new, page), 0)
        sn_ref[0] = jnp.where(s_idx <= t_idx, sn, -jnp.inf)


def _sample_scores(page_table, qi_s, w_s, ki_new_t, cache_kidx_t, t_new):
    n_seq, n_pages = page_table.shape
    page = cache_kidx_t.shape[2]
    rows = qi_s.shape[1]
    page_specs = [pl.BlockSpec((1, IDX_DIM, page),
                               functools.partial(lambda b, c, pt, j: (pt[b, c * SCORE_PAGES + j], 0, 0), j=j))
                  for j in range(SCORE_PAGES)]
    grid_spec = pltpu.PrefetchScalarGridSpec(
        num_scalar_prefetch=1,
        grid=(n_seq, n_pages // SCORE_PAGES),
        in_specs=[pl.BlockSpec((1, rows, IDX_DIM), lambda b, c, pt: (b, 0, 0)),
                  pl.BlockSpec((1, rows, 1), lambda b, c, pt: (b, 0, 0)),
                  pl.BlockSpec((1, IDX_DIM, page), lambda b, c, pt: (b, 0, 0))] + page_specs,
        out_specs=[pl.BlockSpec((1, t_new, SCORE_PAGES * page), lambda b, c, pt: (b, 0, c)),
                   pl.BlockSpec((1, t_new, page), lambda b, c, pt: (b, 0, 0))],
    )
    return pl.pallas_call(
        functools.partial(_sample_score_kernel, t_new=t_new),
        grid_spec=grid_spec,
        out_shape=[jax.ShapeDtypeStruct((n_seq, t_new, n_pages * page), F32),
                   jax.ShapeDtypeStruct((n_seq, t_new, page), F32)],
        compiler_params=_cparams("arbitrary", "arbitrary"),
        name="sample_scores",
    )(page_table, qi_s, w_s, ki_new_t, *([cache_kidx_t] * SCORE_PAGES))


def _sample_thresh_kernel(sp_ref, sn_ref, ex_ref, mask_ref, key_ref, cut_ref, *, n_sel, n_past, idx_bits):
    rows = sp_ref.shape[0]
    n_tiles = n_past // LANES + 1
    key_ref[:, 0:n_past] = _order_key(sp_ref[...])
    key_ref[:, n_past:n_past + LANES] = _order_key(sn_ref[...])
    lane = lax.broadcasted_iota(I32, (rows, LANES), 1)

    def count(pred):
        def body(j, cnt):
            c0 = pl.multiple_of(j * LANES, LANES)
            return cnt + jnp.where(pred(key_ref[:, pl.ds(c0, LANES)], c0 + lane), 1, 0)
        cnt = lax.fori_loop(0, n_tiles, body, jnp.zeros((rows, LANES), I32), unroll=8)
        return jnp.broadcast_to(jnp.sum(cnt, axis=1, keepdims=True), (rows, LANES))

    def class_max(j, carry):
        c0 = pl.multiple_of(j * 2 * LANES, 2 * LANES)
        return (jnp.maximum(carry[0], key_ref[:, pl.ds(c0, LANES)]),
                jnp.maximum(carry[1], key_ref[:, pl.ds(c0 + LANES, LANES)]))

    floor = jnp.full((rows, LANES), INT_MIN, I32)
    even, odd = lax.fori_loop(0, n_tiles // 2, class_max, (floor, floor), unroll=8)
    if n_tiles % 2:
        even = jnp.maximum(even, key_ref[:, (n_tiles - 1) * LANES:n_tiles * LANES])
    lo = jnp.broadcast_to(jnp.min(jnp.minimum(even, odd), axis=1, keepdims=True), (rows, LANES))
    hi = jnp.broadcast_to(jnp.max(jnp.maximum(even, odd), axis=1, keepdims=True), (rows, LANES))
    thr = _kth_largest_key(lo, hi, lambda cand: count(lambda kk, pos: kk >= cand), n_sel)
    cnt_ge = count(lambda kk, pos: kk >= thr)
    need = n_sel - count(lambda kk, pos: kk > thr)
    cut_ref[...] = jnp.full((rows, LANES), INT_MAX, I32)

    @pl.when(jnp.max(cnt_ge) > n_sel)
    def _():
        def search_pos(it, p):
            cand = p + lax.shift_left(jnp.int32(1), idx_bits - 1 - it)
            c = count(lambda kk, pos: jnp.where(kk == thr, pos, INT_MAX) < cand)
            return jnp.where(c < need, cand, p)
        cut_ref[...] = lax.fori_loop(0, idx_bits, search_pos, jnp.zeros((rows, LANES), I32))

    few = thr <= KEY_NEG_INF
    thr_eff = jnp.where(few, KEY_NEG_INF + 1, thr)
    cut = jnp.where(few, INT_MAX, cut_ref[...])

    ex = ex_ref[...]
    width = ex.shape[1]

    def emit(j, carry):
        c0 = pl.multiple_of(j * LANES, LANES)
        hit = jnp.where((key_ref[:, pl.ds(c0, LANES)] - jnp.where(c0 + lane > cut, 1, 0)) >= thr_eff, 1.0, 0.0)
        mask_ref[:, pl.ds(pl.multiple_of(j * width, width), width)] = jnp.dot(
            hit.astype(BF16), ex, preferred_element_type=F32)
        return carry

    lax.fori_loop(0, n_tiles, emit, 0, unroll=4)


def _sample_thresh(sp, sn, n_sel, rows_per_step=64):
    m, n_past = sp.shape
    assert n_sel <= 2 * LANES and n_past >= LANES
    rows_per_step = min(rows_per_step, m)
    tok = jnp.arange(LANES, dtype=I32)[:, None]
    col = jnp.arange(LANES * N_KV_HEADS, dtype=I32)[None, :]
    expand = (col // N_KV_HEADS == tok).astype(BF16)
    width = (n_past + LANES) * N_KV_HEADS
    return pl.pallas_call(
        functools.partial(_sample_thresh_kernel, n_sel=n_sel, n_past=n_past,
                          idx_bits=int(n_past + LANES).bit_length()),
        grid=(m // rows_per_step,),
        in_specs=[pl.BlockSpec((rows_per_step, n_past), lambda r: (r, 0)),
                  pl.BlockSpec((rows_per_step, LANES), lambda r: (r, 0)),
                  pl.BlockSpec(expand.shape, lambda r: (0, 0))],
        out_specs=pl.BlockSpec((rows_per_step, width), lambda r: (r, 0)),
        out_shape=jax.ShapeDtypeStruct((m, width), F32),
        scratch_shapes=[pltpu.VMEM((rows_per_step, n_past + LANES), I32),
                        pltpu.VMEM((rows_per_step, LANES), I32)],
        compiler_params=_cparams("arbitrary"),
        name="sample_thresh",
    )(sp, sn, expand)


def _sample_attn_kernel(pt_ref, q_ref, mask_ref, maskn_ref, kn_ref, vn_ref, *refs, c_exp, group_rows):
    n_pg = ATTN_SEQS * ATTN_PAGES
    k_pages, v_pages = refs[:n_pg], refs[n_pg:2 * n_pg]
    o_ref, m_ref, l_ref, acc_ref = refs[2 * n_pg:]
    c = pl.program_id(1)
    rows = q_ref.shape[1]
    width = maskn_ref.shape[2]
    reps = rows // mask_ref.shape[1]
    own_head = jnp.where(lax.broadcasted_iota(I32, (rows, width), 1) % N_KV_HEADS ==
                         lax.broadcasted_iota(I32, (rows, width), 0) // group_rows, 1.0, 0.0)

    @pl.when(c == 0)
    def _():
        m_ref[...] = jnp.full(m_ref.shape, M_INIT, F32)
        l_ref[...] = jnp.zeros(l_ref.shape, F32)
        acc_ref[...] = jnp.zeros(acc_ref.shape, F32)

    def select(flags):
        return jnp.concatenate([flags] * reps, axis=0) * own_head > 0.5

    def attend(s, tiles):
        q = q_ref[s]
        raws = [jnp.where(sel, lax.dot_general(q, k, _CONTRACT_LAST, preferred_element_type=F32), NEG_BIG)
                for sel, k, _ in tiles]
        m_old = m_ref[s]
        m_new = jnp.maximum(m_old, jnp.max(functools.reduce(jnp.maximum, raws), axis=1, keepdims=True))
        alpha = jnp.exp2((m_old - m_new) * c_exp)
        ps = [jnp.exp2((raw - m_new) * c_exp) for raw in raws]
        l_ref[s] = alpha * l_ref[s] + jnp.sum(functools.reduce(jnp.add, ps), axis=1, keepdims=True)
        m_ref[s] = m_new
        pv = functools.reduce(jnp.add, [jnp.dot(p.astype(BF16), v, preferred_element_type=F32)
                                        for p, (_, _, v) in zip(ps, tiles)])
        acc_ref[s] = alpha * acc_ref[s] + pv

    for s in range(ATTN_SEQS):
        attend(s, [(select(mask_ref[s, :, j * width:(j + 1) * width]),
                    k_pages[s * ATTN_PAGES + j][0].astype(BF16), v_pages[s * ATTN_PAGES + j][0].astype(BF16))
                   for j in range(ATTN_PAGES)])

    @pl.when(c == pl.num_programs(1) - 1)
    def _():
        for s in range(ATTN_SEQS):
            attend(s, [(select(maskn_ref[s]), kn_ref[s], vn_ref[s])])
            o_ref[s] = acc_ref[s] / l_ref[s]


def _sample_attn(page_table, q_s, mask, k_new, v_new, cache_k, cache_v, t_new):
    n_seq, n_pages = page_table.shape
    page_rows, head_dim = cache_k.shape[1], cache_k.shape[2]
    rows = q_s.shape[1]
    page_specs = [pl.BlockSpec((1, page_rows, head_dim),
                               functools.partial(lambda b, c, pt, s, j: (pt[b * ATTN_SEQS + s, c * ATTN_PAGES + j], 0, 0),
                                                 s=s, j=j))
                  for s in range(ATTN_SEQS) for j in range(ATTN_PAGES)]
    per_seq = lambda shape: pl.BlockSpec((ATTN_SEQS,) + shape, lambda b, c, pt: (b, 0, 0))
    grid_spec = pltpu.PrefetchScalarGridSpec(
        num_scalar_prefetch=1,
        grid=(n_seq // ATTN_SEQS, n_pages // ATTN_PAGES),
        in_specs=[per_seq((rows, head_dim)),
                  pl.BlockSpec((ATTN_SEQS, t_new, ATTN_PAGES * page_rows), lambda b, c, pt: (b, 0, c)),
                  pl.BlockSpec((ATTN_SEQS, t_new, page_rows), lambda b, c, pt: (b, 0, n_pages)),
                  per_seq((page_rows, head_dim)), per_seq((page_rows, head_dim))] + page_specs + page_specs,
        out_specs=per_seq((rows, head_dim)),
        scratch_shapes=[pltpu.VMEM((ATTN_SEQS, rows, 1), F32),
                        pltpu.VMEM((ATTN_SEQS, rows, 1), F32),
                        pltpu.VMEM((ATTN_SEQS, rows, head_dim), F32)],
    )
    return pl.pallas_call(
        functools.partial(_sample_attn_kernel, c_exp=head_dim ** -0.5 * LOG2_E, group_rows=rows // N_KV_HEADS),
        grid_spec=grid_spec,
        out_shape=jax.ShapeDtypeStruct((n_seq, rows, head_dim), F32),
        compiler_params=_cparams("arbitrary", "arbitrary"),
        name="sample_attn",
    )(page_table, q_s, mask, mask, k_new, v_new,
      *([cache_k] * (ATTN_SEQS * ATTN_PAGES)), *([cache_v] * (ATTN_SEQS * ATTN_PAGES)))


def _merge_kernel(x_ref, gt_ref, co_ref, ao_ref, go_ref, wc_ref, wa_ref, o_ref, on_ref, *, n_heads):
    @pl.when(pl.program_id(1) == 0)
    def _():
        g = go_ref[...]
        for h in range(n_heads):
            cs = slice(h * LANES, (h + 1) * LANES)
            o = ao_ref[:, cs]
            on_ref[:, cs] = (o * lax.rsqrt(jnp.mean(o * o, axis=-1, keepdims=True) + EPS) * g).astype(BF16)

    y = jnp.dot(co_ref[...].astype(BF16), wc_ref[...].astype(BF16), preferred_element_type=F32)
    y = y + jnp.dot(on_ref[...], wa_ref[...].astype(BF16), preferred_element_type=F32)
    o_ref[...] = x_ref[...] + gt_ref[0] * y


def _merge(x, gt, conv_o, attn_o, g_o, w_out, tm, tn=512):
    m, d = x.shape
    g, r, _ = gt.shape
    d_conv = conv_o.shape[1]
    d_attn = attn_o.shape[1]
    tiles_per_group = (m // tm) // g
    rb = d_conv // d_attn
    return pl.pallas_call(
        functools.partial(_merge_kernel, n_heads=d_attn // LANES),
        grid=(m // tm, d // tn),
        in_specs=[pl.BlockSpec((tm, tn), lambda i, j: (i, j)),
                  pl.BlockSpec((1, r, tn), lambda i, j: (i // tiles_per_group, 0, j)),
                  pl.BlockSpec((tm, d_conv), lambda i, j: (i, 0)),
                  pl.BlockSpec((tm, d_attn), lambda i, j: (i, 0)),
                  pl.BlockSpec((1, LANES), lambda i, j: (0, 0)),
                  pl.BlockSpec((d_conv, tn), lambda i, j: (0, j)),
                  pl.BlockSpec((d_attn, tn), lambda i, j: (rb, j))],
        out_specs=pl.BlockSpec((tm, tn), lambda i, j: (i, j)),
        out_shape=jax.ShapeDtypeStruct((m, d), F32),
        scratch_shapes=[pltpu.VMEM((tm, d_attn), BF16)],
        compiler_params=_cparams("arbitrary", "arbitrary"),
        name="merge",
    )(x, gt, conv_o, attn_o, g_o.reshape(1, -1), w_out, w_out)


def _ffn_act(cur_g, p1_g, p2_g, cur_v, p1_v, p2_v, wg, wv, bg, bv):
    gate = p2_g * wg[0:1, :] + p1_g * wg[1:2, :] + cur_g * wg[2:3, :] + bg
    val = p2_v * wv[0:1, :] + p1_v * wv[1:2, :] + cur_v * wv[2:3, :] + bv
    return (_silu(gate) * val).astype(BF16)


def _ffn_finish(f, act, wd_ref, x_ref, gt_ref, o_ref, acc_ref):
    @pl.when(f == 0)
    def _():
        acc_ref[...] = jnp.zeros(acc_ref.shape, F32)

    acc_ref[...] += jnp.dot(act, wd_ref[...].astype(BF16), preferred_element_type=F32)

    @pl.when(f == pl.num_programs(1) - 1)
    def _():
        o_ref[...] = x_ref[...] + gt_ref[0] * acc_ref[...]


def _to_bf16_kernel(x_ref, o_ref):
    o_ref[...] = x_ref[...].astype(BF16)


def _to_bf16(w, cols_per_step=512):
    r, c = w.shape
    return pl.pallas_call(
        _to_bf16_kernel,
        grid=(c // cols_per_step,),
        in_specs=[pl.BlockSpec((r, cols_per_step), lambda j: (0, j))],
        out_specs=pl.BlockSpec((r, cols_per_step), lambda j: (0, j)),
        out_shape=jax.ShapeDtypeStruct((r, c), BF16),
        compiler_params=_cparams("arbitrary"),
        name="to_bf16",
    )(w)


FFN_HALO = 16
FFN_PARTS = 2


def _ffn_prompt_kernel(x_ref, xh_ref, g_ref, sc_ref, sh_ref, wug_ref, wuv_ref, wg_ref, wv_ref, bg_ref, bv_ref,
                       wd_ref, gt_ref, o_ref, tg_ref, tv_ref, h_ref, hist_g, hist_v, act_ref, *, tm, tiles_per_seq):
    i, f = pl.program_id(0), pl.program_id(1)

    @pl.when(f == 0)
    def _():
        _normmod_to(h_ref, 0, xh_ref, g_ref, sc_ref, sh_ref)
        _normmod_to(h_ref, FFN_HALO, x_ref, g_ref, sc_ref, sh_ref)
        o_ref[...] = jnp.zeros(o_ref.shape, F32)

    n_parts, _, pw = hist_g.shape
    for s in range(n_parts):
        cs = slice(s * pw, (s + 1) * pw)
        hist_g[s] = jnp.dot(h_ref[...], wug_ref[:, cs], preferred_element_type=F32)
        hist_v[s] = jnp.dot(h_ref[...], wuv_ref[:, cs], preferred_element_type=F32)

    @pl.when((i % tiles_per_seq) == 0)
    def _():
        hist_g[:, 0:FFN_HALO, :] = jnp.zeros((n_parts, FFN_HALO, pw), F32)
        hist_v[:, 0:FFN_HALO, :] = jnp.zeros((n_parts, FFN_HALO, pw), F32)

    chunk = 64
    down = None
    for s in range(n_parts):
        cs = slice(s * pw, (s + 1) * pw)
        wg, wv, bg, bv = wg_ref[:, cs], wv_ref[:, cs], bg_ref[:, cs], bv_ref[:, cs]
        for r0 in range(0, tm, chunk):
            rows = lambda hist, k: hist[s, FFN_HALO + r0 - k:FFN_HALO + r0 - k + chunk, :]
            act_ref[s, r0:r0 + chunk, :] = _ffn_act(rows(hist_g, 0), rows(hist_g, 1), rows(hist_g, 2),
                                                    rows(hist_v, 0), rows(hist_v, 1), rows(hist_v, 2),
                                                    wg, wv, bg, bv)
        tg_ref[0, :, cs] = hist_g[s, tm + FFN_HALO - SUBLANES:tm + FFN_HALO, :]
        tv_ref[0, :, cs] = hist_v[s, tm + FFN_HALO - SUBLANES:tm + FFN_HALO, :]
        part = jnp.dot(act_ref[s], wd_ref[cs, :], preferred_element_type=F32)
        down = part if down is None else down + part
    o_ref[...] += down

    @pl.when(f == pl.num_programs(1) - 1)
    def _():
        o_ref[...] = x_ref[...] + gt_ref[0] * o_ref[...]


def _ffn_prompt(x1, gain, sc, sh, gt, w_up_bf, w_dw, b_dw, w_down_bf, seq, tm, tf=512):
    m, d = x1.shape
    d_ff = w_down_bf.shape[0]
    nf = d_ff // tf
    tiles_per_seq = seq // tm
    fw = w_dw.shape[0]
    halo_idx = lambda i: jnp.maximum(i * (tm // FFN_HALO) - 1, 0)
    mod = pl.BlockSpec((1, 1, d), lambda i, f: (i // tiles_per_seq, 0, 0))
    tail = pl.BlockSpec((1, SUBLANES, tf), lambda i, f: (i, 0, f))
    return pl.pallas_call(
        functools.partial(_ffn_prompt_kernel, tm=tm, tiles_per_seq=tiles_per_seq),
        grid=(m // tm, nf),
        in_specs=[pl.BlockSpec((tm, d), lambda i, f: (i, 0)),
                  pl.BlockSpec((FFN_HALO, d), lambda i, f: (halo_idx(i), 0)),
                  pl.BlockSpec((1, d), lambda i, f: (0, 0)),
                  mod, mod,
                  pl.BlockSpec((d, tf), lambda i, f: (0, f)),
                  pl.BlockSpec((d, tf), lambda i, f: (0, f + nf)),
                  pl.BlockSpec((fw, tf), lambda i, f: (0, f)),
                  pl.BlockSpec((fw, tf), lambda i, f: (0, f + nf)),
                  pl.BlockSpec((1, tf), lambda i, f: (0, f)),
                  pl.BlockSpec((1, tf), lambda i, f: (0, f + nf)),
                  pl.BlockSpec((tf, d), lambda i, f: (f, 0)),
                  mod],
        out_specs=[pl.BlockSpec((tm, d), lambda i, f: (i, 0)), tail, tail],
        out_shape=[jax.ShapeDtypeStruct((m, d), F32),
                   jax.ShapeDtypeStruct((m // tm, SUBLANES, d_ff), F32),
                   jax.ShapeDtypeStruct((m // tm, SUBLANES, d_ff), F32)],
        scratch_shapes=[pltpu.VMEM((FFN_HALO + tm, d), BF16),
                        pltpu.VMEM((FFN_PARTS, FFN_HALO + tm, tf // FFN_PARTS), F32),
                        pltpu.VMEM((FFN_PARTS, FFN_HALO + tm, tf // FFN_PARTS), F32),
                        pltpu.VMEM((FFN_PARTS, tm, tf // FFN_PARTS), BF16)],
        compiler_params=_cparams("arbitrary", "arbitrary"),
        name="ffn_prompt",
    )(x1, x1, gain.reshape(1, d), sc, sh, w_up_bf, w_up_bf, w_dw, w_dw, b_dw.reshape(1, -1), b_dw.reshape(1, -1),
      w_down_bf, gt)


def _ffn_sample_kernel(cg_ref, cv_ref, p1g_ref, p1v_ref, p2g_ref, p2v_ref, wg_ref, wv_ref, bg_ref, bv_ref,
                       wd_ref, x_ref, gt_ref, o_ref, acc_ref):
    act = _ffn_act(cg_ref[...], p1g_ref[...], p2g_ref[...], cv_ref[...], p1v_ref[...], p2v_ref[...],
                   wg_ref[...], wv_ref[...], bg_ref[...], bv_ref[...])
    _ffn_finish(pl.program_id(1), act, wd_ref, x_ref, gt_ref, o_ref, acc_ref)


def _ffn_sample(cur, prev1, prev2, x1, gt, w_dw, b_dw, w_down, tf=512):
    m, d = x1.shape
    d_ff = w_down.shape[0]
    nf = d_ff // tf
    fw = w_dw.shape[0]
    lo = pl.BlockSpec((m, tf), lambda i, f: (0, f))
    hi = pl.BlockSpec((m, tf), lambda i, f: (0, f + nf))
    return pl.pallas_call(
        _ffn_sample_kernel,
        grid=(1, nf),
        in_specs=[lo, hi, lo, hi, lo, hi,
                  pl.BlockSpec((fw, tf), lambda i, f: (0, f)),
                  pl.BlockSpec((fw, tf), lambda i, f: (0, f + nf)),
                  pl.BlockSpec((1, tf), lambda i, f: (0, f)),
                  pl.BlockSpec((1, tf), lambda i, f: (0, f + nf)),
                  pl.BlockSpec((tf, d), lambda i, f: (f, 0)),
                  pl.BlockSpec((m, d), lambda i, f: (0, 0)),
                  pl.BlockSpec((1, m, d), lambda i, f: (0, 0, 0))],
        out_specs=pl.BlockSpec((m, d), lambda i, f: (0, 0)),
        out_shape=jax.ShapeDtypeStruct((m, d), F32),
        scratch_shapes=[pltpu.VMEM((m, d), F32)],
        compiler_params=_cparams("arbitrary", "arbitrary"),
        name="ffn_sample",
    )(cur, cur, prev1, prev1, prev2, prev2, w_dw, w_dw, b_dw.reshape(1, -1), b_dw.reshape(1, -1),
      w_down, x1, gt)


def _rope_tables(pos):
    def tab(dim):
        inv = jnp.power(ROPE_THETA, -jnp.arange(0, dim, 2, dtype=F32) / dim)
        ang = pos.astype(F32)[:, None] * inv[None, :]
        cos, sin = jnp.cos(ang), jnp.sin(ang)
        reps = LANES // dim
        return (jnp.tile(jnp.concatenate([cos, cos], axis=-1), (1, reps)),
                jnp.tile(jnp.concatenate([-sin, sin], axis=-1), (1, reps)))
    return tab(LANES) + tab(IDX_DIM)


def _in_proj(x2d, norm1, sc, sh, w_in_t, n_main, tm):
    w_tail = jnp.pad(w_in_t[n_main:], ((0, LANES - (w_in_t.shape[0] - n_main)), (0, 0)))
    z, zt = _normmod_matmul(x2d, norm1, sc, sh, w_in_t, n_main, tm, 512, "in_proj", w_transposed=True,
                            w_tail=w_tail)
    return z, zt


def kernel(x_prompt, x_sample, cache_k, cache_v, cache_kidx, state_conv, state_ffn, page_table, c_prompt, c_sample, norm1, w_ada, b_ada, w_in, g_q, g_k, w_dw_a, b_dw_a, gn_g, gn_b, g_o, w_out, norm2, w_up, w_dw_f, b_dw_f, w_down):
    n_b, seq, d = x_prompt.shape
    n_s, t_new, _ = x_sample.shape
    depth = norm1.shape[0]
    assert depth == 1
    head_dim = g_q.shape[-1]
    assert head_dim == LANES
    d_conv = w_dw_a.shape[-1]
    d_attn = w_out.shape[1] - d_conv
    n_heads = d_attn // head_dim
    kv_w = N_KV_HEADS * head_dim
    n_main = 2 * d_conv + d_attn + 2 * kv_w + N_IDX_HEADS * IDX_DIM
    n_pool, page = cache_k.shape[1], cache_k.shape[2]
    n_pages = page_table.shape[1]
    n_past = n_pages * page
    mp, ms = n_b * seq, n_s * t_new
    heads_per_kv = n_heads // N_KV_HEADS

    n_c = n_b + n_s
    pad_c = (-n_c) % SUBLANES
    c_all = jnp.concatenate([c_prompt, c_sample, jnp.zeros((pad_c, d), F32)], axis=0)
    mods = _ada(c_all, w_ada[0], b_ada[0])
    mp6 = mods[:n_b].reshape(n_b, 6, 1, d)
    sh1p, sc1p, gt1p, sh2p, sc2p, gt2p = [mp6[:, k] for k in range(6)]
    ms6 = jnp.repeat(mods[n_b:n_c].reshape(n_s, 6, d), t_new, axis=0)
    sh1s, sc1s, gt1s, sh2s, sc2s, gt2s = [ms6[:, k][None] for k in range(6)]

    w_in_t = jnp.swapaxes(w_in[0], 0, 1)

    xp = x_prompt.reshape(mp, d)
    tm_p = 1024 if seq % 1024 == 0 else QB
    z, zt = _in_proj(xp, norm1[0], sc1p, sh1p, w_in_t, n_main, tm_p)
    conv_o, new_conv_p = _conv_prompt(z, n_b, seq, d_conv, w_dw_a[0], b_dw_a[0], gn_g[0], gn_b[0],
                                      tm=256 if seq % 256 == 0 else QB)
    tabs_p = _rope_tables(jnp.arange(seq, dtype=I32))
    q_hm, qi_hm, k_p, k_bf, vt_bf, ki_p, ki_bf = _qk_epilogue(z, zt, tabs_p, g_q[0], g_k[0], n_b, seq, n_heads)
    w_hm = zt[:, IDX_DIM:IDX_DIM + N_IDX_HEADS].reshape(mp // QB, QB, N_IDX_HEADS).transpose(0, 2, 1)
    n_sel_p = min(TOPK_MAX, seq // 4)
    attn_p = _attn_prompt(qi_hm, w_hm, ki_bf, q_hm, k_bf, vt_bf, n_b, seq, n_sel_p)
    x1p = _merge(xp, gt1p, conv_o, attn_p, g_o[0], w_out[0],tm_p)
    w_up_bf, w_down_bf = _to_bf16(w_up[0]), _to_bf16(w_down[0])
    tm_f = 512 if seq % 512 == 0 else QB
    y_p, u_tail_g, u_tail_v = _ffn_prompt(x1p, norm2[0], sc2p, sh2p, gt2p, w_up_bf, w_dw_f[0], b_dw_f[0],
                                          w_down_bf, seq, tm_f)
    v_p = z[:, 2 * d_conv + d_attn + kv_w:2 * d_conv + d_attn + 2 * kv_w]
    fw = w_dw_f.shape[1]
    last_tile = (jnp.arange(n_b) + 1) * (seq // tm_f) - 1
    new_ffn_p = jnp.concatenate([u_tail_g[last_tile], u_tail_v[last_tile]], axis=-1)[:, SUBLANES - (fw - 1):]

    xs = x_sample.reshape(ms, d)
    zs, zts = _in_proj(xs, norm1[0], sc1s, sh1s, w_in_t, n_main, ms)
    conv_os, new_conv_s = _conv_sample(zs, state_conv[0], d_conv, w_dw_a[0], b_dw_a[0], gn_g[0], gn_b[0])
    pos_s = jnp.tile(n_past + jnp.arange(t_new, dtype=I32), ms // t_new)
    tabs_s = _rope_tables(pos_s)
    q_hm_s, qi_hm_s, k_s, k_bf_s, _, ki_s, ki_bf_s = _qk_epilogue(zs, zts, tabs_s, g_q[0], g_k[0], 1, ms, n_heads)
    v_s = zs[:, 2 * d_conv + d_attn + kv_w:2 * d_conv + d_attn + 2 * kv_w]
    seq_rows = lambda a, nh: (a.reshape(ms // QB, nh, QB // t_new, t_new, a.shape[-1])
                              .transpose(0, 2, 1, 3, 4).reshape(n_s, nh * t_new, a.shape[-1]))
    qi_s = seq_rows(qi_hm_s, N_IDX_HEADS)
    q_s = seq_rows(q_hm_s, n_heads)
    w_s = (zts[:, IDX_DIM:IDX_DIM + N_IDX_HEADS].reshape(n_s, t_new, N_IDX_HEADS)
           .transpose(0, 2, 1).reshape(n_s, N_IDX_HEADS * t_new, 1))
    ki_new_t = jnp.pad(ki_bf_s.reshape(n_s, t_new, IDX_DIM).transpose(0, 2, 1), ((0, 0), (0, 0), (0, page - t_new)))
    kv_new = lambda a: jnp.pad(a.reshape(n_s, t_new * N_KV_HEADS, head_dim),
                               ((0, 0), (0, (page - t_new) * N_KV_HEADS), (0, 0)))
    kidx_t = jnp.swapaxes(cache_kidx[0], 1, 2)
    sp, sn = _sample_scores(page_table, qi_s, w_s, ki_new_t, kidx_t, t_new)
    n_sel_s = min(TOPK_MAX, (n_past + t_new) // 4)
    mask = _sample_thresh(sp.reshape(ms, n_past), sn.reshape(ms, page), n_sel_s)
    o_s = _sample_attn(page_table, q_s, mask.reshape(n_s, t_new, -1), kv_new(k_bf_s), kv_new(v_s.astype(BF16)),
                       cache_k[0].reshape(n_pool, page * N_KV_HEADS, head_dim),
                       cache_v[0].reshape(n_pool, page * N_KV_HEADS, head_dim), t_new)
    attn_s = o_s.reshape(n_s, n_heads, t_new, head_dim).transpose(0, 2, 1, 3).reshape(ms, d_attn)
    x1s = _merge(xs, gt1s, conv_os, attn_s, g_o[0], w_out[0],ms)
    u_s = _normmod_matmul(x1s, norm2[0], sc2s, sh2s, w_up_bf, w_up.shape[-1], ms, 512, "up_proj")
    u_hist = jnp.concatenate([state_ffn[0], u_s.reshape(n_s, t_new, -1)], axis=1)
    shifted = lambda k: u_hist[:, k:k + t_new].reshape(ms, -1)
    y_s = _ffn_sample(shifted(2), shifted(1), shifted(0), x1s, gt2s, w_dw_f[0], b_dw_f[0], w_down_bf)
    new_ffn_s = u_hist[:, t_new:]

    return (y_p.reshape(n_b, seq, d), y_s.reshape(n_s, t_new, d),
            k_p.reshape(1, n_b, seq, N_KV_HEADS, head_dim), v_p.reshape(1, n_b, seq, N_KV_HEADS, head_dim),
            ki_p.reshape(1, n_b, seq, IDX_DIM), new_conv_p[None], new_ffn_p[None],
            k_s.reshape(1, n_s, t_new, N_KV_HEADS, head_dim), v_s.reshape(1, n_s, t_new, N_KV_HEADS, head_dim),
            ki_s.reshape(1, n_s, t_new, IDX_DIM), new_conv_s[None], new_ffn_s[None])
```

```python
import functools

import jax
import jax.numpy as jnp
from jax import lax
from jax.experimental import pallas as pl
from jax.experimental.pallas import tpu as pltpu

F32 = jnp.float32
BF16 = jnp.bfloat16
I32 = jnp.int32

CONV_GROUPS = 8
N_KV_HEADS = 2
N_IDX_HEADS = 16
IDX_DIM = 64
TOPK_MAX = 256
ROPE_THETA = 10000.0
EPS = 1e-6
IDX_SCALE = (IDX_DIM ** -0.5) * (N_IDX_HEADS ** -0.5)

LANES = 128
SUBLANES = 8
QB = 128
VMEM_LIMIT_BYTES = 56 * 1024 * 1024
NEG_BIG = -1e30
M_INIT = -1e29
LOG2_E = 1.4426950408889634
INT_MIN = -2 ** 31
INT_MAX = 2 ** 31 - 1
KEY_NEG_INF = -2139095041


def _cparams(*sem):
    return pltpu.CompilerParams(dimension_semantics=sem, vmem_limit_bytes=VMEM_LIMIT_BYTES)


def _silu(x):
    return x * jax.nn.sigmoid(x)


def _order_key(x):
    bits = pltpu.bitcast(x, I32)
    return bits ^ ((bits >> 31) & INT_MAX)


def _ada_kernel(c_ref, w_ref, b_ref, o_ref):
    s = _silu(c_ref[...]).astype(BF16)
    o_ref[...] = jnp.dot(s, w_ref[...].astype(BF16), preferred_element_type=F32) + b_ref[...]


def _ada(c_all, w_ada, b_ada, tn=1024):
    r, d = c_all.shape
    n = w_ada.shape[1]
    return pl.pallas_call(
        _ada_kernel,
        grid=(n // tn,),
        in_specs=[pl.BlockSpec((r, d), lambda j: (0, 0)),
                  pl.BlockSpec((d, tn), lambda j: (0, j)),
                  pl.BlockSpec((1, tn), lambda j: (0, j))],
        out_specs=pl.BlockSpec((r, tn), lambda j: (0, j)),
        out_shape=jax.ShapeDtypeStruct((r, n), F32),
        compiler_params=_cparams("arbitrary"),
        name="ada",
    )(c_all, w_ada, b_ada.reshape(1, n))


_CONTRACT_LAST = (((1,), (1,)), ((), ()))


NORM_CHUNK = 128


def _normmod_to(h_ref, dst0, x_ref, g_ref, sc_ref, sh_ref):
    rows = x_ref.shape[0]
    per_row = sc_ref.shape[1] != 1
    chunk = min(NORM_CHUNK, rows)
    for r0 in range(0, rows, chunk):
        x = x_ref[r0:r0 + chunk, :]
        sc = sc_ref[0, r0:r0 + chunk, :] if per_row else sc_ref[0]
        sh = sh_ref[0, r0:r0 + chunk, :] if per_row else sh_ref[0]
        y = x * lax.rsqrt(jnp.mean(x * x, axis=-1, keepdims=True) + EPS) * g_ref[...]
        h_ref[dst0 + r0:dst0 + r0 + chunk, :] = (y * (1.0 + sc) + sh).astype(BF16)


def _normmod_matmul_kernel(x_ref, g_ref, sc_ref, sh_ref, w_ref, *refs, w_transposed, has_tail):
    if has_tail:
        wt_ref, o_ref, ot_ref, h_ref = refs
    else:
        o_ref, h_ref = refs

    @pl.when(pl.program_id(1) == 0)
    def _():
        _normmod_to(h_ref, 0, x_ref, g_ref, sc_ref, sh_ref)
        if has_tail:
            ot_ref[...] = lax.dot_general(h_ref[...], wt_ref[...].astype(BF16), _CONTRACT_LAST,
                                          preferred_element_type=F32)

    w = w_ref[...].astype(BF16)
    if w_transposed:
        o_ref[...] = lax.dot_general(h_ref[...], w, _CONTRACT_LAST, preferred_element_type=F32)
    else:
        o_ref[...] = jnp.dot(h_ref[...], w, preferred_element_type=F32)


def _normmod_matmul(x, gain, sc, sh, w, ncols, tm, tn, name, w_transposed=False, w_tail=None):
    m, d = x.shape
    g, r, _ = sc.shape
    tiles_per_group = (m // tm) // g
    mod_spec = pl.BlockSpec((1, r, d), lambda i, j: (i // tiles_per_group, 0, 0))
    w_spec = pl.BlockSpec((tn, d), lambda i, j: (j, 0)) if w_transposed else pl.BlockSpec((d, tn), lambda i, j: (0, j))
    in_specs = [pl.BlockSpec((tm, d), lambda i, j: (i, 0)),
                pl.BlockSpec((1, d), lambda i, j: (0, 0)),
                mod_spec, mod_spec, w_spec]
    out_specs = [pl.BlockSpec((tm, tn), lambda i, j: (i, j))]
    out_shape = [jax.ShapeDtypeStruct((m, ncols), F32)]
    operands = [x, gain.reshape(1, d), sc, sh, w]
    if w_tail is not None:
        nt = w_tail.shape[0]
        in_specs.append(pl.BlockSpec((nt, d), lambda i, j: (0, 0)))
        out_specs.append(pl.BlockSpec((tm, nt), lambda i, j: (i, 0)))
        out_shape.append(jax.ShapeDtypeStruct((m, nt), F32))
        operands.append(w_tail)
    out = pl.pallas_call(
        functools.partial(_normmod_matmul_kernel, w_transposed=w_transposed, has_tail=w_tail is not None),
        grid=(m // tm, ncols // tn),
        in_specs=in_specs,
        out_specs=out_specs,
        out_shape=out_shape,
        scratch_shapes=[pltpu.VMEM((tm, d), BF16)],
        compiler_params=_cparams("arbitrary", "arbitrary"),
        name=name,
    )(*operands)
    return out if w_tail is not None else out[0]


def _conv_gn_silu(hist_ref, off, rows, wdw_ref, bdw_ref, gng_ref, gnb_ref, o_ref, row_chunk):
    conv_w = wdw_ref.shape[0]
    for r0 in range(0, rows, row_chunk):
        for c in range(CONV_GROUPS):
            cs = slice(c * LANES, (c + 1) * LANES)
            acc = jnp.zeros((row_chunk, LANES), F32)
            for r in range(SUBLANES):
                n_rows = row_chunk if r == 0 else row_chunk + SUBLANES
                part = jnp.zeros((n_rows, LANES), F32)
                for w in range((r - off) % SUBLANES, conv_w, SUBLANES):
                    a0 = off + r0 + w - r
                    part = part + hist_ref[a0:a0 + n_rows, cs] * wdw_ref[w:w + 1, cs]
                acc = acc + part[r:r + row_chunk]
            y = acc + bdw_ref[:, cs]
            mu = jnp.mean(y, axis=-1, keepdims=True)
            dlt = y - mu
            var = jnp.mean(dlt * dlt, axis=-1, keepdims=True)
            yn = dlt * lax.rsqrt(var + EPS) * gng_ref[:, cs] + gnb_ref[:, cs]
            o_ref[r0:r0 + row_chunk, cs] = _silu(yn).astype(o_ref.dtype)


HALO = 32


def _conv_prompt_kernel(za_ref, zg_ref, ha_ref, hg_ref, wdw_ref, bdw_ref, gng_ref, gnb_ref,
                        o_ref, newc_ref, hist_ref, *, tm, tiles_per_seq):
    i = pl.program_id(0)
    first = (i % tiles_per_seq) == 0
    a_halo = ha_ref[...] * jax.nn.sigmoid(hg_ref[...])
    hist_ref[0:HALO, :] = jnp.where(first, 0.0, a_halo)
    hist_ref[HALO:HALO + tm, :] = za_ref[...] * jax.nn.sigmoid(zg_ref[...])
    conv_w = wdw_ref.shape[0]
    _conv_gn_silu(hist_ref, HALO - (conv_w - 1), tm, wdw_ref, bdw_ref, gng_ref, gnb_ref, o_ref, 64)

    @pl.when((i % tiles_per_seq) == tiles_per_seq - 1)
    def _():
        newc_ref[0] = hist_ref[HALO + tm - (conv_w - 1):HALO + tm, :]


def _conv_prompt(z, n_seq, seq, d_conv, wdw, bdw, gng, gnb, tm=256):
    m = z.shape[0]
    conv_w = wdw.shape[0]
    tiles_per_seq = seq // tm
    cb = 1
    halo_idx = lambda i: jnp.maximum(i * (tm // HALO) - 1, 0)
    vec = pl.BlockSpec((1, d_conv), lambda i: (0, 0))
    return pl.pallas_call(
        functools.partial(_conv_prompt_kernel, tm=tm, tiles_per_seq=tiles_per_seq),
        grid=(m // tm,),
        in_specs=[pl.BlockSpec((tm, d_conv), lambda i: (i, 0)),
                  pl.BlockSpec((tm, d_conv), lambda i: (i, cb)),
                  pl.BlockSpec((HALO, d_conv), lambda i: (halo_idx(i), 0)),
                  pl.BlockSpec((HALO, d_conv), lambda i: (halo_idx(i), cb)),
                  pl.BlockSpec((conv_w, d_conv), lambda i: (0, 0)),
                  vec, vec, vec],
        out_specs=[pl.BlockSpec((tm, d_conv), lambda i: (i, 0)),
                   pl.BlockSpec((1, conv_w - 1, d_conv), lambda i: (i // tiles_per_seq, 0, 0))],
        out_shape=[jax.ShapeDtypeStruct((m, d_conv), BF16),
                   jax.ShapeDtypeStruct((n_seq, conv_w - 1, d_conv), F32)],
        scratch_shapes=[pltpu.VMEM((HALO + tm, d_conv), F32)],
        compiler_params=_cparams("arbitrary"),
        name="conv_prompt",
    )(z, z, z, z, wdw, bdw.reshape(1, -1), gng.reshape(1, -1), gnb.reshape(1, -1))


def _conv_sample_kernel(za_ref, zg_ref, st_ref, wdw_ref, bdw_ref, gng_ref, gnb_ref,
                        o_ref, newc_ref, hist_ref, *, t_new):
    conv_w = wdw_ref.shape[0]
    hist_ref[0:conv_w - 1, :] = st_ref[0]
    hist_ref[conv_w - 1:conv_w - 1 + t_new, :] = za_ref[...] * jax.nn.sigmoid(zg_ref[...])
    _conv_gn_silu(hist_ref, 0, t_new, wdw_ref, bdw_ref, gng_ref, gnb_ref, o_ref, t_new)
    newc_ref[0] = hist_ref[t_new:t_new + conv_w - 1, :]


def _conv_sample(z, state, d_conv, wdw, bdw, gng, gnb):
    n_seq, hist_rows, _ = state.shape
    conv_w = wdw.shape[0]
    m = z.shape[0]
    t_new = m // n_seq
    vec = pl.BlockSpec((1, d_conv), lambda b: (0, 0))
    return pl.pallas_call(
        functools.partial(_conv_sample_kernel, t_new=t_new),
        grid=(n_seq,),
        in_specs=[pl.BlockSpec((t_new, d_conv), lambda b: (b, 0)),
                  pl.BlockSpec((t_new, d_conv), lambda b: (b, 1)),
                  pl.BlockSpec((1, hist_rows, d_conv), lambda b: (b, 0, 0)),
                  pl.BlockSpec((conv_w, d_conv), lambda b: (0, 0)),
                  vec, vec, vec],
        out_specs=[pl.BlockSpec((t_new, d_conv), lambda b: (b, 0)),
                   pl.BlockSpec((1, hist_rows, d_conv), lambda b: (b, 0, 0))],
        out_shape=[jax.ShapeDtypeStruct((m, d_conv), F32),
                   jax.ShapeDtypeStruct((n_seq, hist_rows, d_conv), F32)],
        scratch_shapes=[pltpu.VMEM((hist_rows + t_new + SUBLANES, d_conv), F32)],
        compiler_params=_cparams("arbitrary"),
        name="conv_sample",
    )(z, z, state, wdw, bdw.reshape(1, -1), gng.reshape(1, -1), gnb.reshape(1, -1))


def _qk_kernel(zq_ref, zkv_ref, zqi0_ref, zqi1_ref, zt_ref, cos_ref, sin_ref, cosi_ref, sini_ref,
               gq_ref, gk_ref, qhm_ref, qihm_ref, k_ref, kbf_ref, vt_ref, ki_ref, kibf_ref, *, n_heads):
    cos, sin = cos_ref[...], sin_ref[...]
    cosi, sini = cosi_ref[...], sini_ref[...]
    tm = cos.shape[0]
    lane = lax.broadcasted_iota(I32, (tm, LANES), 1)
    low_half = (lane % IDX_DIM) < (IDX_DIM // 2)

    def norm_rope(x, g):
        y = x * lax.rsqrt(jnp.mean(x * x, axis=-1, keepdims=True) + EPS) * g
        return y * cos + pltpu.roll(y, LANES // 2, 1) * sin

    def rope_idx(x):
        r = jnp.where(low_half, pltpu.roll(x, LANES - IDX_DIM // 2, 1), pltpu.roll(x, IDX_DIM // 2, 1))
        return x * cosi + r * sini

    gq, gk = gq_ref[...], gk_ref[...]
    for h in range(n_heads):
        qhm_ref[0, h] = norm_rope(zq_ref[:, h * LANES:(h + 1) * LANES], gq).astype(BF16)
    kv_w = N_KV_HEADS * LANES
    for g in range(N_KV_HEADS):
        kg = norm_rope(zkv_ref[:, g * LANES:(g + 1) * LANES], gk)
        k_ref[:, g * LANES:(g + 1) * LANES] = kg
        kbf_ref[:, g * LANES:(g + 1) * LANES] = kg.astype(BF16)
    vt_ref[0] = zkv_ref[:, kv_w:2 * kv_w].T.astype(BF16)
    half = (N_IDX_HEADS * IDX_DIM) // 2
    for j in range(N_IDX_HEADS // 2):
        src = zqi0_ref if j * LANES < half else zqi1_ref
        c0 = (j * LANES) % half
        y = rope_idx(src[:, c0:c0 + LANES])
        qihm_ref[0, 2 * j] = y[:, :IDX_DIM].astype(BF16)
        qihm_ref[0, 2 * j + 1] = y[:, IDX_DIM:].astype(BF16)
    yk = rope_idx(zt_ref[...])[:, :IDX_DIM]
    ki_ref[...] = yk
    kibf_ref[...] = yk.astype(BF16)


def _qk_epilogue(z, zt, tabs, gq, gk, n_seq, seq, n_heads):
    m = z.shape[0]
    tm = QB
    head_dim = LANES
    cos, sin, cosi, sini = tabs
    tab_tiles = cos.shape[0] // tm
    kv_w = N_KV_HEADS * head_dim
    d_attn = n_heads * head_dim
    d_conv = d_attn
    q_cb = (2 * d_conv) // d_attn
    kv_cb = (2 * d_conv + d_attn) // (2 * kv_w)
    qi_w = (N_IDX_HEADS * IDX_DIM) // 2
    qi_cb = (2 * d_conv + d_attn + 2 * kv_w) // qi_w
    tiles_per_seq = seq // tm
    tab = pl.BlockSpec((tm, LANES), lambda i: (i % tab_tiles, 0))
    vec = pl.BlockSpec((1, LANES), lambda i: (0, 0))
    return pl.pallas_call(
        functools.partial(_qk_kernel, n_heads=n_heads),
        grid=(m // tm,),
        in_specs=[pl.BlockSpec((tm, d_attn), lambda i: (i, q_cb)),
                  pl.BlockSpec((tm, 2 * kv_w), lambda i: (i, kv_cb)),
                  pl.BlockSpec((tm, qi_w), lambda i: (i, qi_cb)),
                  pl.BlockSpec((tm, qi_w), lambda i: (i, qi_cb + 1)),
                  pl.BlockSpec((tm, LANES), lambda i: (i, 0)),
                  tab, tab, tab, tab, vec, vec],
        out_specs=[pl.BlockSpec((1, n_heads, tm, head_dim), lambda i: (i, 0, 0, 0)),
                   pl.BlockSpec((1, N_IDX_HEADS, tm, IDX_DIM), lambda i: (i, 0, 0, 0)),
                   pl.BlockSpec((tm, kv_w), lambda i: (i, 0)),
                   pl.BlockSpec((tm, kv_w), lambda i: (i, 0)),
                   pl.BlockSpec((1, kv_w, tm), lambda i: (i // tiles_per_seq, 0, i % tiles_per_seq)),
                   pl.BlockSpec((tm, IDX_DIM), lambda i: (i, 0)),
                   pl.BlockSpec((tm, IDX_DIM), lambda i: (i, 0))],
        out_shape=[jax.ShapeDtypeStruct((m // tm, n_heads, tm, head_dim), BF16),
                   jax.ShapeDtypeStruct((m // tm, N_IDX_HEADS, tm, IDX_DIM), BF16),
                   jax.ShapeDtypeStruct((m, kv_w), F32),
                   jax.ShapeDtypeStruct((m, kv_w), BF16),
                   jax.ShapeDtypeStruct((n_seq, kv_w, seq), BF16),
                   jax.ShapeDtypeStruct((m, IDX_DIM), F32),
                   jax.ShapeDtypeStruct((m, IDX_DIM), BF16)],
        compiler_params=_cparams("arbitrary"),
        name="qk_epilogue",
    )(z, z, z, z, zt, cos, sin, cosi, sini, gq.reshape(1, -1), gk.reshape(1, -1))


def _kth_largest_key(lo, hi, count_ge, n_sel):
    n_bits = jnp.max(32 - lax.clz(hi - lo))

    def step(it, thr):
        cand = thr + lax.shift_left(jnp.int32(1), n_bits - 1 - it)
        keep = count_ge(cand) >= n_sel
        return jnp.where(cand > thr, jnp.where(keep, cand, thr), thr)

    return lax.fori_loop(0, n_bits, step, lo)


def _pipelined_pairs(n_pairs, produce, consume):
    produce(0, 0)

    def body(p, carry):
        produce(2 * p + 1, 1)
        consume(2 * p, 0)
        produce(2 * p + 2, 0)
        consume(2 * p + 1, 1)
        return carry

    lax.fori_loop(0, n_pairs - 1, body, 0)
    last = 2 * (n_pairs - 1)
    produce(last + 1, 1)
    consume(last, 0)
    consume(last + 1, 1)


KT = 2 * QB


def _attn_prompt_kernel(qi_ref, w_ref, ki_ref, q_ref, k_ref, vt_ref, o_ref,
                        key_ref, m_ref, l_ref, acc_ref, cut_ref, lg_ref, s_ref, gmax_ref, *, n_sel, c_exp, n_heads,
                        idx_bits):
    i = pl.program_id(1)
    n_kt = lax.div(i * QB + QB + KT - 1, KT)
    row_iota = lax.broadcasted_iota(I32, (KT, QB), 0)
    w_all = w_ref[0] * IDX_SCALE
    heads_per_kv = n_heads // N_KV_HEADS
    hc = 4

    n_pairs = lax.shift_right_logical(n_kt + 1, 1)

    def score_products(kt, slot):
        ki_t = ki_ref[0, pl.ds(pl.multiple_of(kt * KT, KT), KT), :]
        for h0 in range(0, N_IDX_HEADS, hc):
            s_ref[slot, :, h0 * QB:(h0 + hc) * QB] = lax.dot_general(
                ki_t, qi_ref[0, h0:h0 + hc].reshape(hc * QB, IDX_DIM), _CONTRACT_LAST,
                preferred_element_type=F32)

    def score_keys(kt, slot):
        for half in range(KT // QB):
            rs = slice(half * QB, (half + 1) * QB)
            ks = pl.multiple_of(kt * KT + half * QB, QB)
            acc = jnp.zeros((QB, QB), F32)
            for h in range(N_IDX_HEADS):
                acc = acc + jnp.maximum(s_ref[slot, rs, h * QB:(h + 1) * QB], 0.0) * w_all[h:h + 1, :]
            acc = jnp.where(ks + lax.broadcasted_iota(I32, (QB, QB), 0) <=
                            i * QB + lax.broadcasted_iota(I32, (QB, QB), 1), acc, -jnp.inf)
            key = _order_key(acc)
            key_ref[pl.ds(ks, QB), :] = key
            gmax_ref[rs, :] = jnp.maximum(gmax_ref[rs, :], key)

    gmax_ref[...] = jnp.full(gmax_ref.shape, INT_MIN, I32)
    _pipelined_pairs(n_pairs, score_products, score_keys)

    def count(pred):
        def tile(kt, cnt):
            ks = pl.multiple_of(kt * KT, KT)
            hit = jnp.where(pred(key_ref[pl.ds(ks, KT), :], ks + row_iota), 1, 0)
            return cnt + jnp.sum(hit.reshape(KT // SUBLANES, SUBLANES, QB), axis=0)
        n_pairs = lax.shift_right_logical(n_kt, 1)
        cnt = lax.fori_loop(0, n_pairs, lambda p, cnt: tile(2 * p + 1, tile(2 * p, cnt)),
                            jnp.zeros((SUBLANES, QB), I32))
        cnt = lax.fori_loop(2 * n_pairs, n_kt, tile, cnt)
        return jnp.sum(cnt, axis=0, keepdims=True)

    gmax = gmax_ref[...]
    thr = _kth_largest_key(jnp.min(gmax, axis=0, keepdims=True), jnp.max(gmax, axis=0, keepdims=True),
                           lambda cand: count(lambda kk, pos: kk >= cand), n_sel)
    cnt_ge = count(lambda kk, pos: kk >= thr)
    need = n_sel - count(lambda kk, pos: kk > thr)
    cut_ref[...] = jnp.full((1, QB), INT_MAX, I32)

    @pl.when(jnp.max(cnt_ge) > n_sel)
    def _():
        def search_pos(it, p):
            cand = p + lax.shift_left(jnp.int32(1), idx_bits - 1 - it)
            c = count(lambda kk, pos: jnp.where(kk == thr, pos, INT_MAX) < cand)
            return jnp.where(c < need, cand, p)
        cut_ref[...] = lax.fori_loop(0, idx_bits, search_pos, jnp.zeros((1, QB), I32))

    few = thr <= KEY_NEG_INF
    thr_eff = jnp.where(few, KEY_NEG_INF + 1, thr)
    cut = jnp.where(few, INT_MAX, cut_ref[...])

    m_ref[...] = jnp.full(m_ref.shape, M_INIT, F32)
    l_ref[...] = jnp.zeros(l_ref.shape, F32)
    acc_ref[...] = jnp.zeros(acc_ref.shape, F32)

    def qk_products(kt, slot):
        ks = pl.multiple_of(kt * KT, KT)
        for g in range(N_KV_HEADS):
            k_t = k_ref[0, pl.ds(ks, KT), g * LANES:(g + 1) * LANES]
            q_g = q_ref[0, g * heads_per_kv:(g + 1) * heads_per_kv].reshape(heads_per_kv * QB, LANES)
            lg_ref[slot, :, g * heads_per_kv * QB:(g + 1) * heads_per_kv * QB] = lax.dot_general(
                k_t, q_g, _CONTRACT_LAST, preferred_element_type=F32)

    def attn_tile(kt, slot):
        ks = pl.multiple_of(kt * KT, KT)
        kk = key_ref[pl.ds(ks, KT), :]
        sel = (kk - jnp.where(ks + row_iota > cut, 1, 0)) >= thr_eff
        m_old, l_old = m_ref[...], l_ref[...]
        m_parts, l_parts = [], []
        for g in range(N_KV_HEADS):
            v_t = vt_ref[0, g * LANES:(g + 1) * LANES, pl.ds(ks, KT)]
            for hh in range(heads_per_kv):
                cs = slice((g * heads_per_kv + hh) * QB, (g * heads_per_kv + hh + 1) * QB)
                raw = jnp.where(sel, lg_ref[slot, :, cs], NEG_BIG)
                m_new = jnp.maximum(m_old[:, cs], jnp.max(raw, axis=0, keepdims=True))
                alpha = jnp.exp2((m_old[:, cs] - m_new) * c_exp)
                p = jnp.exp2((raw - m_new) * c_exp)
                m_parts.append(m_new)
                l_parts.append(alpha * l_old[:, cs] + jnp.sum(p, axis=0, keepdims=True))
                acc_ref[:, cs] = alpha * acc_ref[:, cs] + jnp.dot(v_t, p.astype(BF16),
                                                                   preferred_element_type=F32)
        m_ref[...] = jnp.concatenate(m_parts, axis=1)
        l_ref[...] = jnp.concatenate(l_parts, axis=1)

    _pipelined_pairs(n_pairs, qk_products, attn_tile)
    for h in range(n_heads):
        cs = slice(h * QB, (h + 1) * QB)
        o_ref[:, cs] = (acc_ref[:, cs] / l_ref[:, cs]).T


def _attn_prompt(qi_hm, w_hm, ki_bf, q_hm, k_bf, vt_bf, n_seq, seq, n_sel):
    n_heads = q_hm.shape[1]
    head_dim = q_hm.shape[3]
    nblk = seq // QB
    kv_w = k_bf.shape[-1]
    assert n_sel <= KT and seq % (2 * KT) == 0
    blk = lambda b, i: (b * nblk + i, 0, 0, 0)
    return pl.pallas_call(
        functools.partial(_attn_prompt_kernel, n_sel=n_sel, c_exp=head_dim ** -0.5 * LOG2_E, n_heads=n_heads,
                          idx_bits=int(seq).bit_length()),
        grid=(n_seq, nblk),
        in_specs=[pl.BlockSpec((1, N_IDX_HEADS, QB, IDX_DIM), blk),
                  pl.BlockSpec((1, N_IDX_HEADS, QB), lambda b, i: (b * nblk + i, 0, 0)),
                  pl.BlockSpec((1, seq, IDX_DIM), lambda b, i: (b, 0, 0)),
                  pl.BlockSpec((1, n_heads, QB, head_dim), blk),
                  pl.BlockSpec((1, seq, kv_w), lambda b, i: (b, 0, 0)),
                  pl.BlockSpec((1, kv_w, seq), lambda b, i: (b, 0, 0))],
        out_specs=pl.BlockSpec((QB, n_heads * head_dim), lambda b, i: (b * nblk + i, 0)),
        out_shape=jax.ShapeDtypeStruct((n_seq * seq, n_heads * head_dim), F32),
        scratch_shapes=[pltpu.VMEM((seq, QB), I32),
                        pltpu.VMEM((1, n_heads * QB), F32),
                        pltpu.VMEM((1, n_heads * QB), F32),
                        pltpu.VMEM((head_dim, n_heads * QB), F32),
                        pltpu.VMEM((1, QB), I32),
                        pltpu.VMEM((2, KT, n_heads * QB), F32),
                        pltpu.VMEM((2, KT, N_IDX_HEADS * QB), F32),
                        pltpu.VMEM((KT, QB), I32)],
        compiler_params=_cparams("arbitrary", "arbitrary"),
        name="attn_prompt",
    )(qi_hm, w_hm, ki_bf.reshape(n_seq, seq, IDX_DIM), q_hm, k_bf.reshape(n_seq, seq, kv_w), vt_bf)


SCORE_PAGES = 32
ATTN_PAGES = 16
ATTN_SEQS = 2


def _sample_score_kernel(pt_ref, qi_ref, w_ref, kin_ref, *refs, t_new):
    pages, (sp_ref, sn_ref) = refs[:SCORE_PAGES], refs[SCORE_PAGES:]
    page = pages[0].shape[2]
    qi = qi_ref[0]
    rows = qi.shape[0]
    wb = jnp.broadcast_to(w_ref[0] * IDX_SCALE, (rows, page))

    def score(keys_t_bf):
        s = jnp.dot(qi, keys_t_bf, preferred_element_type=F32)
        r = jnp.maximum(s, 0.0) * wb
        return jnp.sum(r.reshape(N_IDX_HEADS, t_new, page), axis=0)

    for j in range(SCORE_PAGES):
        sp_ref[0, :, j * page:(j + 1) * page] = score(pages[j][0].astype(BF16))

    @pl.when(pl.program_id(1) == 0)
    def _():
        sn = score(kin_ref[0])
        s_idx = lax.broadcasted_iota(I32, (t_new, page), 1)
        t_idx = lax.broadcasted_iota(I32, (t_new, page), 0)
        sn_ref[0] = jnp.where(s_idx <= t_idx, sn, -jnp.inf)


def _sample_scores(page_table, qi_s, w_s, ki_new_t, cache_kidx_t, t_new):
    n_seq, n_pages = page_table.shape
    page = cache_kidx_t.shape[2]
    rows = qi_s.shape[1]
    page_specs = [pl.BlockSpec((1, IDX_DIM, page),
                               functools.partial(lambda b, c, pt, j: (pt[b, c * SCORE_PAGES + j], 0, 0), j=j))
                  for j in range(SCORE_PAGES)]
    grid_spec = pltpu.PrefetchScalarGridSpec(
        num_scalar_prefetch=1,
        grid=(n_seq, n_pages // SCORE_PAGES),
        in_specs=[pl.BlockSpec((1, rows, IDX_DIM), lambda b, c, pt: (b, 0, 0)),
                  pl.BlockSpec((1, rows, 1), lambda b, c, pt: (b, 0, 0)),
                  pl.BlockSpec((1, IDX_DIM, page), lambda b, c, pt: (b, 0, 0))] + page_specs,
        out_specs=[pl.BlockSpec((1, t_new, SCORE_PAGES * page), lambda b, c, pt: (b, 0, c)),
                   pl.BlockSpec((1, t_new, page), lambda b, c, pt: (b, 0, 0))],
    )
    return pl.pallas_call(
        functools.partial(_sample_score_kernel, t_new=t_new),
        grid_spec=grid_spec,
        out_shape=[jax.ShapeDtypeStruct((n_seq, t_new, n_pages * page), F32),
                   jax.ShapeDtypeStruct((n_seq, t_new, page), F32)],
        compiler_params=_cparams("arbitrary", "arbitrary"),
        name="sample_scores",
    )(page_table, qi_s, w_s, ki_new_t, *([cache_kidx_t] * SCORE_PAGES))


def _sample_thresh_kernel(sp_ref, sn_ref, ex_ref, mask_ref, key_ref, cut_ref, *, n_sel, n_past, idx_bits):
    rows = sp_ref.shape[0]
    n_tiles = n_past // LANES + 1
    key_ref[:, 0:n_past] = _order_key(sp_ref[...])
    key_ref[:, n_past:n_past + LANES] = _order_key(sn_ref[...])
    lane = lax.broadcasted_iota(I32, (rows, LANES), 1)

    def count(pred):
        def body(j, cnt):
            c0 = pl.multiple_of(j * LANES, LANES)
            return cnt + jnp.where(pred(key_ref[:, pl.ds(c0, LANES)], c0 + lane), 1, 0)
        cnt = lax.fori_loop(0, n_tiles, body, jnp.zeros((rows, LANES), I32), unroll=8)
        return jnp.broadcast_to(jnp.sum(cnt, axis=1, keepdims=True), (rows, LANES))

    def class_max(j, carry):
        c0 = pl.multiple_of(j * 2 * LANES, 2 * LANES)
        return (jnp.maximum(carry[0], key_ref[:, pl.ds(c0, LANES)]),
                jnp.maximum(carry[1], key_ref[:, pl.ds(c0 + LANES, LANES)]))

    floor = jnp.full((rows, LANES), INT_MIN, I32)
    even, odd = lax.fori_loop(0, n_tiles // 2, class_max, (floor, floor), unroll=8)
    if n_tiles % 2:
        even = jnp.maximum(even, key_ref[:, (n_tiles - 1) * LANES:n_tiles * LANES])
    lo = jnp.broadcast_to(jnp.min(jnp.minimum(even, odd), axis=1, keepdims=True), (rows, LANES))
    hi = jnp.broadcast_to(jnp.max(jnp.maximum(even, odd), axis=1, keepdims=True), (rows, LANES))
    thr = _kth_largest_key(lo, hi, lambda cand: count(lambda kk, pos: kk >= cand), n_sel)
    cnt_ge = count(lambda kk, pos: kk >= thr)
    need = n_sel - count(lambda kk, pos: kk > thr)
    cut_ref[...] = jnp.full((rows, LANES), INT_MAX, I32)

    @pl.when(jnp.max(cnt_ge) > n_sel)
    def _():
        def search_pos(it, p):
            cand = p + lax.shift_left(jnp.int32(1), idx_bits - 1 - it)
            c = count(lambda kk, pos: jnp.where(kk == thr, pos, INT_MAX) < cand)
            return jnp.where(c < need, cand, p)
        cut_ref[...] = lax.fori_loop(0, idx_bits, search_pos, jnp.zeros((rows, LANES), I32))

    few = thr <= KEY_NEG_INF
    thr_eff = jnp.where(few, KEY_NEG_INF + 1, thr)
    cut = jnp.where(few, INT_MAX, cut_ref[...])

    ex = ex_ref[...]
    width = ex.shape[1]

    def emit(j, carry):
        c0 = pl.multiple_of(j * LANES, LANES)
        hit = jnp.where((key_ref[:, pl.ds(c0, LANES)] - jnp.where(c0 + lane > cut, 1, 0)) >= thr_eff, 1.0, 0.0)
        mask_ref[:, pl.ds(pl.multiple_of(j * width, width), width)] = jnp.dot(
            hit.astype(BF16), ex, preferred_element_type=F32)
        return carry

    lax.fori_loop(0, n_tiles, emit, 0, unroll=4)


def _sample_thresh(sp, sn, n_sel, rows_per_step=64):
    m, n_past = sp.shape
    assert n_sel <= 2 * LANES and n_past >= LANES
    rows_per_step = min(rows_per_step, m)
    tok = jnp.arange(LANES, dtype=I32)[:, None]
    col = jnp.arange(LANES * N_KV_HEADS, dtype=I32)[None, :]
    expand = (col // N_KV_HEADS == tok).astype(BF16)
    width = (n_past + LANES) * N_KV_HEADS
    return pl.pallas_call(
        functools.partial(_sample_thresh_kernel, n_sel=n_sel, n_past=n_past,
                          idx_bits=int(n_past + LANES).bit_length()),
        grid=(m // rows_per_step,),
        in_specs=[pl.BlockSpec((rows_per_step, n_past), lambda r: (r, 0)),
                  pl.BlockSpec((rows_per_step, LANES), lambda r: (r, 0)),
                  pl.BlockSpec(expand.shape, lambda r: (0, 0))],
        out_specs=pl.BlockSpec((rows_per_step, width), lambda r: (r, 0)),
        out_shape=jax.ShapeDtypeStruct((m, width), F32),
        scratch_shapes=[pltpu.VMEM((rows_per_step, n_past + LANES), I32),
                        pltpu.VMEM((rows_per_step, LANES), I32)],
        compiler_params=_cparams("arbitrary"),
        name="sample_thresh",
    )(sp, sn, expand)


def _sample_attn_kernel(pt_ref, q_ref, mask_ref, maskn_ref, kn_ref, vn_ref, *refs, c_exp, group_rows):
    n_pg = ATTN_SEQS * ATTN_PAGES
    k_pages, v_pages = refs[:n_pg], refs[n_pg:2 * n_pg]
    o_ref, m_ref, l_ref, acc_ref = refs[2 * n_pg:]
    c = pl.program_id(1)
    rows = q_ref.shape[1]
    width = maskn_ref.shape[2]
    reps = rows // mask_ref.shape[1]
    own_head = jnp.where(lax.broadcasted_iota(I32, (rows, width), 1) % N_KV_HEADS ==
                         lax.broadcasted_iota(I32, (rows, width), 0) // group_rows, 1.0, 0.0)

    @pl.when(c == 0)
    def _():
        m_ref[...] = jnp.full(m_ref.shape, M_INIT, F32)
        l_ref[...] = jnp.zeros(l_ref.shape, F32)
        acc_ref[...] = jnp.zeros(acc_ref.shape, F32)

    def select(flags):
        return jnp.concatenate([flags] * reps, axis=0) * own_head > 0.5

    def attend(s, tiles):
        q = q_ref[s]
        raws = [jnp.where(sel, lax.dot_general(q, k, _CONTRACT_LAST, preferred_element_type=F32), NEG_BIG)
                for sel, k, _ in tiles]
        m_old = m_ref[s]
        m_new = jnp.maximum(m_old, jnp.max(functools.reduce(jnp.maximum, raws), axis=1, keepdims=True))
        alpha = jnp.exp2((m_old - m_new) * c_exp)
        ps = [jnp.exp2((raw - m_new) * c_exp) for raw in raws]
        l_ref[s] = alpha * l_ref[s] + jnp.sum(functools.reduce(jnp.add, ps), axis=1, keepdims=True)
        m_ref[s] = m_new
        pv = functools.reduce(jnp.add, [jnp.dot(p.astype(BF16), v, preferred_element_type=F32)
                                        for p, (_, _, v) in zip(ps, tiles)])
        acc_ref[s] = alpha * acc_ref[s] + pv

    for s in range(ATTN_SEQS):
        attend(s, [(select(mask_ref[s, :, j * width:(j + 1) * width]),
                    k_pages[s * ATTN_PAGES + j][0].astype(BF16), v_pages[s * ATTN_PAGES + j][0].astype(BF16))
                   for j in range(ATTN_PAGES)])

    @pl.when(c == pl.num_programs(1) - 1)
    def _():
        for s in range(ATTN_SEQS):
            attend(s, [(select(maskn_ref[s]), kn_ref[s], vn_ref[s])])
            o_ref[s] = acc_ref[s] / l_ref[s]


def _sample_attn(page_table, q_s, mask, k_new, v_new, cache_k, cache_v, t_new):
    n_seq, n_pages = page_table.shape
    page_rows, head_dim = cache_k.shape[1], cache_k.shape[2]
    rows = q_s.shape[1]
    page_specs = [pl.BlockSpec((1, page_rows, head_dim),
                               functools.partial(lambda b, c, pt, s, j: (pt[b * ATTN_SEQS + s, c * ATTN_PAGES + j], 0, 0),
                                                 s=s, j=j))
                  for s in range(ATTN_SEQS) for j in range(ATTN_PAGES)]
    per_seq = lambda shape: pl.BlockSpec((ATTN_SEQS,) + shape, lambda b, c, pt: (b, 0, 0))
    grid_spec = pltpu.PrefetchScalarGridSpec(
        num_scalar_prefetch=1,
        grid=(n_seq // ATTN_SEQS, n_pages // ATTN_PAGES),
        in_specs=[per_seq((rows, head_dim)),
                  pl.BlockSpec((ATTN_SEQS, t_new, ATTN_PAGES * page_rows), lambda b, c, pt: (b, 0, c)),
                  pl.BlockSpec((ATTN_SEQS, t_new, page_rows), lambda b, c, pt: (b, 0, n_pages)),
                  per_seq((page_rows, head_dim)), per_seq((page_rows, head_dim))] + page_specs + page_specs,
        out_specs=per_seq((rows, head_dim)),
        scratch_shapes=[pltpu.VMEM((ATTN_SEQS, rows, 1), F32),
                        pltpu.VMEM((ATTN_SEQS, rows, 1), F32),
                        pltpu.VMEM((ATTN_SEQS, rows, head_dim), F32)],
    )
    return pl.pallas_call(
        functools.partial(_sample_attn_kernel, c_exp=head_dim ** -0.5 * LOG2_E, group_rows=rows // N_KV_HEADS),
        grid_spec=grid_spec,
        out_shape=jax.ShapeDtypeStruct((n_seq, rows, head_dim), F32),
        compiler_params=_cparams("arbitrary", "arbitrary"),
        name="sample_attn",
    )(page_table, q_s, mask, mask, k_new, v_new,
      *([cache_k] * (ATTN_SEQS * ATTN_PAGES)), *([cache_v] * (ATTN_SEQS * ATTN_PAGES)))


def _merge_kernel(x_ref, gt_ref, co_ref, ao_ref, go_ref, wc_ref, wa_ref, o_ref, on_ref, *, n_heads):
    @pl.when(pl.program_id(1) == 0)
    def _():
        g = go_ref[...]
        for h in range(n_heads):
            cs = slice(h * LANES, (h + 1) * LANES)
            o = ao_ref[:, cs]
            on_ref[:, cs] = (o * lax.rsqrt(jnp.mean(o * o, axis=-1, keepdims=True) + EPS) * g).astype(BF16)

    y = jnp.dot(co_ref[...].astype(BF16), wc_ref[...].astype(BF16), preferred_element_type=F32)
    y = y + jnp.dot(on_ref[...], wa_ref[...].astype(BF16), preferred_element_type=F32)
    o_ref[...] = x_ref[...] + gt_ref[0] * y


def _merge(x, gt, conv_o, attn_o, g_o, w_out, tm, tn=512):
    m, d = x.shape
    g, r, _ = gt.shape
    d_conv = conv_o.shape[1]
    d_attn = attn_o.shape[1]
    tiles_per_group = (m // tm) // g
    rb = d_conv // d_attn
    return pl.pallas_call(
        functools.partial(_merge_kernel, n_heads=d_attn // LANES),
        grid=(m // tm, d // tn),
        in_specs=[pl.BlockSpec((tm, tn), lambda i, j: (i, j)),
                  pl.BlockSpec((1, r, tn), lambda i, j: (i // tiles_per_group, 0, j)),
                  pl.BlockSpec((tm, d_conv), lambda i, j: (i, 0)),
                  pl.BlockSpec((tm, d_attn), lambda i, j: (i, 0)),
                  pl.BlockSpec((1, LANES), lambda i, j: (0, 0)),
                  pl.BlockSpec((d_conv, tn), lambda i, j: (0, j)),
                  pl.BlockSpec((d_attn, tn), lambda i, j: (rb, j))],
        out_specs=pl.BlockSpec((tm, tn), lambda i, j: (i, j)),
        out_shape=jax.ShapeDtypeStruct((m, d), F32),
        scratch_shapes=[pltpu.VMEM((tm, d_attn), BF16)],
        compiler_params=_cparams("arbitrary", "arbitrary"),
        name="merge",
    )(x, gt, conv_o, attn_o, g_o.reshape(1, -1), w_out, w_out)


def _ffn_act(cur_g, p1_g, p2_g, cur_v, p1_v, p2_v, wg, wv, bg, bv):
    gate = p2_g * wg[0:1, :] + p1_g * wg[1:2, :] + cur_g * wg[2:3, :] + bg
    val = p2_v * wv[0:1, :] + p1_v * wv[1:2, :] + cur_v * wv[2:3, :] + bv
    return (_silu(gate) * val).astype(BF16)


def _ffn_finish(f, act, wd_ref, x_ref, gt_ref, o_ref, acc_ref):
    @pl.when(f == 0)
    def _():
        acc_ref[...] = jnp.zeros(acc_ref.shape, F32)

    acc_ref[...] += jnp.dot(act, wd_ref[...].astype(BF16), preferred_element_type=F32)

    @pl.when(f == pl.num_programs(1) - 1)
    def _():
        o_ref[...] = x_ref[...] + gt_ref[0] * acc_ref[...]


def _to_bf16_kernel(x_ref, o_ref):
    o_ref[...] = x_ref[...].astype(BF16)


def _to_bf16(w, cols_per_step=512):
    r, c = w.shape
    return pl.pallas_call(
        _to_bf16_kernel,
        grid=(c // cols_per_step,),
        in_specs=[pl.BlockSpec((r, cols_per_step), lambda j: (0, j))],
        out_specs=pl.BlockSpec((r, cols_per_step), lambda j: (0, j)),
        out_shape=jax.ShapeDtypeStruct((r, c), BF16),
        compiler_params=_cparams("arbitrary"),
        name="to_bf16",
    )(w)


FFN_HALO = 16
FFN_PARTS = 2


def _ffn_prompt_kernel(x_ref, xh_ref, g_ref, sc_ref, sh_ref, wug_ref, wuv_ref, wg_ref, wv_ref, bg_ref, bv_ref,
                       wd_ref, gt_ref, o_ref, tg_ref, tv_ref, h_ref, hist_g, hist_v, act_ref, *, tm, tiles_per_seq):
    i, f = pl.program_id(0), pl.program_id(1)

    @pl.when(f == 0)
    def _():
        _normmod_to(h_ref, 0, xh_ref, g_ref, sc_ref, sh_ref)
        _normmod_to(h_ref, FFN_HALO, x_ref, g_ref, sc_ref, sh_ref)
        o_ref[...] = jnp.zeros(o_ref.shape, F32)

    n_parts, _, pw = hist_g.shape
    for s in range(n_parts):
        cs = slice(s * pw, (s + 1) * pw)
        hist_g[s] = jnp.dot(h_ref[...], wug_ref[:, cs], preferred_element_type=F32)
        hist_v[s] = jnp.dot(h_ref[...], wuv_ref[:, cs], preferred_element_type=F32)

    seq_start = (i % tiles_per_seq) == 0
    chunk = 64
    row_idx = lax.broadcasted_iota(I32, (chunk, pw), 0)
    down = None
    for s in range(n_parts):
        cs = slice(s * pw, (s + 1) * pw)
        wg, wv, bg, bv = wg_ref[:, cs], wv_ref[:, cs], bg_ref[:, cs], bv_ref[:, cs]
        for r0 in range(0, tm, chunk):
            def rows(hist, k):
                u = hist[s, FFN_HALO + r0 - k:FFN_HALO + r0 - k + chunk, :]
                if r0 == 0 and k > 0:
                    u = jnp.where(row_idx < jnp.where(seq_start, k, 0), 0.0, u)
                return u
            act_ref[s, r0:r0 + chunk, :] = _ffn_act(rows(hist_g, 0), rows(hist_g, 1), rows(hist_g, 2),
                                                    rows(hist_v, 0), rows(hist_v, 1), rows(hist_v, 2),
                                                    wg, wv, bg, bv)
        tg_ref[0, :, cs] = hist_g[s, tm + FFN_HALO - SUBLANES:tm + FFN_HALO, :]
        tv_ref[0, :, cs] = hist_v[s, tm + FFN_HALO - SUBLANES:tm + FFN_HALO, :]
        part = jnp.dot(act_ref[s], wd_ref[cs, :], preferred_element_type=F32)
        down = part if down is None else down + part
    o_ref[...] += down

    @pl.when(f == pl.num_programs(1) - 1)
    def _():
        o_ref[...] = x_ref[...] + gt_ref[0] * o_ref[...]


def _ffn_prompt(x1, gain, sc, sh, gt, w_up_bf, w_dw, b_dw, w_down_bf, seq, tm, tf=512):
    m, d = x1.shape
    d_ff = w_down_bf.shape[0]
    nf = d_ff // tf
    tiles_per_seq = seq // tm
    fw = w_dw.shape[0]
    halo_idx = lambda i: jnp.maximum(i * (tm // FFN_HALO) - 1, 0)
    mod = pl.BlockSpec((1, 1, d), lambda i, f: (i // tiles_per_seq, 0, 0))
    tail = pl.BlockSpec((1, SUBLANES, tf), lambda i, f: (i, 0, f))
    return pl.pallas_call(
        functools.partial(_ffn_prompt_kernel, tm=tm, tiles_per_seq=tiles_per_seq),
        grid=(m // tm, nf),
        in_specs=[pl.BlockSpec((tm, d), lambda i, f: (i, 0)),
                  pl.BlockSpec((FFN_HALO, d), lambda i, f: (halo_idx(i), 0)),
                  pl.BlockSpec((1, d), lambda i, f: (0, 0)),
                  mod, mod,
                  pl.BlockSpec((d, tf), lambda i, f: (0, f)),
                  pl.BlockSpec((d, tf), lambda i, f: (0, f + nf)),
                  pl.BlockSpec((fw, tf), lambda i, f: (0, f)),
                  pl.BlockSpec((fw, tf), lambda i, f: (0, f + nf)),
                  pl.BlockSpec((1, tf), lambda i, f: (0, f)),
                  pl.BlockSpec((1, tf), lambda i, f: (0, f + nf)),
                  pl.BlockSpec((tf, d), lambda i, f: (f, 0)),
                  mod],
        out_specs=[pl.BlockSpec((tm, d), lambda i, f: (i, 0)), tail, tail],
        out_shape=[jax.ShapeDtypeStruct((m, d), F32),
                   jax.ShapeDtypeStruct((m // tm, SUBLANES, d_ff), F32),
                   jax.ShapeDtypeStruct((m // tm, SUBLANES, d_ff), F32)],
        scratch_shapes=[pltpu.VMEM((FFN_HALO + tm, d), BF16),
                        pltpu.VMEM((FFN_PARTS, FFN_HALO + tm, tf // FFN_PARTS), F32),
                        pltpu.VMEM((FFN_PARTS, FFN_HALO + tm, tf // FFN_PARTS), F32),
                        pltpu.VMEM((FFN_PARTS, tm, tf // FFN_PARTS), BF16)],
        compiler_params=_cparams("arbitrary", "arbitrary"),
        name="ffn_prompt",
    )(x1, x1, gain.reshape(1, d), sc, sh, w_up_bf, w_up_bf, w_dw, w_dw, b_dw.reshape(1, -1), b_dw.reshape(1, -1),
      w_down_bf, gt)


def _ffn_sample_kernel(cg_ref, cv_ref, p1g_ref, p1v_ref, p2g_ref, p2v_ref, wg_ref, wv_ref, bg_ref, bv_ref,
                       wd_ref, x_ref, gt_ref, o_ref, acc_ref):
    act = _ffn_act(cg_ref[...], p1g_ref[...], p2g_ref[...], cv_ref[...], p1v_ref[...], p2v_ref[...],
                   wg_ref[...], wv_ref[...], bg_ref[...], bv_ref[...])
    _ffn_finish(pl.program_id(1), act, wd_ref, x_ref, gt_ref, o_ref, acc_ref)


def _ffn_sample(cur, prev1, prev2, x1, gt, w_dw, b_dw, w_down, tf=512):
    m, d = x1.shape
    d_ff = w_down.shape[0]
    nf = d_ff // tf
    fw = w_dw.shape[0]
    lo = pl.BlockSpec((m, tf), lambda i, f: (0, f))
    hi = pl.BlockSpec((m, tf), lambda i, f: (0, f + nf))
    return pl.pallas_call(
        _ffn_sample_kernel,
        grid=(1, nf),
        in_specs=[lo, hi, lo, hi, lo, hi,
                  pl.BlockSpec((fw, tf), lambda i, f: (0, f)),
                  pl.BlockSpec((fw, tf), lambda i, f: (0, f + nf)),
                  pl.BlockSpec((1, tf), lambda i, f: (0, f)),
                  pl.BlockSpec((1, tf), lambda i, f: (0, f + nf)),
                  pl.BlockSpec((tf, d), lambda i, f: (f, 0)),
                  pl.BlockSpec((m, d), lambda i, f: (0, 0)),
                  pl.BlockSpec((1, m, d), lambda i, f: (0, 0, 0))],
        out_specs=pl.BlockSpec((m, d), lambda i, f: (0, 0)),
        out_shape=jax.ShapeDtypeStruct((m, d), F32),
        scratch_shapes=[pltpu.VMEM((m, d), F32)],
        compiler_params=_cparams("arbitrary", "arbitrary"),
        name="ffn_sample",
    )(cur, cur, prev1, prev1, prev2, prev2, w_dw, w_dw, b_dw.reshape(1, -1), b_dw.reshape(1, -1),
      w_down, x1, gt)


def _rope_tables(pos):
    def tab(dim):
        inv = jnp.power(ROPE_THETA, -jnp.arange(0, dim, 2, dtype=F32) / dim)
        ang = pos.astype(F32)[:, None] * inv[None, :]
        cos, sin = jnp.cos(ang), jnp.sin(ang)
        reps = LANES // dim
        return (jnp.tile(jnp.concatenate([cos, cos], axis=-1), (1, reps)),
                jnp.tile(jnp.concatenate([-sin, sin], axis=-1), (1, reps)))
    return tab(LANES) + tab(IDX_DIM)


def _in_proj(x2d, norm1, sc, sh, w_in_t, n_main, tm):
    w_tail = jnp.pad(w_in_t[n_main:], ((0, LANES - (w_in_t.shape[0] - n_main)), (0, 0)))
    z, zt = _normmod_matmul(x2d, norm1, sc, sh, w_in_t, n_main, tm, 512, "in_proj", w_transposed=True,
                            w_tail=w_tail)
    return z, zt


def kernel(x_prompt, x_sample, cache_k, cache_v, cache_kidx, state_conv, state_ffn, page_table, c_prompt, c_sample, norm1, w_ada, b_ada, w_in, g_q, g_k, w_dw_a, b_dw_a, gn_g, gn_b, g_o, w_out, norm2, w_up, w_dw_f, b_dw_f, w_down):
    n_b, seq, d = x_prompt.shape
    n_s, t_new, _ = x_sample.shape
    depth = norm1.shape[0]
    assert depth == 1
    head_dim = g_q.shape[-1]
    assert head_dim == LANES
    d_conv = w_dw_a.shape[-1]
    d_attn = w_out.shape[1] - d_conv
    n_heads = d_attn // head_dim
    kv_w = N_KV_HEADS * head_dim
    n_main = 2 * d_conv + d_attn + 2 * kv_w + N_IDX_HEADS * IDX_DIM
    n_pool, page = cache_k.shape[1], cache_k.shape[2]
    n_pages = page_table.shape[1]
    n_past = n_pages * page
    mp, ms = n_b * seq, n_s * t_new
    heads_per_kv = n_heads // N_KV_HEADS

    n_c = n_b + n_s
    pad_c = (-n_c) % SUBLANES
    c_all = jnp.concatenate([c_prompt, c_sample, jnp.zeros((pad_c, d), F32)], axis=0)
    mods = _ada(c_all, w_ada[0], b_ada[0])
    mp6 = mods[:n_b].reshape(n_b, 6, 1, d)
    sh1p, sc1p, gt1p, sh2p, sc2p, gt2p = [mp6[:, k] for k in range(6)]
    ms6 = jnp.repeat(mods[n_b:n_c].reshape(n_s, 6, d), t_new, axis=0)
    sh1s, sc1s, gt1s, sh2s, sc2s, gt2s = [ms6[:, k][None] for k in range(6)]

    w_in_t = jnp.swapaxes(w_in[0], 0, 1)

    xp = x_prompt.reshape(mp, d)
    tm_p = 1024 if seq % 1024 == 0 else QB
    z, zt = _in_proj(xp, norm1[0], sc1p, sh1p, w_in_t, n_main, tm_p)
    conv_o, new_conv_p = _conv_prompt(z, n_b, seq, d_conv, w_dw_a[0], b_dw_a[0], gn_g[0], gn_b[0],
                                      tm=256 if seq % 256 == 0 else QB)
    tabs_p = _rope_tables(jnp.arange(seq, dtype=I32))
    q_hm, qi_hm, k_p, k_bf, vt_bf, ki_p, ki_bf = _qk_epilogue(z, zt, tabs_p, g_q[0], g_k[0], n_b, seq, n_heads)
    w_hm = zt[:, IDX_DIM:IDX_DIM + N_IDX_HEADS].reshape(mp // QB, QB, N_IDX_HEADS).transpose(0, 2, 1)
    n_sel_p = min(TOPK_MAX, seq // 4)
    attn_p = _attn_prompt(qi_hm, w_hm, ki_bf, q_hm, k_bf, vt_bf, n_b, seq, n_sel_p)
    x1p = _merge(xp, gt1p, conv_o, attn_p, g_o[0], w_out[0],tm_p)
    w_up_bf, w_down_bf = _to_bf16(w_up[0]), _to_bf16(w_down[0])
    tm_f = 512 if seq % 512 == 0 else QB
    y_p, u_tail_g, u_tail_v = _ffn_prompt(x1p, norm2[0], sc2p, sh2p, gt2p, w_up_bf, w_dw_f[0], b_dw_f[0],
                                          w_down_bf, seq, tm_f)
    v_p = z[:, 2 * d_conv + d_attn + kv_w:2 * d_conv + d_attn + 2 * kv_w]
    fw = w_dw_f.shape[1]
    last_tile = (jnp.arange(n_b) + 1) * (seq // tm_f) - 1
    new_ffn_p = jnp.concatenate([u_tail_g[last_tile], u_tail_v[last_tile]], axis=-1)[:, SUBLANES - (fw - 1):]

    xs = x_sample.reshape(ms, d)
    zs, zts = _in_proj(xs, norm1[0], sc1s, sh1s, w_in_t, n_main, ms)
    conv_os, new_conv_s = _conv_sample(zs, state_conv[0], d_conv, w_dw_a[0], b_dw_a[0], gn_g[0], gn_b[0])
    pos_s = jnp.tile(n_past + jnp.arange(t_new, dtype=I32), ms // t_new)
    tabs_s = _rope_tables(pos_s)
    q_hm_s, qi_hm_s, k_s, k_bf_s, _, ki_s, ki_bf_s = _qk_epilogue(zs, zts, tabs_s, g_q[0], g_k[0], 1, ms, n_heads)
    v_s = zs[:, 2 * d_conv + d_attn + kv_w:2 * d_conv + d_attn + 2 * kv_w]
    seq_rows = lambda a, nh: (a.reshape(ms // QB, nh, QB // t_new, t_new, a.shape[-1])
                              .transpose(0, 2, 1, 3, 4).reshape(n_s, nh * t_new, a.shape[-1]))
    qi_s = seq_rows(qi_hm_s, N_IDX_HEADS)
    q_s = seq_rows(q_hm_s, n_heads)
    w_s = (zts[:, IDX_DIM:IDX_DIM + N_IDX_HEADS].reshape(n_s, t_new, N_IDX_HEADS)
           .transpose(0, 2, 1).reshape(n_s, N_IDX_HEADS * t_new, 1))
    ki_new_t = jnp.pad(ki_bf_s.reshape(n_s, t_new, IDX_DIM).transpose(0, 2, 1), ((0, 0), (0, 0), (0, page - t_new)))
    kv_new = lambda a: jnp.pad(a.reshape(n_s, t_new * N_KV_HEADS, head_dim),
                               ((0, 0), (0, (page - t_new) * N_KV_HEADS), (0, 0)))
    kidx_t = jnp.swapaxes(cache_kidx[0], 1, 2)
    sp, sn = _sample_scores(page_table, qi_s, w_s, ki_new_t, kidx_t, t_new)
    n_sel_s = min(TOPK_MAX, (n_past + t_new) // 4)
    mask = _sample_thresh(sp.reshape(ms, n_past), sn.reshape(ms, page), n_sel_s)
    o_s = _sample_attn(page_table, q_s, mask.reshape(n_s, t_new, -1), kv_new(k_bf_s), kv_new(v_s.astype(BF16)),
                       cache_k[0].reshape(n_pool, page * N_KV_HEADS, head_dim),
                       cache_v[0].reshape(n_pool, page * N_KV_HEADS, head_dim), t_new)
    attn_s = o_s.reshape(n_s, n_heads, t_new, head_dim).transpose(0, 2, 1, 3).reshape(ms, d_attn)
    x1s = _merge(xs, gt1s, conv_os, attn_s, g_o[0], w_out[0],ms)
    u_s = _normmod_matmul(x1s, norm2[0], sc2s, sh2s, w_up_bf, w_up.shape[-1], ms, 512, "up_proj")
    u_hist = jnp.concatenate([state_ffn[0], u_s.reshape(n_s, t_new, -1)], axis=1)
    shifted = lambda k: u_hist[:, k:k + t_new].reshape(ms, -1)
    y_s = _ffn_sample(shifted(2), shifted(1), shifted(0), x1s, gt2s, w_dw_f[0], b_dw_f[0], w_down_bf)
    new_ffn_s = u_hist[:, t_new:]

    return (y_p.reshape(n_b, seq, d), y_s.reshape(n_s, t_new, d),
            k_p.reshape(1, n_b, seq, N_KV_HEADS, head_dim), v_p.reshape(1, n_b, seq, N_KV_HEADS, head_dim),
            ki_p.reshape(1, n_b, seq, IDX_DIM), new_conv_p[None], new_ffn_p[None],
            k_s.reshape(1, n_s, t_new, N_KV_HEADS, head_dim), v_s.reshape(1, n_s, t_new, N_KV_HEADS, head_dim),
            ki_s.reshape(1, n_s, t_new, IDX_DIM), new_conv_s[None], new_ffn_s[None])
```
